```python
import math
import jax
import jax.numpy as jnp
from jax import lax
import numpy as np

D_MODEL = 1024
BATCH = 8
SEQ = 2048
DEPTH = 2

GRID_W = 64
CTX_LEN = 256
N_MIXERS = 2
N_MOD = 6
EPS = 1e-6

DA_HEADS = 8
DA_HEAD_DIM = 64
ROPE_BASE = 10000.0
Q_BLOCK = 128

GDN_HEADS = 8
GDN_HEAD_DIM = 128
GDN_WIDTH = GDN_HEADS * GDN_HEAD_DIM
CONV_K = 5
CHUNK = 64

N_EXPERTS = 16
EXPERT_FF = 2048
CAP_FACTOR = 2

kernel_name = 'hybrid_diffattn_gdn_ecmoe_dit'


def rmsnorm(x, g):
    xf = x.astype(jnp.float32)
    y = xf * lax.rsqrt(jnp.mean(xf * xf, axis=-1, keepdims=True) + EPS)
    return (y * g.astype(jnp.float32)).astype(x.dtype)


def l2norm(x):
    return x * lax.rsqrt(jnp.sum(x * x, axis=-1, keepdims=True) + EPS)


def modulate(h, g, shift, scale):
    return rmsnorm(h, g) * (1 + scale) + shift


def ada_mod(cvec, w_ada, b_ada):
    m = jax.nn.silu(cvec) @ w_ada + b_ada
    return jnp.split(m, N_MOD, axis=-1)


def axial_rope_angles(n):
    rows = n // GRID_W
    r = jnp.repeat(jnp.arange(rows), GRID_W).astype(jnp.float32)
    col = jnp.tile(jnp.arange(GRID_W), rows).astype(jnp.float32)
    half = DA_HEAD_DIM // 2
    inv = ROPE_BASE ** (-jnp.arange(0, half, 2, dtype=jnp.float32) / half)
    return r[:, None] * inv, col[:, None] * inv


def rope_pair(x, ang):
    cos = jnp.cos(ang)[:, None, None, :].astype(x.dtype)
    sin = jnp.sin(ang)[:, None, None, :].astype(x.dtype)
    m = ang.shape[-1]
    x1, x2 = x[..., :m], x[..., m:]
    return jnp.concatenate([x1 * cos - x2 * sin, x2 * cos + x1 * sin], axis=-1)


def apply_axial_rope(x, ang_r, ang_c):
    half = x.shape[-1] // 2
    return jnp.concatenate([rope_pair(x[..., :half], ang_r), rope_pair(x[..., half:], ang_c)], axis=-1)


def diff_attn_mixer(uc, ul, w_qkv, lambda_q1, lambda_k1, lambda_q2, lambda_k2, g_subln, w_o, depth, need_ctx):
    f32 = jnp.float32
    lam_init = 0.8 - 0.6 * math.exp(-0.3 * depth)
    lam = (jnp.exp(jnp.sum(lambda_q1.astype(f32) * lambda_k1.astype(f32)))
           - jnp.exp(jnp.sum(lambda_q2.astype(f32) * lambda_k2.astype(f32))) + lam_init)

    def project(u):
        B, N, _ = u.shape
        p = u @ w_qkv
        q = p[..., :D_MODEL].reshape(B, N, DA_HEADS, 2, DA_HEAD_DIM) * DA_HEAD_DIM ** -0.5
        k = p[..., D_MODEL:2 * D_MODEL].reshape(B, N, DA_HEADS, 2, DA_HEAD_DIM)
        v = p[..., 2 * D_MODEL:].reshape(B, N, DA_HEADS, 2 * DA_HEAD_DIM)
        return q, k, v

    qc, kc, vc = project(uc)
    ql, kl, vl = project(ul)
    ang_r, ang_c = axial_rope_angles(ul.shape[1])
    ql = apply_axial_rope(ql, ang_r, ang_c)
    kl = apply_axial_rope(kl, ang_r, ang_c)
    k_all = jnp.concatenate([kc, kl], axis=1)
    v_all = jnp.concatenate([vc, vl], axis=1)

    def attend(q, k, v):
        s = jnp.einsum('bqhcd,bkhcd->bhcqk', q, k).astype(f32)
        p = jax.nn.softmax(s, axis=-1)
        a = (p[:, :, 0] - lam * p[:, :, 1]).astype(v.dtype)
        return jnp.einsum('bhqk,bkhe->bqhe', a, v)

    def finish(o):
        B, N = o.shape[:2]
        o = rmsnorm(o, g_subln) * (1.0 - lam_init)
        return o.reshape(B, N, D_MODEL) @ w_o

    B, S = ql.shape[:2]
    nb = S // Q_BLOCK
    qb = jnp.moveaxis(ql.reshape(B, nb, Q_BLOCK, DA_HEADS, 2, DA_HEAD_DIM), 1, 0)
    ol = lax.map(lambda q: attend(q, k_all, v_all), qb)
    ol = jnp.moveaxis(ol, 0, 1).reshape(B, S, DA_HEADS, 2 * DA_HEAD_DIM)
    oc = finish(attend(qc, kc, vc)) if need_ctx else None
    return oc, finish(ol)


def short_conv(x, w):
    ch = x.shape[-1]
    y = lax.conv_general_dilated(x, w[:, None, :], window_strides=(1,), padding=[(CONV_K // 2, CONV_K // 2)],
                                 dimension_numbers=('NWC', 'WIO', 'NWC'), feature_group_count=ch)
    return jax.nn.silu(y)


def gated_delta_chunked(q, k, v, g, beta, s0):
    B, N, H, _ = q.shape
    nc = N // CHUNK

    def to_chunks(t):
        t = t.reshape((B, nc, CHUNK, H) + t.shape[3:])
        return jnp.moveaxis(t, (1, 3), (0, 2))

    q, k, v, g, beta = (to_chunks(t) for t in (q, k, v, g, beta))
    gc = jnp.cumsum(g, axis=-1)
    kb = k * beta[..., None]
    vb = v * beta[..., None]
    pos = jnp.arange(CHUNK)
    incl = pos[:, None] >= pos[None, :]
    strict = pos[:, None] > pos[None, :]
    gdiff = gc[..., :, None] - gc[..., None, :]
    decay = jnp.where(incl, jnp.exp(jnp.where(incl, gdiff, 0.0)), 0.0)
    a_mat = jnp.where(strict, jnp.einsum('nbhik,nbhjk->nbhij', kb, k) * decay, 0.0) + jnp.eye(CHUNK, dtype=q.dtype)
    u = lax.linalg.triangular_solve(a_mat, vb, left_side=True, lower=True, unit_diagonal=True)
    w = lax.linalg.triangular_solve(a_mat, kb * jnp.exp(gc)[..., None], left_side=True, lower=True, unit_diagonal=True)
    intra = jnp.einsum('nbhik,nbhjk->nbhij', q, k) * decay

    def step(s, xs):
        q_i, k_i, u_i, w_i, a_i, g_i = xs
        v_new = u_i - jnp.einsum('bhck,bhkv->bhcv', w_i, s)
        o_i = (jnp.einsum('bhck,bhkv->bhcv', q_i * jnp.exp(g_i)[..., None], s)
               + jnp.einsum('bhij,bhjv->bhiv', a_i, v_new))
        g_last = g_i[..., -1:]
        s = s * jnp.exp(g_last)[..., None] + jnp.einsum('bhck,bhcv->bhkv', k_i * jnp.exp(g_last - g_i)[..., None], v_new)
        return s, o_i

    s_fin, o = lax.scan(step, s0, (q, k, u, w, intra, gc))
    o = jnp.moveaxis(o, (0, 2), (1, 3)).reshape(B, N, H, v.shape[-1])
    return o, s_fin


def gdn_mixer(uc, ul, w_in, w_conv, a_log_f, dt_bias_f, a_log_b, dt_bias_b, g_onorm, w_o, need_ctx):
    f32 = jnp.float32

    def project(u):
        B, N, _ = u.shape
        p = u @ w_in
        qkv = short_conv(p[..., :3 * GDN_WIDTH], w_conv).astype(f32)
        q, k, v = (qkv[..., i * GDN_WIDTH:(i + 1) * GDN_WIDTH].reshape(B, N, GDN_HEADS, GDN_HEAD_DIM) for i in range(3))
        q = l2norm(q) * GDN_HEAD_DIM ** -0.5
        k = l2norm(k)
        z = p[..., 3 * GDN_WIDTH:4 * GDN_WIDTH].reshape(B, N, GDN_HEADS, GDN_HEAD_DIM)
        gates = p[..., 4 * GDN_WIDTH:].astype(f32).reshape(B, N, 4, GDN_HEADS)
        g_f = -jnp.exp(a_log_f.astype(f32)) * jax.nn.softplus(gates[..., 0, :] + dt_bias_f.astype(f32))
        g_b = -jnp.exp(a_log_b.astype(f32)) * jax.nn.softplus(gates[..., 2, :] + dt_bias_b.astype(f32))
        beta_f = jax.nn.sigmoid(gates[..., 1, :])
        beta_b = jax.nn.sigmoid(gates[..., 3, :])
        return q, k, v, z, g_f, beta_f, g_b, beta_b

    def flip(t):
        return jnp.flip(t, axis=1)

    qc, kc, vc, zc, gfc, bfc, gbc, bbc = project(uc)
    ql, kl, vl, zl, gfl, bfl, gbl, bbl = project(ul)
    s0 = jnp.zeros((ul.shape[0], GDN_HEADS, GDN_HEAD_DIM, GDN_HEAD_DIM), f32)
    oc_f, sc_f = gated_delta_chunked(qc, kc, vc, gfc, bfc, s0)
    oc_b, sc_b = gated_delta_chunked(flip(qc), flip(kc), flip(vc), flip(gbc), flip(bbc), s0)
    ol_f, _ = gated_delta_chunked(ql, kl, vl, gfl, bfl, sc_f)
    ol_b, _ = gated_delta_chunked(flip(ql), flip(kl), flip(vl), flip(gbl), flip(bbl), sc_b)

    def finish(o, z):
        B, N = o.shape[:2]
        o = (rmsnorm(o, g_onorm).astype(z.dtype) * jax.nn.silu(z)).reshape(B, N, GDN_WIDTH)
        return o @ w_o

    ol = finish(ol_f + flip(ol_b), zl)
    oc = finish(oc_f + flip(oc_b), zc) if need_ctx else None
    return oc, ol


def expert_choice_moe(u, w_router, w_gate, w_up, w_down):
    B, N, D = u.shape
    cap = CAP_FACTOR * N // N_EXPERTS
    aff = jax.nn.softmax(jnp.einsum('bnd,de->bne', u, w_router).astype(jnp.float32), axis=-1)
    gate_vals, idx = lax.top_k(jnp.swapaxes(aff, 1, 2), cap)
    xs = jax.vmap(lambda ub, ib: ub[ib])(u, idx)
    h = jax.nn.silu(jnp.einsum('becd,edf->becf', xs, w_gate)) * jnp.einsum('becd,edf->becf', xs, w_up)
    y = jnp.einsum('becf,efd->becd', h, w_down) * gate_vals[..., None].astype(u.dtype)
    return jax.vmap(lambda ib, yb: jnp.zeros((N, D), yb.dtype).at[ib.reshape(-1)].add(yb.reshape(-1, D)))(idx, y)


def setup_inputs(seed: int = 0) -> dict:
    key = jax.random.key(seed)
    keys = jax.random.split(key, 48)
    counter = [0]

    def nxt():
        k = keys[counter[0]]
        counter[0] += 1
        return k

    def normal(shape, scale):
        return scale * jax.random.normal(nxt(), shape, jnp.float32)

    def gain(n):
        return 1.0 + normal((n,), 0.1)

    def a_log():
        return jnp.log(jax.random.uniform(nxt(), (GDN_HEADS,), jnp.float32, 1.0, 16.0))

    def dt_bias():
        dt = jnp.exp(jax.random.uniform(nxt(), (GDN_HEADS,), jnp.float32, math.log(1e-3), math.log(1e-1)))
        return dt + jnp.log(-jnp.expm1(-dt))

    d = D_MODEL
    inp = {
        'x': normal((BATCH, SEQ, d), 1.0),
        'c': normal((BATCH, d), 1.0),
        'ctx': normal((BATCH, CTX_LEN, d), 1.0),
        'c_ctx': normal((d,), 1.0),
    }
    for li in range(2):
        p = 'l%d_' % li
        inp[p + 'w_ada'] = normal((d, N_MOD * d), 0.5 * d ** -0.5)
        inp[p + 'b_ada'] = normal((N_MOD * d,), 0.02)
        inp[p + 'g_pre_mix'] = gain(d)
        inp[p + 'g_post_mix'] = gain(d)
        inp[p + 'g_pre_ffn'] = gain(d)
        inp[p + 'g_post_ffn'] = gain(d)
        if li == 0:
            inp[p + 'w_qkv'] = normal((d, 3 * d), d ** -0.5)
            inp[p + 'lambda_q1'] = normal((DA_HEAD_DIM,), 0.1)
            inp[p + 'lambda_k1'] = normal((DA_HEAD_DIM,), 0.1)
            inp[p + 'lambda_q2'] = normal((DA_HEAD_DIM,), 0.1)
            inp[p + 'lambda_k2'] = normal((DA_HEAD_DIM,), 0.1)
            inp[p + 'g_subln'] = gain(2 * DA_HEAD_DIM)
            inp[p + 'w_o'] = normal((d, d), d ** -0.5)
        else:
            inp[p + 'w_in'] = normal((d, 4 * GDN_WIDTH + 4 * GDN_HEADS), d ** -0.5)
            inp[p + 'w_conv'] = normal((CONV_K, 3 * GDN_WIDTH), CONV_K ** -0.5)
            inp[p + 'a_log_f'] = a_log()
            inp[p + 'dt_bias_f'] = dt_bias()
            inp[p + 'a_log_b'] = a_log()
            inp[p + 'dt_bias_b'] = dt_bias()
            inp[p + 'g_onorm'] = gain(GDN_HEAD_DIM)
            inp[p + 'w_o'] = normal((GDN_WIDTH, d), GDN_WIDTH ** -0.5)
        inp[p + 'w_router'] = normal((d, N_EXPERTS), d ** -0.5)
        inp[p + 'w_gate'] = normal((N_EXPERTS, d, EXPERT_FF), d ** -0.5)
        inp[p + 'w_up'] = normal((N_EXPERTS, d, EXPERT_FF), d ** -0.5)
        inp[p + 'w_down'] = normal((N_EXPERTS, EXPERT_FF, d), EXPERT_FF ** -0.5)
    return inp


def reference(x, c, ctx, c_ctx,
              l0_w_ada, l0_b_ada, l0_g_pre_mix, l0_g_post_mix, l0_g_pre_ffn, l0_g_post_ffn,
              l0_w_qkv, l0_lambda_q1, l0_lambda_k1, l0_lambda_q2, l0_lambda_k2, l0_g_subln, l0_w_o,
              l0_w_router, l0_w_gate, l0_w_up, l0_w_down,
              l1_w_ada, l1_b_ada, l1_g_pre_mix, l1_g_post_mix, l1_g_pre_ffn, l1_g_post_ffn,
              l1_w_in, l1_w_conv, l1_a_log_f, l1_dt_bias_f, l1_a_log_b, l1_dt_bias_b, l1_g_onorm, l1_w_o,
              l1_w_router, l1_w_gate, l1_w_up, l1_w_down):
    commons = (
        (l0_w_ada, l0_b_ada, l0_g_pre_mix, l0_g_post_mix, l0_g_pre_ffn, l0_g_post_ffn, l0_w_router, l0_w_gate, l0_w_up, l0_w_down),
        (l1_w_ada, l1_b_ada, l1_g_pre_mix, l1_g_post_mix, l1_g_pre_ffn, l1_g_post_ffn, l1_w_router, l1_w_gate, l1_w_up, l1_w_down),
    )
    mixer_args = (
        (l0_w_qkv, l0_lambda_q1, l0_lambda_k1, l0_lambda_q2, l0_lambda_k2, l0_g_subln, l0_w_o),
        (l1_w_in, l1_w_conv, l1_a_log_f, l1_dt_bias_f, l1_a_log_b, l1_dt_bias_b, l1_g_onorm, l1_w_o),
    )
    hl, hc = x, ctx
    for i in range(DEPTH):
        w_ada, b_ada, g_pre_mix, g_post_mix, g_pre_ffn, g_post_ffn, w_router, w_gate, w_up, w_down = commons[i]
        need_ctx = i < DEPTH - 1
        sh_ml, sc_ml, gt_ml, sh_fl, sc_fl, gt_fl = ada_mod(c[:, None, :], w_ada, b_ada)
        sh_mc, sc_mc, gt_mc, sh_fc, sc_fc, gt_fc = ada_mod(c_ctx, w_ada, b_ada)
        uc = modulate(hc, g_pre_mix, sh_mc, sc_mc)
        ul = modulate(hl, g_pre_mix, sh_ml, sc_ml)
        if i % N_MIXERS == 0:
            oc, ol = diff_attn_mixer(uc, ul, *mixer_args[i], depth=i, need_ctx=need_ctx)
        else:
            oc, ol = gdn_mixer(uc, ul, *mixer_args[i], need_ctx=need_ctx)
        hl = hl + gt_ml * rmsnorm(ol, g_post_mix)
        fl = expert_choice_moe(modulate(hl, g_pre_ffn, sh_fl, sc_fl), w_router, w_gate, w_up, w_down)
        hl = hl + gt_fl * rmsnorm(fl, g_post_ffn)
        if need_ctx:
            hc = hc + gt_mc * rmsnorm(oc, g_post_mix)
            fc = expert_choice_moe(modulate(hc, g_pre_ffn, sh_fc, sc_fc), w_router, w_gate, w_up, w_down)
            hc = hc + gt_fc * rmsnorm(fc, g_post_ffn)
    return hl
```

```python
import functools
import math

import jax
import jax.numpy as jnp
from jax import lax
from jax.experimental import pallas as pl
from jax.experimental.pallas import tpu as pltpu

F32 = jnp.float32
BF16 = jnp.bfloat16
HIGHEST = lax.Precision.HIGHEST

EPS = 1e-6
N_MOD = 6
GRID_W = 64
ROPE_BASE = 10000.0
DA_HEADS = 8
DA_HEAD_DIM = 64
GDN_HEADS = 8
GDN_HEAD_DIM = 128
CONV_K = 5
N_EXPERTS = 16
CAP_FACTOR = 2

LANES = 128
ROW_TILE = 256
GDN_CHUNK = 64
GDN_HEADS_PER_STEP = 2
FF_CHUNK = 512
VMEM_LIMIT = 56 * 1024 * 1024

SH_M, SC_M, GT_M, SH_F, SC_F, GT_F = range(6)


def _cparams(*sem):
    return pltpu.CompilerParams(dimension_semantics=sem, vmem_limit_bytes=VMEM_LIMIT)


def _dot(a, b):
    return jnp.dot(a, b, preferred_element_type=F32)


def _dot_nt(a, b, precision=None):
    return lax.dot_general(a, b, (((1,), (1,)), ((), ())), preferred_element_type=F32, precision=precision)


def _dot_tn(a, b):
    return lax.dot_general(a, b, (((0,), (0,)), ((), ())), preferred_element_type=F32)


def _rms(x, g):
    return x * lax.rsqrt(jnp.mean(x * x, axis=-1, keepdims=True) + EPS) * g


def _sigmoid(x):
    return 1.0 / (1.0 + jnp.exp(-x))


def _silu(x):
    return x * _sigmoid(x)


def _softplus(x):
    return jnp.maximum(x, 0.0) + jnp.log(1.0 + jnp.exp(-jnp.abs(x)))


def _ada_body(c_ref, w_ref, b_ref, o_ref):
    o_ref[...] = jnp.dot(_silu(c_ref[...]), w_ref[...], precision=HIGHEST, preferred_element_type=F32) + b_ref[...]


def _ada_mods(c, c_ctx, w_ada, b_ada):
    bsz, d = c.shape
    rows = 16
    cc = jnp.concatenate([c, c_ctx[None, :], jnp.zeros((rows - bsz - 1, d), F32)], axis=0)
    tn = 1024
    m = pl.pallas_call(
        _ada_body,
        grid=(N_MOD * d // tn,),
        in_specs=[pl.BlockSpec((rows, d), lambda j: (0, 0)),
                  pl.BlockSpec((d, tn), lambda j: (0, j)),
                  pl.BlockSpec((1, tn), lambda j: (0, j))],
        out_specs=pl.BlockSpec((rows, tn), lambda j: (0, j)),
        out_shape=jax.ShapeDtypeStruct((rows, N_MOD * d), F32),
        compiler_params=_cparams("arbitrary"),
        name="ada_mods",
    )(cc, w_ada, b_ada[None, :])
    lat = m[:bsz].reshape(bsz, 1, N_MOD, d)
    ctx = jnp.broadcast_to(m[bsz].reshape(1, 1, N_MOD, d), (bsz, 1, N_MOD, d))
    return jnp.concatenate([lat, ctx], axis=1)


def _modulated(h_ref, mod_ref, g_ref, shift, scale):
    x = h_ref[0]
    return _rms(x, g_ref[...]) * (1.0 + mod_ref[0, 0, scale:scale + 1, :]) + mod_ref[0, 0, shift:shift + 1, :]


def _proj_body(h_ref, mod_ref, g_ref, w_ref, *rest, rope):
    if rope:
        cos_ref, sin_ref, o_ref, u_scr = rest
    else:
        o_ref, u_scr = rest
    j = pl.program_id(2)

    @pl.when(j == 0)
    def _():
        u_scr[...] = _modulated(h_ref, mod_ref, g_ref, SH_M, SC_M).astype(BF16)

    acc = _dot(u_scr[...], w_ref[...])
    if not rope:
        o_ref[0] = acc.astype(o_ref.dtype)
        return

    @pl.when(j == 2)
    def _():
        o_ref[0] = acc.astype(o_ref.dtype)

    @pl.when(j < 2)
    def _():
        reps = acc.shape[1] // LANES
        cos = jnp.tile(cos_ref[...], (1, reps))
        sin = jnp.tile(sin_ref[...], (1, reps))
        lane = lax.broadcasted_iota(jnp.int32, acc.shape, 1)
        half = DA_HEAD_DIM // 4
        first = (lane % (2 * half)) < half
        n = acc.shape[1]
        partner = jnp.where(first, pltpu.roll(acc, n - half, 1), pltpu.roll(acc, half, 1))
        r = acc * cos + partner * sin
        r = r * jnp.where(j == 0, DA_HEAD_DIM ** -0.5, 1.0)
        o_ref[0] = r.astype(o_ref.dtype)


def _project(h, mods, g, w_bf, n_lat, out_dtype, rope_tables=None, tn=1024):
    bsz, nt, d = h.shape
    n_out = w_bf.shape[1]
    lat_tiles = n_lat // ROW_TILE
    rope = rope_tables is not None
    in_specs = [pl.BlockSpec((1, ROW_TILE, d), lambda b, i, j: (b, i, 0)),
                pl.BlockSpec((1, 1, N_MOD, d), lambda b, i, j: (b, i // lat_tiles, 0, 0)),
                pl.BlockSpec((1, d), lambda b, i, j: (0, 0)),
                pl.BlockSpec((d, tn), lambda b, i, j: (0, j))]
    args = [h, mods, g[None, :], w_bf]
    if rope:
        in_specs += [pl.BlockSpec((ROW_TILE, LANES), lambda b, i, j: (i, 0))] * 2
        args += list(rope_tables)
    return pl.pallas_call(
        functools.partial(_proj_body, rope=rope),
        grid=(bsz, nt // ROW_TILE, n_out // tn),
        in_specs=in_specs,
        out_specs=pl.BlockSpec((1, ROW_TILE, tn), lambda b, i, j: (b, i, j)),
        out_shape=jax.ShapeDtypeStruct((bsz, nt, n_out), out_dtype),
        scratch_shapes=[pltpu.VMEM((ROW_TILE, d), BF16)],
        compiler_params=_cparams("arbitrary", "arbitrary", "arbitrary"),
        name="mod_project",
    )(*args)


def _rope_tables(n_lat, n_ctx):
    rows = n_lat // GRID_W
    r = jnp.repeat(jnp.arange(rows), GRID_W).astype(F32)
    col = jnp.tile(jnp.arange(GRID_W), rows).astype(F32)
    half = DA_HEAD_DIM // 2
    inv = ROPE_BASE ** (-jnp.arange(0, half, 2, dtype=F32) / half)
    ang_r, ang_c = r[:, None] * inv, col[:, None] * inv
    cos = jnp.concatenate([jnp.cos(ang_r)] * 2 + [jnp.cos(ang_c)] * 2, axis=-1)
    sin = jnp.concatenate([-jnp.sin(ang_r), jnp.sin(ang_r), -jnp.sin(ang_c), jnp.sin(ang_c)], axis=-1)
    cos = jnp.concatenate([jnp.tile(cos, (1, 2)), jnp.ones((n_ctx, LANES), F32)], axis=0)
    sin = jnp.concatenate([jnp.tile(sin, (1, 2)), jnp.zeros((n_ctx, LANES), F32)], axis=0)
    return cos, sin


def _attn_body(lam_ref, q_ref, k_ref, v_ref, gs_ref, o_ref, *, n_lat, lam_init):
    i = pl.program_id(2)
    lv = lam_ref[...]
    lam = (jnp.exp(jnp.sum(lv[0:1] * lv[1:2], axis=-1, keepdims=True))
           - jnp.exp(jnp.sum(lv[2:3] * lv[3:4], axis=-1, keepdims=True)) + lam_init)
    dh = DA_HEAD_DIM

    def attend(k, v):
        q = q_ref[0]

        def probs(c):
            s = _dot_nt(q[:, c * dh:(c + 1) * dh], k[:, c * dh:(c + 1) * dh])
            e = jnp.exp(s - jnp.max(s, axis=-1, keepdims=True))
            return e / jnp.sum(e, axis=-1, keepdims=True)

        a = (probs(0) - lam * probs(1)).astype(BF16)
        o = _dot(a, v)
        o_ref[0] = (_rms(o, gs_ref[...]) * (1.0 - lam_init)).astype(o_ref.dtype)

    @pl.when(i < n_lat // ROW_TILE)
    def _():
        attend(k_ref[0], v_ref[0])

    @pl.when(i >= n_lat // ROW_TILE)
    def _():
        attend(k_ref[0, n_lat:, :], v_ref[0, n_lat:, :])


def _diff_attention(qkv, lam_vecs, g_subln, n_lat, depth):
    bsz, nt, d3 = qkv.shape
    d = d3 // 3
    hw = 2 * DA_HEAD_DIM
    lam_init = 0.8 - 0.6 * math.exp(-0.3 * depth)
    return pl.pallas_call(
        functools.partial(_attn_body, n_lat=n_lat, lam_init=lam_init),
        grid=(bsz, DA_HEADS, nt // ROW_TILE),
        in_specs=[pl.BlockSpec((4, DA_HEAD_DIM), lambda b, h, i: (0, 0)),
                  pl.BlockSpec((1, ROW_TILE, hw), lambda b, h, i: (b, i, h)),
                  pl.BlockSpec((1, nt, hw), lambda b, h, i: (b, 0, DA_HEADS + h)),
                  pl.BlockSpec((1, nt, hw), lambda b, h, i: (b, 0, 2 * DA_HEADS + h)),
                  pl.BlockSpec((1, hw), lambda b, h, i: (0, 0))],
        out_specs=pl.BlockSpec((1, ROW_TILE, hw), lambda b, h, i: (b, i, h)),
        out_shape=jax.ShapeDtypeStruct((bsz, nt, d), BF16),
        compiler_params=_cparams("arbitrary", "arbitrary", "arbitrary"),
        name="diff_attention",
    )(lam_vecs, qkv, qkv, qkv, g_subln[None, :])


def _oproj_body(o_ref, w_ref, h_ref, mod_ref, g_ref, out_ref):
    y = _dot(o_ref[0], w_ref[...])
    out_ref[0] = h_ref[0] + mod_ref[0, 0, GT_M:GT_M + 1, :] * _rms(y, g_ref[...])


def _out_project(o, w_bf, h, mods, g_post, n_lat):
    bsz, nt, d = h.shape
    lat_tiles = n_lat // ROW_TILE
    return pl.pallas_call(
        _oproj_body,
        grid=(bsz, nt // ROW_TILE),
        in_specs=[pl.BlockSpec((1, ROW_TILE, d), lambda b, i: (b, i, 0)),
                  pl.BlockSpec((d, d), lambda b, i: (0, 0)),
                  pl.BlockSpec((1, ROW_TILE, d), lambda b, i: (b, i, 0)),
                  pl.BlockSpec((1, 1, N_MOD, d), lambda b, i: (b, i // lat_tiles, 0, 0)),
                  pl.BlockSpec((1, d), lambda b, i: (0, 0))],
        out_specs=pl.BlockSpec((1, ROW_TILE, d), lambda b, i: (b, i, 0)),
        out_shape=jax.ShapeDtypeStruct((bsz, nt, d), F32),
        compiler_params=_cparams("arbitrary", "arbitrary"),
        name="out_project",
    )(o, w_bf, h, mods, g_post[None, :])


def _router_body(h_ref, mod_ref, g_ref, wr_ref, u_ref, aff_ref):
    u = _modulated(h_ref, mod_ref, g_ref, SH_F, SC_F)
    u_ref[0] = u.astype(BF16)
    logits = _dot_nt(wr_ref[...], u, precision=HIGHEST)
    e = jnp.exp(logits - jnp.max(logits, axis=0, keepdims=True))
    aff_ref[0] = e / jnp.sum(e, axis=0, keepdims=True)


def _route(h, mods, g, w_router_t, n_lat):
    bsz, nt, d = h.shape
    lat_tiles = n_lat // ROW_TILE
    return pl.pallas_call(
        _router_body,
        grid=(bsz, nt // ROW_TILE),
        in_specs=[pl.BlockSpec((1, ROW_TILE, d), lambda b, i: (b, i, 0)),
                  pl.BlockSpec((1, 1, N_MOD, d), lambda b, i: (b, i // lat_tiles, 0, 0)),
                  pl.BlockSpec((1, d), lambda b, i: (0, 0)),
                  pl.BlockSpec((N_EXPERTS, d), lambda b, i: (0, 0))],
        out_specs=[pl.BlockSpec((1, ROW_TILE, d), lambda b, i: (b, i, 0)),
                   pl.BlockSpec((1, N_EXPERTS, ROW_TILE), lambda b, i: (b, 0, i))],
        out_shape=[jax.ShapeDtypeStruct((bsz, nt, d), BF16),
                   jax.ShapeDtypeStruct((bsz, N_EXPERTS, nt), F32)],
        compiler_params=_cparams("arbitrary", "arbitrary"),
        name="moe_router",
    )(h, mods, g[None, :], w_router_t)


def _lane_cumsum(x):
    n = x.shape[1]
    jj = lax.broadcasted_iota(jnp.int32, (ROW_TILE, ROW_TILE), 0)
    nn = lax.broadcasted_iota(jnp.int32, (ROW_TILE, ROW_TILE), 1)
    tri = (jj <= nn).astype(BF16)
    run = jnp.zeros((x.shape[0], 1), F32)
    parts = []
    for t in range(n // ROW_TILE):
        local = _dot(x[:, t * ROW_TILE:(t + 1) * ROW_TILE].astype(BF16), tri) + run
        parts.append(local)
        run = local[:, ROW_TILE - 1:ROW_TILE]
    return jnp.concatenate(parts, axis=1) if len(parts) > 1 else parts[0]


def _select_slots(aff, cap):
    bits = pltpu.bitcast(aff, jnp.int32)

    def step(t, lo):
        cand = lo | jnp.left_shift(jnp.int32(1), 30 - t)
        cnt = jnp.sum((bits >= cand).astype(F32), axis=1, keepdims=True)
        return jnp.where(cnt >= cap, cand, lo)

    thr = lax.fori_loop(0, 31, step, jnp.zeros((aff.shape[0], 1), jnp.int32))
    gt = bits > thr
    eq = bits == thr
    need = cap - jnp.sum(gt.astype(F32), axis=1, keepdims=True)
    eq_rank = _lane_cumsum(eq.astype(F32))
    sel = gt | (eq & (eq_rank <= need))
    slot = _lane_cumsum(sel.astype(F32)) - 1.0
    return jnp.where(sel, slot, -1.0).astype(jnp.int32)


def _select_body(aff_ref, pos_ref, *, n_lat, n_ctx):
    aff = aff_ref[0]
    pos_ref[0, :, :n_lat] = _select_slots(aff[:, :n_lat], CAP_FACTOR * n_lat // N_EXPERTS)
    if n_ctx:
        pos_ref[0, :, n_lat:] = _select_slots(aff[:, n_lat:], CAP_FACTOR * n_ctx // N_EXPERTS)


def _select(aff, n_lat, n_ctx):
    bsz, e, nt = aff.shape
    return pl.pallas_call(
        functools.partial(_select_body, n_lat=n_lat, n_ctx=n_ctx),
        grid=(bsz,),
        in_specs=[pl.BlockSpec((1, e, nt), lambda b: (b, 0, 0))],
        out_specs=pl.BlockSpec((1, e, nt), lambda b: (b, 0, 0)),
        out_shape=jax.ShapeDtypeStruct((bsz, e, nt), jnp.int32),
        compiler_params=_cparams("arbitrary"),
        name="moe_select",
    )(aff)


def _one_hot_slots(pos_row, cap):
    slot = lax.broadcasted_iota(jnp.int32, (cap, pos_row.shape[1]), 0)
    return pos_row == slot


def _expert_body(pos_ref, aff_ref, u_ref, wg_ref, wu_ref, wd_ref, y_ref, xs_scr, gate_scr, *, segments):
    pos = pos_ref[0, 0]
    aff = aff_ref[0, 0]
    row = 0
    for start, n, cap in segments:
        hot = _one_hot_slots(pos[:, start:start + n], cap)
        xs_scr[row:row + cap, :] = _dot(jnp.where(hot, 1.0, 0.0).astype(BF16), u_ref[0, start:start + n, :]).astype(BF16)
        gate_scr[row:row + cap, :] = jnp.sum(jnp.where(hot, aff[:, start:start + n], 0.0), axis=1, keepdims=True)
        row += cap
    xs = xs_scr[...]
    ff = wg_ref.shape[2]
    acc = jnp.zeros((xs.shape[0], wd_ref.shape[2]), F32)
    for c in range(ff // FF_CHUNK):
        cols = slice(c * FF_CHUNK, (c + 1) * FF_CHUNK)
        hid = _silu(_dot(xs, wg_ref[0, :, cols])) * _dot(xs, wu_ref[0, :, cols])
        acc = acc + _dot(hid.astype(BF16), wd_ref[0, cols, :])
    y_ref[0, 0] = (acc * gate_scr[...]).astype(y_ref.dtype)


def _segments(n_lat, n_ctx):
    segs = [(0, n_lat, CAP_FACTOR * n_lat // N_EXPERTS)]
    if n_ctx:
        segs.append((n_lat, n_ctx, CAP_FACTOR * n_ctx // N_EXPERTS))
    return tuple(segs)


def _experts(pos, aff, u, wg, wu, wd, n_lat, n_ctx):
    bsz, nt, d = u.shape
    e, _, ff = wg.shape
    segs = _segments(n_lat, n_ctx)
    rows = sum(s[2] for s in segs)
    return pl.pallas_call(
        functools.partial(_expert_body, segments=segs),
        grid=(e, bsz),
        in_specs=[pl.BlockSpec((1, 1, 1, nt), lambda x, b: (b, x, 0, 0)),
                  pl.BlockSpec((1, 1, 1, nt), lambda x, b: (b, x, 0, 0)),
                  pl.BlockSpec((1, nt, d), lambda x, b: (b, 0, 0)),
                  pl.BlockSpec((1, d, ff), lambda x, b: (x, 0, 0)),
                  pl.BlockSpec((1, d, ff), lambda x, b: (x, 0, 0)),
                  pl.BlockSpec((1, ff, d), lambda x, b: (x, 0, 0))],
        out_specs=pl.BlockSpec((1, 1, rows, d), lambda x, b: (b, x, 0, 0)),
        out_shape=jax.ShapeDtypeStruct((bsz, e, rows, d), BF16),
        scratch_shapes=[pltpu.VMEM((rows, d), BF16), pltpu.VMEM((rows, 1), F32)],
        compiler_params=_cparams("arbitrary", "arbitrary"),
        name="moe_experts",
    )(pos.reshape(bsz, e, 1, nt), aff.reshape(bsz, e, 1, nt), u, wg, wu, wd)


def _combine_body(pos_ref, y_ref, out_ref, *, segments):
    x = pl.program_id(1)

    @pl.when(x == 0)
    def _():
        out_ref[...] = jnp.zeros_like(out_ref)

    pos = pos_ref[0, 0]
    row = 0
    for start, n, cap in segments:
        y = y_ref[0, 0, row:row + cap, :]
        for t in range(start, start + n, ROW_TILE):
            hot = jnp.where(_one_hot_slots(pos[:, t:t + ROW_TILE], cap), 1.0, 0.0).astype(BF16)
            out_ref[0, t:t + ROW_TILE, :] += _dot_tn(hot, y)
        row += cap


def _combine(pos, y, n_lat, n_ctx):
    bsz, e, rows, d = y.shape
    nt = n_lat + n_ctx
    return pl.pallas_call(
        functools.partial(_combine_body, segments=_segments(n_lat, n_ctx)),
        grid=(bsz, e),
        in_specs=[pl.BlockSpec((1, 1, 1, nt), lambda b, x: (b, x, 0, 0)),
                  pl.BlockSpec((1, 1, rows, d), lambda b, x: (b, x, 0, 0))],
        out_specs=pl.BlockSpec((1, nt, d), lambda b, x: (b, 0, 0)),
        out_shape=jax.ShapeDtypeStruct((bsz, nt, d), F32),
        compiler_params=_cparams("arbitrary", "arbitrary"),
        name="moe_combine",
    )(pos.reshape(bsz, e, 1, nt), y)


def _post_body(h_ref, y_ref, mod_ref, g_ref, out_ref, *, gate):
    out_ref[0] = h_ref[0] + mod_ref[0, 0, gate:gate + 1, :] * _rms(y_ref[0], g_ref[...])


def _gated_residual(h, y, mods, g, gate, n_lat):
    bsz, nt, d = y.shape
    lat_tiles = n_lat // ROW_TILE
    return pl.pallas_call(
        functools.partial(_post_body, gate=gate),
        grid=(bsz, nt // ROW_TILE),
        in_specs=[pl.BlockSpec((1, ROW_TILE, d), lambda b, i: (b, i, 0)),
                  pl.BlockSpec((1, ROW_TILE, d), lambda b, i: (b, i, 0)),
                  pl.BlockSpec((1, 1, N_MOD, d), lambda b, i: (b, i // lat_tiles, 0, 0)),
                  pl.BlockSpec((1, d), lambda b, i: (0, 0))],
        out_specs=pl.BlockSpec((1, ROW_TILE, d), lambda b, i: (b, i, 0)),
        out_shape=jax.ShapeDtypeStruct((bsz, nt, d), F32),
        compiler_params=_cparams("arbitrary", "arbitrary"),
        name="gated_residual",
    )(h, y, mods, g[None, :])


def _moe(h, mods, g_pre, g_post, w_router, wg, wu, wd, n_lat, n_ctx):
    u, aff = _route(h[:, :n_lat + n_ctx], mods, g_pre, w_router.T, n_lat)
    pos = _select(aff, n_lat, n_ctx)
    y = _experts(pos, aff, u, wg, wu, wd, n_lat, n_ctx)
    f = _combine(pos, y, n_lat, n_ctx)
    return _gated_residual(h, f, mods, g_post, GT_F, n_lat)


def _conv_body(x_ref, w_ref, o_ref, *, n_lat):
    j = pl.program_id(1)
    x = x_ref[0]
    nt = x.shape[0]
    t = lax.broadcasted_iota(jnp.int32, x.shape, 0)
    seg_lo = jnp.where(t < n_lat, 0, n_lat)
    seg_hi = jnp.where(t < n_lat, n_lat, nt)
    acc = jnp.zeros_like(x)
    for tap in range(CONV_K):
        d = tap - CONV_K // 2
        shifted = x if d == 0 else pltpu.roll(x, (-d) % nt, 0)
        ok = (t + d >= seg_lo) & (t + d < seg_hi)
        acc = acc + jnp.where(ok, shifted, 0.0) * w_ref[tap:tap + 1, :]
    y = _silu(acc)
    unit = y * lax.rsqrt(jnp.sum(y * y, axis=-1, keepdims=True) + EPS)
    y = jnp.where(j < GDN_HEADS, unit * GDN_HEAD_DIM ** -0.5, jnp.where(j < 2 * GDN_HEADS, unit, y))
    o_ref[0] = y


def _short_conv(p, w_conv, n_lat):
    bsz, nt, _ = p.shape
    nblk = 3 * GDN_HEADS
    return pl.pallas_call(
        functools.partial(_conv_body, n_lat=n_lat),
        grid=(bsz, nblk),
        in_specs=[pl.BlockSpec((1, nt, LANES), lambda b, j: (b, 0, j)),
                  pl.BlockSpec((CONV_K, LANES), lambda b, j: (0, j))],
        out_specs=pl.BlockSpec((1, nt, LANES), lambda b, j: (b, 0, j)),
        out_shape=jax.ShapeDtypeStruct((bsz, nt, nblk * LANES), F32),
        compiler_params=_cparams("arbitrary", "arbitrary"),
        name="gdn_conv",
    )(p, w_conv)


def _gdn_body(q_ref, k_ref, v_ref, gate_ref, gate_t_ref, alog_ref, dt_ref, alog_t_ref, dt_t_ref,
              o_ref, s_scr, gct_scr, *, rev, n_lat_groups):
    hb = pl.program_id(1)
    step = pl.program_id(2)
    r_t = ROW_TILE
    c_sz = GDN_CHUNK
    goff = 2 * GDN_HEADS if rev else 0

    @pl.when(step == 0)
    def _():
        s_scr[...] = jnp.zeros_like(s_scr)

    ii = lax.broadcasted_iota(jnp.int32, (r_t, r_t), 0)
    jj = lax.broadcasted_iota(jnp.int32, (r_t, r_t), 1)
    same = (ii // c_sz) == (jj // c_sz)
    if rev:
        incl, strict = same & (jj >= ii), same & (jj > ii)
    else:
        incl, strict = same & (jj <= ii), same & (jj < ii)
    m_incl = incl.astype(F32)
    m_same = same.astype(F32)
    eye = (ii == jj).astype(F32)

    lane = lax.broadcasted_iota(jnp.int32, (1, LANES), 1)
    is_g = (lane >= goff) & (lane < goff + GDN_HEADS)
    raw = gate_ref[0]
    g_all = jnp.where(is_g, -jnp.exp(alog_ref[...]) * _softplus(raw + dt_ref[...]), 0.0)
    beta_all = _sigmoid(raw)
    gc_all = jnp.dot(m_incl, g_all, precision=HIGHEST, preferred_element_type=F32)
    gl_all = jnp.dot(m_same, g_all, precision=HIGHEST, preferred_element_type=F32)
    g_t = -jnp.exp(alog_t_ref[...]) * _softplus(gate_t_ref[0] + dt_t_ref[...])
    gct_scr[...] = _dot_nt(g_t, m_incl, precision=HIGHEST)
    lane_f = lax.broadcasted_iota(jnp.int32, (r_t, LANES), 1)

    def column(a, idx):
        return jnp.sum(jnp.where(lane_f == idx, a, 0.0), axis=1, keepdims=True)

    for hh in range(GDN_HEADS_PER_STEP):
        head = hb * GDN_HEADS_PER_STEP + hh
        cols = slice(hh * GDN_HEAD_DIM, (hh + 1) * GDN_HEAD_DIM)
        q, k, v = q_ref[0, :, cols], k_ref[0, :, cols], v_ref[0, :, cols]
        gc = column(gc_all, goff + head)
        gl = column(gl_all, goff + head)
        beta = column(beta_all, goff + GDN_HEADS + head)
        gc_row = gct_scr[pl.ds(goff + head, 1), :]
        decay = jnp.where(incl, jnp.exp(jnp.where(incl, gc - gc_row, 0.0)), 0.0)
        kb = k * beta
        k_bf = k.astype(BF16)
        a_mat = jnp.where(strict, _dot_nt(kb.astype(BF16), k_bf) * decay, 0.0)
        inv = eye - a_mat
        pw = a_mat
        for _ in range(int(math.log2(c_sz)) - 1):
            pw_bf = pw.astype(BF16)
            pw = _dot(pw_bf, pw_bf)
            inv = inv + _dot(inv.astype(BF16), pw.astype(BF16))
        rhs = jnp.concatenate([v * beta, kb * jnp.exp(gc)], axis=1).astype(BF16)
        uw = _dot(inv.astype(BF16), rhs)
        u_mat, w_mat = uw[:, :GDN_HEAD_DIM], uw[:, GDN_HEAD_DIM:]
        intra = (_dot_nt(q.astype(BF16), k_bf) * decay).astype(BF16)
        q_dec = (q * jnp.exp(gc)).astype(BF16)
        k_dec_t = (k * jnp.exp(gl - gc)).T.astype(BF16)
        s = s_scr[hh]
        n_chunks = r_t // c_sz
        outs = [None] * n_chunks
        for c in (range(n_chunks - 1, -1, -1) if rev else range(n_chunks)):
            rows = slice(c * c_sz, (c + 1) * c_sz)
            s_bf = s.astype(BF16)
            v_new = (u_mat[rows] - _dot(w_mat[rows].astype(BF16), s_bf)).astype(BF16)
            outs[c] = _dot(q_dec[rows], s_bf) + _dot(intra[rows, rows], v_new)
            s = s * jnp.exp(gl[c * c_sz:c * c_sz + 1, :]) + _dot(k_dec_t[:, rows], v_new)
        s_scr[hh] = s
        o_ref[0, :, cols] = jnp.concatenate(outs, axis=0)


def _gdn_scan(qkv, gates, gates_t, a_log_row, dt_row, a_log_col, dt_col, n_lat, rev):
    bsz, nt, _ = qkv.shape
    hps = GDN_HEADS_PER_STEP
    hblocks = GDN_HEADS // hps
    n_groups = nt // ROW_TILE
    lat_groups = n_lat // ROW_TILE
    width = hps * GDN_HEAD_DIM

    def group(i):
        lat = (lat_groups - i) if rev else (i - 1)
        return jnp.where(i == 0, n_groups - 1, lat)

    return pl.pallas_call(
        functools.partial(_gdn_body, rev=rev, n_lat_groups=lat_groups),
        grid=(bsz, hblocks, n_groups),
        in_specs=[pl.BlockSpec((1, ROW_TILE, width), lambda b, h, i: (b, group(i), h)),
                  pl.BlockSpec((1, ROW_TILE, width), lambda b, h, i: (b, group(i), hblocks + h)),
                  pl.BlockSpec((1, ROW_TILE, width), lambda b, h, i: (b, group(i), 2 * hblocks + h)),
                  pl.BlockSpec((1, ROW_TILE, LANES), lambda b, h, i: (b, group(i), 0)),
                  pl.BlockSpec((1, 4 * GDN_HEADS, ROW_TILE), lambda b, h, i: (b, 0, group(i))),
                  pl.BlockSpec((1, LANES), lambda b, h, i: (0, 0)),
                  pl.BlockSpec((1, LANES), lambda b, h, i: (0, 0)),
                  pl.BlockSpec((4 * GDN_HEADS, 1), lambda b, h, i: (0, 0)),
                  pl.BlockSpec((4 * GDN_HEADS, 1), lambda b, h, i: (0, 0))],
        out_specs=pl.BlockSpec((1, ROW_TILE, width), lambda b, h, i: (b, group(i), h)),
        out_shape=jax.ShapeDtypeStruct((bsz, nt, GDN_HEADS * GDN_HEAD_DIM), F32),
        scratch_shapes=[pltpu.VMEM((hps, GDN_HEAD_DIM, GDN_HEAD_DIM), F32),
                        pltpu.VMEM((4 * GDN_HEADS, ROW_TILE), F32)],
        compiler_params=_cparams("arbitrary", "arbitrary", "arbitrary"),
        name="gdn_scan_bwd" if rev else "gdn_scan_fwd",
    )(qkv, qkv, qkv, gates, gates_t, a_log_row, dt_row, a_log_col, dt_col)


def _gdn_out_body(of_ref, ob_ref, z_ref, gn_ref, w_ref, h_ref, mod_ref, g_ref, out_ref, a_scr):
    for hh in range(GDN_HEADS):
        cols = slice(hh * GDN_HEAD_DIM, (hh + 1) * GDN_HEAD_DIM)
        o = of_ref[0, :, cols] + ob_ref[0, :, cols]
        a_scr[:, cols] = (_rms(o, gn_ref[...]) * _silu(z_ref[0, :, cols])).astype(BF16)
    y = _dot(a_scr[...], w_ref[...])
    out_ref[0] = h_ref[0] + mod_ref[0, 0, GT_M:GT_M + 1, :] * _rms(y, g_ref[...])


def _gdn_out(o_f, o_b, p, g_onorm, w_bf, h, mods, g_post, n_lat):
    bsz, _, d = h.shape
    width = GDN_HEADS * GDN_HEAD_DIM
    return pl.pallas_call(
        _gdn_out_body,
        grid=(bsz, n_lat // ROW_TILE),
        in_specs=[pl.BlockSpec((1, ROW_TILE, width), lambda b, i: (b, i, 0)),
                  pl.BlockSpec((1, ROW_TILE, width), lambda b, i: (b, i, 0)),
                  pl.BlockSpec((1, ROW_TILE, width), lambda b, i: (b, i, 3)),
                  pl.BlockSpec((1, GDN_HEAD_DIM), lambda b, i: (0, 0)),
                  pl.BlockSpec((width, d), lambda b, i: (0, 0)),
                  pl.BlockSpec((1, ROW_TILE, d), lambda b, i: (b, i, 0)),
                  pl.BlockSpec((1, 1, N_MOD, d), lambda b, i: (b, 0, 0, 0)),
                  pl.BlockSpec((1, d), lambda b, i: (0, 0))],
        out_specs=pl.BlockSpec((1, ROW_TILE, d), lambda b, i: (b, i, 0)),
        out_shape=jax.ShapeDtypeStruct((bsz, n_lat, d), F32),
        scratch_shapes=[pltpu.VMEM((ROW_TILE, width), BF16)],
        compiler_params=_cparams("arbitrary", "arbitrary"),
        name="gdn_out",
    )(o_f, o_b, p, g_onorm[None, :], w_bf, h, mods, g_post[None, :])


def _gate_vectors(a_log_f, dt_bias_f, a_log_b, dt_bias_b):
    z = jnp.zeros((GDN_HEADS,), F32)
    a = jnp.concatenate([a_log_f, z, a_log_b, z])
    d = jnp.concatenate([dt_bias_f, z, dt_bias_b, z])
    pad = jnp.zeros((LANES - 4 * GDN_HEADS,), F32)
    return (jnp.concatenate([a, pad])[None, :], jnp.concatenate([d, pad])[None, :], a[:, None], d[:, None])


def kernel(x, c, ctx, c_ctx, l0_w_ada, l0_b_ada, l0_g_pre_mix, l0_g_post_mix, l0_g_pre_ffn, l0_g_post_ffn, l0_w_qkv, l0_lambda_q1, l0_lambda_k1, l0_lambda_q2, l0_lambda_k2, l0_g_subln, l0_w_o, l0_w_router, l0_w_gate, l0_w_up, l0_w_down, l1_w_ada, l1_b_ada, l1_g_pre_mix, l1_g_post_mix, l1_g_pre_ffn, l1_g_post_ffn, l1_w_in, l1_w_conv, l1_a_log_f, l1_dt_bias_f, l1_a_log_b, l1_dt_bias_b, l1_g_onorm, l1_w_o, l1_w_router, l1_w_gate, l1_w_up, l1_w_down):
    n_lat, n_ctx = x.shape[1], ctx.shape[1]
    h = jnp.concatenate([x, ctx], axis=1)

    mods = _ada_mods(c, c_ctx, l0_w_ada, l0_b_ada)
    qkv = _project(h, mods, l0_g_pre_mix, l0_w_qkv.astype(BF16), n_lat, BF16, _rope_tables(n_lat, n_ctx))
    lam_vecs = jnp.stack([l0_lambda_q1, l0_lambda_k1, l0_lambda_q2, l0_lambda_k2])
    o = _diff_attention(qkv, lam_vecs, l0_g_subln, n_lat, depth=0)
    h = _out_project(o, l0_w_o.astype(BF16), h, mods, l0_g_post_mix, n_lat)
    h = _moe(h, mods, l0_g_pre_ffn, l0_g_post_ffn, l0_w_router, l0_w_gate.astype(BF16), l0_w_up.astype(BF16),
             l0_w_down.astype(BF16), n_lat, n_ctx)

    mods = _ada_mods(c, c_ctx, l1_w_ada, l1_b_ada)
    width = GDN_HEADS * GDN_HEAD_DIM
    p = _project(h, mods, l1_g_pre_mix, l1_w_in[:, :4 * width].astype(BF16), n_lat, F32)
    w_gates = jnp.pad(l1_w_in[:, 4 * width:], ((0, 0), (0, LANES - 4 * GDN_HEADS))).astype(BF16)
    gates = _project(h, mods, l1_g_pre_mix, w_gates, n_lat, F32, tn=LANES)
    gates_t = jnp.swapaxes(gates[:, :, :4 * GDN_HEADS], 1, 2)
    qkv = _short_conv(p, l1_w_conv, n_lat)
    gv = _gate_vectors(l1_a_log_f, l1_dt_bias_f, l1_a_log_b, l1_dt_bias_b)
    o_f = _gdn_scan(qkv, gates, gates_t, *gv, n_lat, rev=False)
    o_b = _gdn_scan(qkv, gates, gates_t, *gv, n_lat, rev=True)
    hl = _gdn_out(o_f, o_b, p, l1_g_onorm, l1_w_o.astype(BF16), h, mods, l1_g_post_mix, n_lat)
    return _moe(hl, mods, l1_g_pre_ffn, l1_g_post_ffn, l1_w_router, l1_w_gate.astype(BF16), l1_w_up.astype(BF16),
                l1_w_down.astype(BF16), n_lat, 0)
```

```python
import functools
import math

import jax
import jax.numpy as jnp
from jax import lax
from jax.experimental import pallas as pl
from jax.experimental.pallas import tpu as pltpu

F32 = jnp.float32
BF16 = jnp.bfloat16
HIGHEST = lax.Precision.HIGHEST

EPS = 1e-6
N_MOD = 6
GRID_W = 64
ROPE_BASE = 10000.0
DA_HEADS = 8
DA_HEAD_DIM = 64
GDN_HEADS = 8
GDN_HEAD_DIM = 128
CONV_K = 5
N_EXPERTS = 16
CAP_FACTOR = 2

LANES = 128
ROW_TILE = 256
PROJ_ROWS = 768
ATTN_Q_ROWS = 512
ATTN_CHAIN_ROWS = 128
Q_PRESCALE = DA_HEAD_DIM ** -0.5 * math.log2(math.e)
GDN_CHUNK = 64
GDN_HEADS_PER_STEP = 8
FF_CHUNK = 512
VMEM_LIMIT = 56 * 1024 * 1024

SH_M, SC_M, GT_M, SH_F, SC_F, GT_F = range(6)


def _cparams(*sem):
    return pltpu.CompilerParams(dimension_semantics=sem, vmem_limit_bytes=VMEM_LIMIT)


def _dot(a, b):
    return jnp.dot(a, b, preferred_element_type=F32)


def _dot_nt(a, b, precision=None):
    return lax.dot_general(a, b, (((1,), (1,)), ((), ())), preferred_element_type=F32, precision=precision)


def _dot_tn(a, b):
    return lax.dot_general(a, b, (((0,), (0,)), ((), ())), preferred_element_type=F32)


def _rms(x, g):
    return x * lax.rsqrt(jnp.mean(x * x, axis=-1, keepdims=True) + EPS) * g


def _sigmoid(x):
    return 1.0 / (1.0 + jnp.exp(-x))


def _silu(x):
    return x * _sigmoid(x)


def _softplus(x):
    return jnp.maximum(x, 0.0) + jnp.log(1.0 + jnp.exp(-jnp.abs(x)))


def _ada_body(c_ref, w_ref, b_ref, o_ref):
    o_ref[...] = jnp.dot(_silu(c_ref[...]), w_ref[...], precision=HIGHEST, preferred_element_type=F32) + b_ref[...]


def _ada_mods(c, c_ctx, w_ada, b_ada):
    bsz, d = c.shape
    rows = 16
    cc = jnp.concatenate([c, c_ctx[None, :], jnp.zeros((rows - bsz - 1, d), F32)], axis=0)
    tn = 1024
    m = pl.pallas_call(
        _ada_body,
        grid=(N_MOD * d // tn,),
        in_specs=[pl.BlockSpec((rows, d), lambda j: (0, 0)),
                  pl.BlockSpec((d, tn), lambda j: (0, j)),
                  pl.BlockSpec((1, tn), lambda j: (0, j))],
        out_specs=pl.BlockSpec((rows, tn), lambda j: (0, j)),
        out_shape=jax.ShapeDtypeStruct((rows, N_MOD * d), F32),
        compiler_params=_cparams("arbitrary"),
        name="ada_mods",
    )(cc, w_ada, b_ada[None, :])
    lat = m[:bsz].reshape(bsz, 1, N_MOD, d)
    ctx = jnp.broadcast_to(m[bsz].reshape(1, 1, N_MOD, d), (bsz, 1, N_MOD, d))
    return jnp.concatenate([lat, ctx], axis=1)


def _modulated(h_ref, mod_ref, g_ref, shift, scale):
    x = h_ref[0]
    return _rms(x, g_ref[...]) * (1.0 + mod_ref[0, 0, scale:scale + 1, :]) + mod_ref[0, 0, shift:shift + 1, :]


def _proj_body(h_ref, mod_ref, g_ref, w_ref, *rest, n_lat, rope, gates):
    rest = list(rest)
    cos_ref, sin_ref = (rest.pop(0), rest.pop(0)) if rope else (None, None)
    wg_ref = rest.pop(0) if gates else None
    o_ref = rest.pop(0)
    og_ref = rest.pop(0) if gates else None
    x = h_ref[0]
    tm, d = x.shape
    row = pl.program_id(1) * tm + lax.broadcasted_iota(jnp.int32, (tm, 1), 0)
    is_ctx = row >= n_lat
    scale = jnp.where(is_ctx, mod_ref[0, 1, SC_M:SC_M + 1, :], mod_ref[0, 0, SC_M:SC_M + 1, :])
    shift = jnp.where(is_ctx, mod_ref[0, 1, SH_M:SH_M + 1, :], mod_ref[0, 0, SH_M:SH_M + 1, :])
    u = (_rms(x, g_ref[...]) * (1.0 + scale) + shift).astype(BF16)
    if gates:
        og_ref[0] = _dot(u, wg_ref[...])
    for j in range(w_ref.shape[1] // d):
        cols = slice(j * d, (j + 1) * d)
        acc = _dot(u, w_ref[:, cols])
        if rope and j < 2:
            reps = d // LANES
            cos = jnp.tile(cos_ref[...], (1, reps))
            sin = jnp.tile(sin_ref[...], (1, reps))
            lane = lax.broadcasted_iota(jnp.int32, acc.shape, 1)
            half = DA_HEAD_DIM // 4
            first = (lane % (2 * half)) < half
            partner = jnp.where(first, pltpu.roll(acc, d - half, 1), pltpu.roll(acc, half, 1))
            acc = acc * cos + partner * sin
            if j == 0:
                acc = acc * Q_PRESCALE
        o_ref[0, :, cols] = acc.astype(o_ref.dtype)


def _project(h, mods, g, w_bf, n_lat, rope_tables=None, w_gates=None):
    bsz, nt, d = h.shape
    n_out = w_bf.shape[1]
    tm = PROJ_ROWS
    rope = rope_tables is not None
    gates = w_gates is not None
    in_specs = [pl.BlockSpec((1, tm, d), lambda b, i: (b, i, 0)),
                pl.BlockSpec((1, 2, N_MOD, d), lambda b, i: (b, 0, 0, 0)),
                pl.BlockSpec((1, d), lambda b, i: (0, 0)),
                pl.BlockSpec((d, n_out), lambda b, i: (0, 0))]
    args = [h, mods, g[None, :], w_bf]
    out_specs = [pl.BlockSpec((1, tm, n_out), lambda b, i: (b, i, 0))]
    out_shape = [jax.ShapeDtypeStruct((bsz, nt, n_out), BF16)]
    if rope:
        in_specs += [pl.BlockSpec((tm, LANES), lambda b, i: (i, 0))] * 2
        args += list(rope_tables)
    if gates:
        in_specs.append(pl.BlockSpec((d, LANES), lambda b, i: (0, 0)))
        args.append(w_gates)
        out_specs.append(pl.BlockSpec((1, tm, LANES), lambda b, i: (b, i, 0)))
        out_shape.append(jax.ShapeDtypeStruct((bsz, nt, LANES), F32))
    out = pl.pallas_call(
        functools.partial(_proj_body, n_lat=n_lat, rope=rope, gates=gates),
        grid=(bsz, nt // tm),
        in_specs=in_specs,
        out_specs=out_specs,
        out_shape=out_shape,
        compiler_params=_cparams("arbitrary", "arbitrary"),
        name="mod_project",
    )(*args)
    return out if gates else out[0]


def _rope_tables(n_lat, n_ctx):
    rows = n_lat // GRID_W
    r = jnp.repeat(jnp.arange(rows), GRID_W).astype(F32)
    col = jnp.tile(jnp.arange(GRID_W), rows).astype(F32)
    half = DA_HEAD_DIM // 2
    inv = ROPE_BASE ** (-jnp.arange(0, half, 2, dtype=F32) / half)
    ang_r, ang_c = r[:, None] * inv, col[:, None] * inv
    cos = jnp.concatenate([jnp.cos(ang_r)] * 2 + [jnp.cos(ang_c)] * 2, axis=-1)
    sin = jnp.concatenate([-jnp.sin(ang_r), jnp.sin(ang_r), -jnp.sin(ang_c), jnp.sin(ang_c)], axis=-1)
    cos = jnp.concatenate([jnp.tile(cos, (1, 2)), jnp.ones((n_ctx, LANES), F32)], axis=0)
    sin = jnp.concatenate([jnp.tile(sin, (1, 2)), jnp.zeros((n_ctx, LANES), F32)], axis=0)
    return cos, sin


def _attn_body(lam_ref, q_ref, k_ref, v_ref, gs_ref, o_ref, *, lam_init):
    lv = lam_ref[...]
    lam = (jnp.exp(jnp.sum(lv[0:1] * lv[1:2], axis=-1, keepdims=True))
           - jnp.exp(jnp.sum(lv[2:3] * lv[3:4], axis=-1, keepdims=True)) + lam_init)
    q, k, v = q_ref[0], k_ref[0], v_ref[0]
    lane = lax.broadcasted_iota(jnp.int32, q.shape, 1)
    hw = v.shape[1]
    v_one = jnp.concatenate([v, jnp.ones_like(v)], axis=1)

    rows = min(q.shape[0], ATTN_CHAIN_ROWS)
    chains = [(r, c) for r in range(0, q.shape[0], rows) for c in range(2)]
    qm = [jnp.where((lane >= c * DA_HEAD_DIM) & (lane < (c + 1) * DA_HEAD_DIM), q, jnp.zeros_like(q)) for c in range(2)]
    s = [_dot_nt(qm[c][r:r + rows], k) for r, c in chains]
    e = [jnp.exp2((x - jnp.max(x, axis=-1, keepdims=True)).astype(BF16)) for x in s]
    ov = [_dot(x, v_one) for x in e]
    att = [x[:, :hw] / x[:, hw:hw + 1] for x in ov]
    for n in range(0, len(chains), 2):
        r = chains[n][0]
        o = att[n] - lam * att[n + 1]
        o_ref[0, r:r + rows, :] = (_rms(o, gs_ref[...]) * (1.0 - lam_init)).astype(o_ref.dtype)


def _diff_attention(qkv, lam_vecs, g_subln, n_lat, depth):
    bsz, nt, d3 = qkv.shape
    d = d3 // 3
    hw = 2 * DA_HEAD_DIM
    lam_init = 0.8 - 0.6 * math.exp(-0.3 * depth)
    n_ctx = nt - n_lat

    def call(n_q, tq, q_blk0, n_k, k_blk0):
        return pl.pallas_call(
            functools.partial(_attn_body, lam_init=lam_init),
            grid=(bsz, DA_HEADS, n_q // tq),
            in_specs=[pl.BlockSpec((4, DA_HEAD_DIM), lambda b, h, i: (0, 0)),
                      pl.BlockSpec((1, tq, hw), lambda b, h, i: (b, q_blk0 + i, h)),
                      pl.BlockSpec((1, n_k, hw), lambda b, h, i: (b, k_blk0, DA_HEADS + h)),
                      pl.BlockSpec((1, n_k, hw), lambda b, h, i: (b, k_blk0, 2 * DA_HEADS + h)),
                      pl.BlockSpec((1, hw), lambda b, h, i: (0, 0))],
            out_specs=pl.BlockSpec((1, tq, hw), lambda b, h, i: (b, i, h)),
            out_shape=jax.ShapeDtypeStruct((bsz, n_q, d), BF16),
            compiler_params=_cparams("arbitrary", "arbitrary", "arbitrary"),
            name="diff_attention",
        )(lam_vecs, qkv, qkv, qkv, g_subln[None, :])

    o_lat = call(n_lat, ATTN_Q_ROWS, 0, nt, 0)
    o_ctx = call(n_ctx, n_ctx, n_lat // n_ctx, n_ctx, n_lat // n_ctx)
    return jnp.concatenate([o_lat, o_ctx], axis=1)


def _oproj_body(o_ref, w_ref, h_ref, mod_ref, g_ref, out_ref):
    y = _dot(o_ref[0], w_ref[...])
    out_ref[0] = h_ref[0] + mod_ref[0, 0, GT_M:GT_M + 1, :] * _rms(y, g_ref[...])


def _out_project(o, w_bf, h, mods, g_post, n_lat):
    bsz, nt, d = h.shape
    lat_tiles = n_lat // ROW_TILE
    return pl.pallas_call(
        _oproj_body,
        grid=(bsz, nt // ROW_TILE),
        in_specs=[pl.BlockSpec((1, ROW_TILE, d), lambda b, i: (b, i, 0)),
                  pl.BlockSpec((d, d), lambda b, i: (0, 0)),
                  pl.BlockSpec((1, ROW_TILE, d), lambda b, i: (b, i, 0)),
                  pl.BlockSpec((1, 1, N_MOD, d), lambda b, i: (b, i // lat_tiles, 0, 0)),
                  pl.BlockSpec((1, d), lambda b, i: (0, 0))],
        out_specs=pl.BlockSpec((1, ROW_TILE, d), lambda b, i: (b, i, 0)),
        out_shape=jax.ShapeDtypeStruct((bsz, nt, d), F32),
        compiler_params=_cparams("arbitrary", "arbitrary"),
        name="out_project",
    )(o, w_bf, h, mods, g_post[None, :])


def _router_body(h_ref, mod_ref, g_ref, wr_ref, u_ref, aff_ref):
    u = _modulated(h_ref, mod_ref, g_ref, SH_F, SC_F)
    u_ref[0] = u.astype(BF16)
    logits = _dot_nt(wr_ref[...], u, precision=HIGHEST)
    e = jnp.exp(logits - jnp.max(logits, axis=0, keepdims=True))
    aff_ref[0] = e / jnp.sum(e, axis=0, keepdims=True)


def _route(h, mods, g, w_router_t, n_lat):
    bsz, nt, d = h.shape
    lat_tiles = n_lat // ROW_TILE
    return pl.pallas_call(
        _router_body,
        grid=(bsz, nt // ROW_TILE),
        in_specs=[pl.BlockSpec((1, ROW_TILE, d), lambda b, i: (b, i, 0)),
                  pl.BlockSpec((1, 1, N_MOD, d), lambda b, i: (b, i // lat_tiles, 0, 0)),
                  pl.BlockSpec((1, d), lambda b, i: (0, 0)),
                  pl.BlockSpec((N_EXPERTS, d), lambda b, i: (0, 0))],
        out_specs=[pl.BlockSpec((1, ROW_TILE, d), lambda b, i: (b, i, 0)),
                   pl.BlockSpec((1, N_EXPERTS, ROW_TILE), lambda b, i: (b, 0, i))],
        out_shape=[jax.ShapeDtypeStruct((bsz, nt, d), BF16),
                   jax.ShapeDtypeStruct((bsz, N_EXPERTS, nt), F32)],
        compiler_params=_cparams("arbitrary", "arbitrary"),
        name="moe_router",
    )(h, mods, g[None, :], w_router_t)


def _lane_cumsum(x):
    n = x.shape[1]
    jj = lax.broadcasted_iota(jnp.int32, (ROW_TILE, ROW_TILE), 0)
    nn = lax.broadcasted_iota(jnp.int32, (ROW_TILE, ROW_TILE), 1)
    tri = (jj <= nn).astype(BF16)
    run = jnp.zeros((x.shape[0], 1), F32)
    parts = []
    for t in range(n // ROW_TILE):
        local = _dot(x[:, t * ROW_TILE:(t + 1) * ROW_TILE].astype(BF16), tri) + run
        parts.append(local)
        run = local[:, ROW_TILE - 1:ROW_TILE]
    return jnp.concatenate(parts, axis=1) if len(parts) > 1 else parts[0]


def _select_slots(aff, cap):
    bits = pltpu.bitcast(aff, jnp.int32)

    def step(t, lo):
        cand = lo | jnp.left_shift(jnp.int32(1), 30 - t)
        cnt = jnp.sum((bits >= cand).astype(F32), axis=1, keepdims=True)
        return jnp.where(cnt >= cap, cand, lo)

    thr = lax.fori_loop(0, 31, step, jnp.zeros((aff.shape[0], 1), jnp.int32))
    gt = bits > thr
    eq = bits == thr
    need = cap - jnp.sum(gt.astype(F32), axis=1, keepdims=True)
    eq_rank = _lane_cumsum(eq.astype(F32))
    sel = gt | (eq & (eq_rank <= need))
    slot = _lane_cumsum(sel.astype(F32)) - 1.0
    return jnp.where(sel, slot, -1.0).astype(jnp.int32)


def _select_body(aff_ref, pos_ref, *, n_lat, n_ctx):
    aff = aff_ref[0]
    pos_ref[0, :, :n_lat] = _select_slots(aff[:, :n_lat], CAP_FACTOR * n_lat // N_EXPERTS)
    if n_ctx:
        pos_ref[0, :, n_lat:] = _select_slots(aff[:, n_lat:], CAP_FACTOR * n_ctx // N_EXPERTS)


def _select(aff, n_lat, n_ctx):
    bsz, e, nt = aff.shape
    return pl.pallas_call(
        functools.partial(_select_body, n_lat=n_lat, n_ctx=n_ctx),
        grid=(bsz,),
        in_specs=[pl.BlockSpec((1, e, nt), lambda b: (b, 0, 0))],
        out_specs=pl.BlockSpec((1, e, nt), lambda b: (b, 0, 0)),
        out_shape=jax.ShapeDtypeStruct((bsz, e, nt), jnp.int32),
        compiler_params=_cparams("arbitrary"),
        name="moe_select",
    )(aff)


def _one_hot_slots(pos_row, cap):
    slot = lax.broadcasted_iota(jnp.int32, (cap, pos_row.shape[1]), 0)
    return pos_row == slot


def _expert_body(pos_ref, aff_ref, u_ref, wg_ref, wu_ref, wd_ref, y_ref, xs_scr, gate_scr, *, segments):
    pos = pos_ref[0, 0]
    aff = aff_ref[0, 0]
    row = 0
    for start, n, cap in segments:
        hot = _one_hot_slots(pos[:, start:start + n], cap)
        xs_scr[row:row + cap, :] = _dot(jnp.where(hot, 1.0, 0.0).astype(BF16), u_ref[0, start:start + n, :]).astype(BF16)
        gate_scr[row:row + cap, :] = jnp.sum(jnp.where(hot, aff[:, start:start + n], 0.0), axis=1, keepdims=True)
        row += cap
    xs = xs_scr[...]
    ff = wg_ref.shape[2]
    acc = jnp.zeros((xs.shape[0], wd_ref.shape[2]), F32)
    for c in range(ff // FF_CHUNK):
        cols = slice(c * FF_CHUNK, (c + 1) * FF_CHUNK)
        hid = _silu(_dot(xs, wg_ref[0, :, cols])) * _dot(xs, wu_ref[0, :, cols])
        acc = acc + _dot(hid.astype(BF16), wd_ref[0, cols, :])
    y_ref[0, 0] = (acc * gate_scr[...]).astype(y_ref.dtype)


def _segments(n_lat, n_ctx):
    segs = [(0, n_lat, CAP_FACTOR * n_lat // N_EXPERTS)]
    if n_ctx:
        segs.append((n_lat, n_ctx, CAP_FACTOR * n_ctx // N_EXPERTS))
    return tuple(segs)


def _experts(pos, aff, u, wg, wu, wd, n_lat, n_ctx):
    bsz, nt, d = u.shape
    e, _, ff = wg.shape
    segs = _segments(n_lat, n_ctx)
    rows = sum(s[2] for s in segs)
    return pl.pallas_call(
        functools.partial(_expert_body, segments=segs),
        grid=(e, bsz),
        in_specs=[pl.BlockSpec((1, 1, 1, nt), lambda x, b: (b, x, 0, 0)),
                  pl.BlockSpec((1, 1, 1, nt), lambda x, b: (b, x, 0, 0)),
                  pl.BlockSpec((1, nt, d), lambda x, b: (b, 0, 0)),
                  pl.BlockSpec((1, d, ff), lambda x, b: (x, 0, 0)),
                  pl.BlockSpec((1, d, ff), lambda x, b: (x, 0, 0)),
                  pl.BlockSpec((1, ff, d), lambda x, b: (x, 0, 0))],
        out_specs=pl.BlockSpec((1, 1, rows, d), lambda x, b: (b, x, 0, 0)),
        out_shape=jax.ShapeDtypeStruct((bsz, e, rows, d), BF16),
        scratch_shapes=[pltpu.VMEM((rows, d), BF16), pltpu.VMEM((rows, 1), F32)],
        compiler_params=_cparams("arbitrary", "arbitrary"),
        name="moe_experts",
    )(pos.reshape(bsz, e, 1, nt), aff.reshape(bsz, e, 1, nt), u, wg, wu, wd)


def _combine_body(pos_ref, y_ref, out_ref, *, segments):
    x = pl.program_id(1)

    @pl.when(x == 0)
    def _():
        out_ref[...] = jnp.zeros_like(out_ref)

    pos = pos_ref[0, 0]
    row = 0
    for start, n, cap in segments:
        y = y_ref[0, 0, row:row + cap, :]
        for t in range(start, start + n, ROW_TILE):
            hot = jnp.where(_one_hot_slots(pos[:, t:t + ROW_TILE], cap), 1.0, 0.0).astype(BF16)
            out_ref[0, t:t + ROW_TILE, :] += _dot_tn(hot, y)
        row += cap


def _combine(pos, y, n_lat, n_ctx):
    bsz, e, rows, d = y.shape
    nt = n_lat + n_ctx
    return pl.pallas_call(
        functools.partial(_combine_body, segments=_segments(n_lat, n_ctx)),
        grid=(bsz, e),
        in_specs=[pl.BlockSpec((1, 1, 1, nt), lambda b, x: (b, x, 0, 0)),
                  pl.BlockSpec((1, 1, rows, d), lambda b, x: (b, x, 0, 0))],
        out_specs=pl.BlockSpec((1, nt, d), lambda b, x: (b, 0, 0)),
        out_shape=jax.ShapeDtypeStruct((bsz, nt, d), F32),
        compiler_params=_cparams("arbitrary", "arbitrary"),
        name="moe_combine",
    )(pos.reshape(bsz, e, 1, nt), y)


def _post_body(h_ref, y_ref, mod_ref, g_ref, out_ref, *, gate):
    out_ref[0] = h_ref[0] + mod_ref[0, 0, gate:gate + 1, :] * _rms(y_ref[0], g_ref[...])


def _gated_residual(h, y, mods, g, gate, n_lat):
    bsz, nt, d = y.shape
    lat_tiles = n_lat // ROW_TILE
    return pl.pallas_call(
        functools.partial(_post_body, gate=gate),
        grid=(bsz, nt // ROW_TILE),
        in_specs=[pl.BlockSpec((1, ROW_TILE, d), lambda b, i: (b, i, 0)),
                  pl.BlockSpec((1, ROW_TILE, d), lambda b, i: (b, i, 0)),
                  pl.BlockSpec((1, 1, N_MOD, d), lambda b, i: (b, i // lat_tiles, 0, 0)),
                  pl.BlockSpec((1, d), lambda b, i: (0, 0))],
        out_specs=pl.BlockSpec((1, ROW_TILE, d), lambda b, i: (b, i, 0)),
        out_shape=jax.ShapeDtypeStruct((bsz, nt, d), F32),
        compiler_params=_cparams("arbitrary", "arbitrary"),
        name="gated_residual",
    )(h, y, mods, g[None, :])


def _moe(h, mods, g_pre, g_post, w_router, wg, wu, wd, n_lat, n_ctx):
    u, aff = _route(h[:, :n_lat + n_ctx], mods, g_pre, w_router.T, n_lat)
    pos = _select(aff, n_lat, n_ctx)
    y = _experts(pos, aff, u, wg, wu, wd, n_lat, n_ctx)
    f = _combine(pos, y, n_lat, n_ctx)
    return _gated_residual(h, f, mods, g_post, GT_F, n_lat)


def _conv_body(x_ref, w_ref, o_ref, *, n_lat):
    j = pl.program_id(1)
    x = x_ref[0].astype(F32)
    nt = x.shape[0]
    t = lax.broadcasted_iota(jnp.int32, x.shape, 0)
    seg_lo = jnp.where(t < n_lat, 0, n_lat)
    seg_hi = jnp.where(t < n_lat, n_lat, nt)
    acc = jnp.zeros_like(x)
    for tap in range(CONV_K):
        d = tap - CONV_K // 2
        shifted = x if d == 0 else pltpu.roll(x, (-d) % nt, 0)
        ok = (t + d >= seg_lo) & (t + d < seg_hi)
        acc = acc + jnp.where(ok, shifted, 0.0) * w_ref[tap:tap + 1, :]
    y = _silu(acc)
    unit = y * lax.rsqrt(jnp.sum(y * y, axis=-1, keepdims=True) + EPS)
    y = jnp.where(j < GDN_HEADS, unit * GDN_HEAD_DIM ** -0.5, jnp.where(j < 2 * GDN_HEADS, unit, y))
    o_ref[0] = y


def _short_conv(p, w_conv, n_lat):
    bsz, nt, _ = p.shape
    nblk = 3 * GDN_HEADS
    return pl.pallas_call(
        functools.partial(_conv_body, n_lat=n_lat),
        grid=(bsz, nblk),
        in_specs=[pl.BlockSpec((1, nt, LANES), lambda b, j: (b, 0, j)),
                  pl.BlockSpec((CONV_K, LANES), lambda b, j: (0, j))],
        out_specs=pl.BlockSpec((1, nt, LANES), lambda b, j: (b, 0, j)),
        out_shape=jax.ShapeDtypeStruct((bsz, nt, nblk * LANES), F32),
        compiler_params=_cparams("arbitrary", "arbitrary"),
        name="gdn_conv",
    )(p, w_conv)


def _gdn_body(q_ref, k_ref, v_ref, gate_ref, gate_t_ref, alog_ref, dt_ref, alog_t_ref, dt_t_ref,
              o_ref, s_scr, gct_scr, *, rev, n_lat_groups):
    hb = pl.program_id(1)
    step = pl.program_id(2)
    r_t = ROW_TILE
    c_sz = GDN_CHUNK
    goff = 2 * GDN_HEADS if rev else 0

    @pl.when(step == 0)
    def _():
        s_scr[...] = jnp.zeros_like(s_scr)

    ii = lax.broadcasted_iota(jnp.int32, (r_t, r_t), 0)
    jj = lax.broadcasted_iota(jnp.int32, (r_t, r_t), 1)
    same = (ii // c_sz) == (jj // c_sz)
    if rev:
        incl, strict = same & (jj >= ii), same & (jj > ii)
    else:
        incl, strict = same & (jj <= ii), same & (jj < ii)
    m_incl = incl.astype(F32)
    m_same = same.astype(F32)
    eye = (ii == jj).astype(F32)

    lane = lax.broadcasted_iota(jnp.int32, (1, LANES), 1)
    is_g = (lane >= goff) & (lane < goff + GDN_HEADS)
    raw = gate_ref[0]
    g_all = jnp.where(is_g, -jnp.exp(alog_ref[...]) * _softplus(raw + dt_ref[...]), 0.0)
    beta_all = _sigmoid(raw)
    gc_all = jnp.dot(m_incl, g_all, precision=HIGHEST, preferred_element_type=F32)
    gl_all = jnp.dot(m_same, g_all, precision=HIGHEST, preferred_element_type=F32)
    g_t = -jnp.exp(alog_t_ref[...]) * _softplus(gate_t_ref[0] + dt_t_ref[...])
    gct_scr[...] = _dot_nt(g_t, m_incl, precision=HIGHEST)
    lane_f = lax.broadcasted_iota(jnp.int32, (r_t, LANES), 1)

    def column(a, idx):
        return jnp.sum(jnp.where(lane_f == idx, a, 0.0), axis=1, keepdims=True)

    heads = range(GDN_HEADS_PER_STEP)
    n_chunks = r_t // c_sz
    n_levels = int(math.log2(c_sz)) - 1
    st = []
    for hh in heads:
        head = hb * GDN_HEADS_PER_STEP + hh
        cols = slice(hh * GDN_HEAD_DIM, (hh + 1) * GDN_HEAD_DIM)
        q, k, v = q_ref[0, :, cols], k_ref[0, :, cols], v_ref[0, :, cols]
        gc = column(gc_all, goff + head)
        gl = column(gl_all, goff + head)
        beta = column(beta_all, goff + GDN_HEADS + head)
        gc_row = gct_scr[pl.ds(goff + head, 1), :]
        decay = jnp.where(incl, jnp.exp(jnp.where(incl, gc - gc_row, 0.0)), 0.0)
        kb = k * beta
        k_bf = k.astype(BF16)
        a_mat = jnp.where(strict, _dot_nt(kb.astype(BF16), k_bf) * decay, 0.0)
        st.append(dict(
            cols=cols, gl=gl, inv=eye - a_mat, pw=a_mat,
            rhs=jnp.concatenate([v * beta, kb * jnp.exp(gc)], axis=1).astype(BF16),
            intra=(_dot_nt(q.astype(BF16), k_bf) * decay).astype(BF16),
            q_dec=(q * jnp.exp(gc)).astype(BF16),
            k_dec_t=(k * jnp.exp(gl - gc)).T.astype(BF16),
            s=s_scr[hh], outs=[None] * n_chunks))
    for _ in range(n_levels):
        for t in st:
            pw_bf = t["pw"].astype(BF16)
            t["pw"] = _dot(pw_bf, pw_bf)
        for t in st:
            t["inv"] = t["inv"] + _dot(t["inv"].astype(BF16), t["pw"].astype(BF16))
    for t in st:
        uw = _dot(t["inv"].astype(BF16), t["rhs"])
        t["u"], t["w"] = uw[:, :GDN_HEAD_DIM], uw[:, GDN_HEAD_DIM:].astype(BF16)
    for c in (range(n_chunks - 1, -1, -1) if rev else range(n_chunks)):
        rows = slice(c * c_sz, (c + 1) * c_sz)
        for t in st:
            t["s_bf"] = t["s"].astype(BF16)
            t["v_new"] = (t["u"][rows] - _dot(t["w"][rows], t["s_bf"])).astype(BF16)
        for t in st:
            t["outs"][c] = _dot(t["q_dec"][rows], t["s_bf"]) + _dot(t["intra"][rows, rows], t["v_new"])
            t["s"] = (t["s"] * jnp.exp(t["gl"][c * c_sz:c * c_sz + 1, :])
                      + _dot(t["k_dec_t"][:, rows], t["v_new"]))
    for hh, t in enumerate(st):
        s_scr[hh] = t["s"]
        o_ref[0, :, t["cols"]] = jnp.concatenate(t["outs"], axis=0)


def _gdn_scan(qkv, gates, gates_t, a_log_row, dt_row, a_log_col, dt_col, n_lat, rev):
    bsz, nt, _ = qkv.shape
    hps = GDN_HEADS_PER_STEP
    hblocks = GDN_HEADS // hps
    n_groups = nt // ROW_TILE
    lat_groups = n_lat // ROW_TILE
    width = hps * GDN_HEAD_DIM

    def group(i):
        lat = (lat_groups - i) if rev else (i - 1)
        return jnp.where(i == 0, n_groups - 1, lat)

    return pl.pallas_call(
        functools.partial(_gdn_body, rev=rev, n_lat_groups=lat_groups),
        grid=(bsz, hblocks, n_groups),
        in_specs=[pl.BlockSpec((1, ROW_TILE, width), lambda b, h, i: (b, group(i), h)),
                  pl.BlockSpec((1, ROW_TILE, width), lambda b, h, i: (b, group(i), hblocks + h)),
                  pl.BlockSpec((1, ROW_TILE, width), lambda b, h, i: (b, group(i), 2 * hblocks + h)),
                  pl.BlockSpec((1, ROW_TILE, LANES), lambda b, h, i: (b, group(i), 0)),
                  pl.BlockSpec((1, 4 * GDN_HEADS, ROW_TILE), lambda b, h, i: (b, 0, group(i))),
                  pl.BlockSpec((1, LANES), lambda b, h, i: (0, 0)),
                  pl.BlockSpec((1, LANES), lambda b, h, i: (0, 0)),
                  pl.BlockSpec((4 * GDN_HEADS, 1), lambda b, h, i: (0, 0)),
                  pl.BlockSpec((4 * GDN_HEADS, 1), lambda b, h, i: (0, 0))],
        out_specs=pl.BlockSpec((1, ROW_TILE, width), lambda b, h, i: (b, group(i), h)),
        out_shape=jax.ShapeDtypeStruct((bsz, nt, GDN_HEADS * GDN_HEAD_DIM), F32),
        scratch_shapes=[pltpu.VMEM((hps, GDN_HEAD_DIM, GDN_HEAD_DIM), F32),
                        pltpu.VMEM((4 * GDN_HEADS, ROW_TILE), F32)],
        compiler_params=_cparams("arbitrary", "arbitrary", "arbitrary"),
        name="gdn_scan_bwd" if rev else "gdn_scan_fwd",
    )(qkv, qkv, qkv, gates, gates_t, a_log_row, dt_row, a_log_col, dt_col)


def _gdn_out_body(of_ref, ob_ref, z_ref, gn_ref, w_ref, h_ref, mod_ref, g_ref, out_ref, a_scr):
    for hh in range(GDN_HEADS):
        cols = slice(hh * GDN_HEAD_DIM, (hh + 1) * GDN_HEAD_DIM)
        o = of_ref[0, :, cols] + ob_ref[0, :, cols]
        a_scr[:, cols] = (_rms(o, gn_ref[...]) * _silu(z_ref[0, :, cols].astype(F32))).astype(BF16)
    y = _dot(a_scr[...], w_ref[...])
    out_ref[0] = h_ref[0] + mod_ref[0, 0, GT_M:GT_M + 1, :] * _rms(y, g_ref[...])


def _gdn_out(o_f, o_b, p, g_onorm, w_bf, h, mods, g_post, n_lat):
    bsz, _, d = h.shape
    width = GDN_HEADS * GDN_HEAD_DIM
    return pl.pallas_call(
        _gdn_out_body,
        grid=(bsz, n_lat // ROW_TILE),
        in_specs=[pl.BlockSpec((1, ROW_TILE, width), lambda b, i: (b, i, 0)),
                  pl.BlockSpec((1, ROW_TILE, width), lambda b, i: (b, i, 0)),
                  pl.BlockSpec((1, ROW_TILE, width), lambda b, i: (b, i, 3)),
                  pl.BlockSpec((1, GDN_HEAD_DIM), lambda b, i: (0, 0)),
                  pl.BlockSpec((width, d), lambda b, i: (0, 0)),
                  pl.BlockSpec((1, ROW_TILE, d), lambda b, i: (b, i, 0)),
                  pl.BlockSpec((1, 1, N_MOD, d), lambda b, i: (b, 0, 0, 0)),
                  pl.BlockSpec((1, d), lambda b, i: (0, 0))],
        out_specs=pl.BlockSpec((1, ROW_TILE, d), lambda b, i: (b, i, 0)),
        out_shape=jax.ShapeDtypeStruct((bsz, n_lat, d), F32),
        scratch_shapes=[pltpu.VMEM((ROW_TILE, width), BF16)],
        compiler_params=_cparams("arbitrary", "arbitrary"),
        name="gdn_out",
    )(o_f, o_b, p, g_onorm[None, :], w_bf, h, mods, g_post[None, :])


def _gate_vectors(a_log_f, dt_bias_f, a_log_b, dt_bias_b):
    z = jnp.zeros((GDN_HEADS,), F32)
    a = jnp.concatenate([a_log_f, z, a_log_b, z])
    d = jnp.concatenate([dt_bias_f, z, dt_bias_b, z])
    pad = jnp.zeros((LANES - 4 * GDN_HEADS,), F32)
    return (jnp.concatenate([a, pad])[None, :], jnp.concatenate([d, pad])[None, :], a[:, None], d[:, None])


def kernel(x, c, ctx, c_ctx, l0_w_ada, l0_b_ada, l0_g_pre_mix, l0_g_post_mix, l0_g_pre_ffn, l0_g_post_ffn, l0_w_qkv, l0_lambda_q1, l0_lambda_k1, l0_lambda_q2, l0_lambda_k2, l0_g_subln, l0_w_o, l0_w_router, l0_w_gate, l0_w_up, l0_w_down, l1_w_ada, l1_b_ada, l1_g_pre_mix, l1_g_post_mix, l1_g_pre_ffn, l1_g_post_ffn, l1_w_in, l1_w_conv, l1_a_log_f, l1_dt_bias_f, l1_a_log_b, l1_dt_bias_b, l1_g_onorm, l1_w_o, l1_w_router, l1_w_gate, l1_w_up, l1_w_down):
    n_lat, n_ctx = x.shape[1], ctx.shape[1]
    h = jnp.concatenate([x, ctx], axis=1)

    mods = _ada_mods(c, c_ctx, l0_w_ada, l0_b_ada)
    qkv = _project(h, mods, l0_g_pre_mix, l0_w_qkv.astype(BF16), n_lat, rope_tables=_rope_tables(n_lat, n_ctx))
    lam_vecs = jnp.stack([l0_lambda_q1, l0_lambda_k1, l0_lambda_q2, l0_lambda_k2])
    o = _diff_attention(qkv, lam_vecs, l0_g_subln, n_lat, depth=0)
    h = _out_project(o, l0_w_o.astype(BF16), h, mods, l0_g_post_mix, n_lat)
    h = _moe(h, mods, l0_g_pre_ffn, l0_g_post_ffn, l0_w_router, l0_w_gate.astype(BF16), l0_w_up.astype(BF16),
             l0_w_down.astype(BF16), n_lat, n_ctx)

    mods = _ada_mods(c, c_ctx, l1_w_ada, l1_b_ada)
    width = GDN_HEADS * GDN_HEAD_DIM
    w_gates = jnp.pad(l1_w_in[:, 4 * width:], ((0, 0), (0, LANES - 4 * GDN_HEADS))).astype(BF16)
    p, gates = _project(h, mods, l1_g_pre_mix, l1_w_in[:, :4 * width].astype(BF16), n_lat, w_gates=w_gates)
    gates_t = jnp.swapaxes(gates[:, :, :4 * GDN_HEADS], 1, 2)
    qkv = _short_conv(p, l1_w_conv, n_lat)
    gv = _gate_vectors(l1_a_log_f, l1_dt_bias_f, l1_a_log_b, l1_dt_bias_b)
    o_f = _gdn_scan(qkv, gates, gates_t, *gv, n_lat, rev=False)
    o_b = _gdn_scan(qkv, gates, gates_t, *gv, n_lat, rev=True)
    hl = _gdn_out(o_f, o_b, p, l1_g_onorm, l1_w_o.astype(BF16), h, mods, l1_g_post_mix, n_lat)
    return _moe(hl, mods, l1_g_pre_ffn, l1_g_post_ffn, l1_w_router, l1_w_gate.astype(BF16), l1_w_up.astype(BF16),
                l1_w_down.astype(BF16), n_lat, 0)
```

```python
import functools
import math

import jax
import jax.numpy as jnp
from jax import lax
from jax.experimental import pallas as pl
from jax.experimental.pallas import tpu as pltpu

F32 = jnp.float32
BF16 = jnp.bfloat16
HIGHEST = lax.Precision.HIGHEST

EPS = 1e-6
N_MOD = 6
GRID_W = 64
ROPE_BASE = 10000.0
DA_HEADS = 8
DA_HEAD_DIM = 64
GDN_HEADS = 8
GDN_HEAD_DIM = 128
CONV_K = 5
CONV_HALO = 8
N_EXPERTS = 16
CAP_FACTOR = 2

LANES = 128
ROW_TILE = 256
PROJ_ROWS = 768
ATTN_Q_ROWS = 512
ATTN_CHAIN_ROWS = 128
Q_PRESCALE = DA_HEAD_DIM ** -0.5 * math.log2(math.e)
GDN_CHUNK = 64
GDN_HEADS_PER_STEP = 8
FF_CHUNK = 512
VMEM_LIMIT = 56 * 1024 * 1024

SH_M, SC_M, GT_M, SH_F, SC_F, GT_F = range(6)


def _cparams(*sem):
    return pltpu.CompilerParams(dimension_semantics=sem, vmem_limit_bytes=VMEM_LIMIT)


def _dot(a, b):
    return jnp.dot(a, b, preferred_element_type=F32)


def _dot_nt(a, b, precision=None):
    return lax.dot_general(a, b, (((1,), (1,)), ((), ())), preferred_element_type=F32, precision=precision)


def _dot_tn(a, b):
    return lax.dot_general(a, b, (((0,), (0,)), ((), ())), preferred_element_type=F32)


def _rms(x, g):
    return x * lax.rsqrt(jnp.mean(x * x, axis=-1, keepdims=True) + EPS) * g


def _sigmoid(x):
    return 1.0 / (1.0 + jnp.exp(-x))


def _silu(x):
    return x * _sigmoid(x)


def _softplus(x):
    return jnp.maximum(x, 0.0) + jnp.log(1.0 + jnp.exp(-jnp.abs(x)))


def _ada_body(c_ref, w_ref, b_ref, o_ref):
    o_ref[...] = jnp.dot(_silu(c_ref[...]), w_ref[...], precision=HIGHEST, preferred_element_type=F32) + b_ref[...]


def _ada_mods(c, c_ctx, w_ada, b_ada):
    bsz, d = c.shape
    rows = 16
    cc = jnp.concatenate([c, c_ctx[None, :], jnp.zeros((rows - bsz - 1, d), F32)], axis=0)
    tn = 1024
    m = pl.pallas_call(
        _ada_body,
        grid=(N_MOD * d // tn,),
        in_specs=[pl.BlockSpec((rows, d), lambda j: (0, 0)),
                  pl.BlockSpec((d, tn), lambda j: (0, j)),
                  pl.BlockSpec((1, tn), lambda j: (0, j))],
        out_specs=pl.BlockSpec((rows, tn), lambda j: (0, j)),
        out_shape=jax.ShapeDtypeStruct((rows, N_MOD * d), F32),
        compiler_params=_cparams("arbitrary"),
        name="ada_mods",
    )(cc, w_ada, b_ada[None, :])
    lat = m[:bsz].reshape(bsz, 1, N_MOD, d)
    ctx = jnp.broadcast_to(m[bsz].reshape(1, 1, N_MOD, d), (bsz, 1, N_MOD, d))
    return jnp.concatenate([lat, ctx], axis=1)


def _modulated(h_ref, mod_ref, g_ref, shift, scale):
    x = h_ref[0]
    return _rms(x, g_ref[...]) * (1.0 + mod_ref[0, 0, scale:scale + 1, :]) + mod_ref[0, 0, shift:shift + 1, :]


def _proj_body(h_ref, mod_ref, g_ref, w_ref, *rest, n_lat, rope, gates):
    rest = list(rest)
    cos_ref, sin_ref = (rest.pop(0), rest.pop(0)) if rope else (None, None)
    wg_ref = rest.pop(0) if gates else None
    o_ref = rest.pop(0)
    og_ref = rest.pop(0) if gates else None
    x = h_ref[0]
    tm, d = x.shape
    row = pl.program_id(1) * tm + lax.broadcasted_iota(jnp.int32, (tm, 1), 0)
    is_ctx = row >= n_lat
    scale = jnp.where(is_ctx, mod_ref[0, 1, SC_M:SC_M + 1, :], mod_ref[0, 0, SC_M:SC_M + 1, :])
    shift = jnp.where(is_ctx, mod_ref[0, 1, SH_M:SH_M + 1, :], mod_ref[0, 0, SH_M:SH_M + 1, :])
    u = (_rms(x, g_ref[...]) * (1.0 + scale) + shift).astype(BF16)
    if gates:
        og_ref[0] = _dot(u, wg_ref[...])
    for j in range(w_ref.shape[1] // d):
        cols = slice(j * d, (j + 1) * d)
        acc = _dot(u, w_ref[:, cols])
        if rope and j < 2:
            reps = d // LANES
            cos = jnp.tile(cos_ref[...], (1, reps))
            sin = jnp.tile(sin_ref[...], (1, reps))
            lane = lax.broadcasted_iota(jnp.int32, acc.shape, 1)
            half = DA_HEAD_DIM // 4
            first = (lane % (2 * half)) < half
            partner = jnp.where(first, pltpu.roll(acc, d - half, 1), pltpu.roll(acc, half, 1))
            acc = acc * cos + partner * sin
            if j == 0:
                acc = acc * Q_PRESCALE
        o_ref[0, :, cols] = acc.astype(o_ref.dtype)


def _project(h, mods, g, w_bf, n_lat, rope_tables=None, w_gates=None):
    bsz, nt, d = h.shape
    n_out = w_bf.shape[1]
    tm = PROJ_ROWS
    rope = rope_tables is not None
    gates = w_gates is not None
    in_specs = [pl.BlockSpec((1, tm, d), lambda b, i: (b, i, 0)),
                pl.BlockSpec((1, 2, N_MOD, d), lambda b, i: (b, 0, 0, 0)),
                pl.BlockSpec((1, d), lambda b, i: (0, 0)),
                pl.BlockSpec((d, n_out), lambda b, i: (0, 0))]
    args = [h, mods, g[None, :], w_bf]
    out_specs = [pl.BlockSpec((1, tm, n_out), lambda b, i: (b, i, 0))]
    out_shape = [jax.ShapeDtypeStruct((bsz, nt, n_out), BF16)]
    if rope:
        in_specs += [pl.BlockSpec((tm, LANES), lambda b, i: (i, 0))] * 2
        args += list(rope_tables)
    if gates:
        in_specs.append(pl.BlockSpec((d, LANES), lambda b, i: (0, 0)))
        args.append(w_gates)
        out_specs.append(pl.BlockSpec((1, tm, LANES), lambda b, i: (b, i, 0)))
        out_shape.append(jax.ShapeDtypeStruct((bsz, nt, LANES), F32))
    out = pl.pallas_call(
        functools.partial(_proj_body, n_lat=n_lat, rope=rope, gates=gates),
        grid=(bsz, nt // tm),
        in_specs=in_specs,
        out_specs=out_specs,
        out_shape=out_shape,
        compiler_params=_cparams("arbitrary", "arbitrary"),
        name="mod_project",
    )(*args)
    return out if gates else out[0]


def _rope_tables(n_lat, n_ctx):
    rows = n_lat // GRID_W
    r = jnp.repeat(jnp.arange(rows), GRID_W).astype(F32)
    col = jnp.tile(jnp.arange(GRID_W), rows).astype(F32)
    half = DA_HEAD_DIM // 2
    inv = ROPE_BASE ** (-jnp.arange(0, half, 2, dtype=F32) / half)
    ang_r, ang_c = r[:, None] * inv, col[:, None] * inv
    cos = jnp.concatenate([jnp.cos(ang_r)] * 2 + [jnp.cos(ang_c)] * 2, axis=-1)
    sin = jnp.concatenate([-jnp.sin(ang_r), jnp.sin(ang_r), -jnp.sin(ang_c), jnp.sin(ang_c)], axis=-1)
    cos = jnp.concatenate([jnp.tile(cos, (1, 2)), jnp.ones((n_ctx, LANES), F32)], axis=0)
    sin = jnp.concatenate([jnp.tile(sin, (1, 2)), jnp.zeros((n_ctx, LANES), F32)], axis=0)
    return cos, sin


def _attn_body(lam_ref, q_ref, k_ref, v_ref, gs_ref, o_ref, *, lam_init):
    lv = lam_ref[...]
    lam = (jnp.exp(jnp.sum(lv[0:1] * lv[1:2], axis=-1, keepdims=True))
           - jnp.exp(jnp.sum(lv[2:3] * lv[3:4], axis=-1, keepdims=True)) + lam_init)
    q, k, v = q_ref[0], k_ref[0], v_ref[0]
    lane = lax.broadcasted_iota(jnp.int32, q.shape, 1)
    hw = v.shape[1]
    v_one = jnp.concatenate([v, jnp.ones_like(v)], axis=1)

    rows = min(q.shape[0], ATTN_CHAIN_ROWS)
    chains = [(r, c) for r in range(0, q.shape[0], rows) for c in range(2)]
    qm = [jnp.where((lane >= c * DA_HEAD_DIM) & (lane < (c + 1) * DA_HEAD_DIM), q, jnp.zeros_like(q)) for c in range(2)]
    s = [_dot_nt(qm[c][r:r + rows], k) for r, c in chains]
    e = [jnp.exp2((x - jnp.max(x, axis=-1, keepdims=True)).astype(BF16)) for x in s]
    ov = [_dot(x, v_one) for x in e]
    att = [x[:, :hw] / x[:, hw:hw + 1] for x in ov]
    for n in range(0, len(chains), 2):
        r = chains[n][0]
        o = att[n] - lam * att[n + 1]
        o_ref[0, r:r + rows, :] = (_rms(o, gs_ref[...]) * (1.0 - lam_init)).astype(o_ref.dtype)


def _diff_attention(qkv, lam_vecs, g_subln, n_lat, depth):
    bsz, nt, d3 = qkv.shape
    d = d3 // 3
    hw = 2 * DA_HEAD_DIM
    lam_init = 0.8 - 0.6 * math.exp(-0.3 * depth)
    n_ctx = nt - n_lat

    def call(n_q, tq, q_blk0, n_k, k_blk0):
        return pl.pallas_call(
            functools.partial(_attn_body, lam_init=lam_init),
            grid=(bsz, DA_HEADS, n_q // tq),
            in_specs=[pl.BlockSpec((4, DA_HEAD_DIM), lambda b, h, i: (0, 0)),
                      pl.BlockSpec((1, tq, hw), lambda b, h, i: (b, q_blk0 + i, h)),
                      pl.BlockSpec((1, n_k, hw), lambda b, h, i: (b, k_blk0, DA_HEADS + h)),
                      pl.BlockSpec((1, n_k, hw), lambda b, h, i: (b, k_blk0, 2 * DA_HEADS + h)),
                      pl.BlockSpec((1, hw), lambda b, h, i: (0, 0))],
            out_specs=pl.BlockSpec((1, tq, hw), lambda b, h, i: (b, i, h)),
            out_shape=jax.ShapeDtypeStruct((bsz, n_q, d), BF16),
            compiler_params=_cparams("arbitrary", "arbitrary", "arbitrary"),
            name="diff_attention",
        )(lam_vecs, qkv, qkv, qkv, g_subln[None, :])

    o_lat = call(n_lat, ATTN_Q_ROWS, 0, nt, 0)
    o_ctx = call(n_ctx, n_ctx, n_lat // n_ctx, n_ctx, n_lat // n_ctx)
    return jnp.concatenate([o_lat, o_ctx], axis=1)


def _oproj_body(o_ref, w_ref, h_ref, mod_ref, g_ref, out_ref):
    y = _dot(o_ref[0], w_ref[...])
    out_ref[0] = h_ref[0] + mod_ref[0, 0, GT_M:GT_M + 1, :] * _rms(y, g_ref[...])


def _out_project(o, w_bf, h, mods, g_post, n_lat):
    bsz, nt, d = h.shape
    lat_tiles = n_lat // ROW_TILE
    return pl.pallas_call(
        _oproj_body,
        grid=(bsz, nt // ROW_TILE),
        in_specs=[pl.BlockSpec((1, ROW_TILE, d), lambda b, i: (b, i, 0)),
                  pl.BlockSpec((d, d), lambda b, i: (0, 0)),
                  pl.BlockSpec((1, ROW_TILE, d), lambda b, i: (b, i, 0)),
                  pl.BlockSpec((1, 1, N_MOD, d), lambda b, i: (b, i // lat_tiles, 0, 0)),
                  pl.BlockSpec((1, d), lambda b, i: (0, 0))],
        out_specs=pl.BlockSpec((1, ROW_TILE, d), lambda b, i: (b, i, 0)),
        out_shape=jax.ShapeDtypeStruct((bsz, nt, d), F32),
        compiler_params=_cparams("arbitrary", "arbitrary"),
        name="out_project",
    )(o, w_bf, h, mods, g_post[None, :])


def _router_body(h_ref, mod_ref, g_ref, wr_ref, u_ref, aff_ref):
    u = _modulated(h_ref, mod_ref, g_ref, SH_F, SC_F)
    u_ref[0] = u.astype(BF16)
    logits = _dot_nt(wr_ref[...], u, precision=HIGHEST)
    e = jnp.exp(logits - jnp.max(logits, axis=0, keepdims=True))
    aff_ref[0] = e / jnp.sum(e, axis=0, keepdims=True)


def _route(h, mods, g, w_router_t, n_lat):
    bsz, nt, d = h.shape
    lat_tiles = n_lat // ROW_TILE
    return pl.pallas_call(
        _router_body,
        grid=(bsz, nt // ROW_TILE),
        in_specs=[pl.BlockSpec((1, ROW_TILE, d), lambda b, i: (b, i, 0)),
                  pl.BlockSpec((1, 1, N_MOD, d), lambda b, i: (b, i // lat_tiles, 0, 0)),
                  pl.BlockSpec((1, d), lambda b, i: (0, 0)),
                  pl.BlockSpec((N_EXPERTS, d), lambda b, i: (0, 0))],
        out_specs=[pl.BlockSpec((1, ROW_TILE, d), lambda b, i: (b, i, 0)),
                   pl.BlockSpec((1, N_EXPERTS, ROW_TILE), lambda b, i: (b, 0, i))],
        out_shape=[jax.ShapeDtypeStruct((bsz, nt, d), BF16),
                   jax.ShapeDtypeStruct((bsz, N_EXPERTS, nt), F32)],
        compiler_params=_cparams("arbitrary", "arbitrary"),
        name="moe_router",
    )(h, mods, g[None, :], w_router_t)


def _lane_cumsum(x):
    n = x.shape[1]
    jj = lax.broadcasted_iota(jnp.int32, (ROW_TILE, ROW_TILE), 0)
    nn = lax.broadcasted_iota(jnp.int32, (ROW_TILE, ROW_TILE), 1)
    tri = (jj <= nn).astype(BF16)
    run = jnp.zeros((x.shape[0], 1), F32)
    parts = []
    for t in range(n // ROW_TILE):
        local = _dot(x[:, t * ROW_TILE:(t + 1) * ROW_TILE].astype(BF16), tri) + run
        parts.append(local)
        run = local[:, ROW_TILE - 1:ROW_TILE]
    return jnp.concatenate(parts, axis=1) if len(parts) > 1 else parts[0]


def _select_slots(aff, cap):
    bits = pltpu.bitcast(aff, jnp.int32)

    def step(t, lo):
        cand = lo | jnp.left_shift(jnp.int32(1), 30 - t)
        cnt = jnp.sum((bits >= cand).astype(F32), axis=1, keepdims=True)
        return jnp.where(cnt >= cap, cand, lo)

    thr = lax.fori_loop(0, 31, step, jnp.zeros((aff.shape[0], 1), jnp.int32))
    gt = bits > thr
    eq = bits == thr
    need = cap - jnp.sum(gt.astype(F32), axis=1, keepdims=True)
    eq_rank = _lane_cumsum(eq.astype(F32))
    sel = gt | (eq & (eq_rank <= need))
    slot = _lane_cumsum(sel.astype(F32)) - 1.0
    return jnp.where(sel, slot, -1.0).astype(jnp.int32)


def _select_body(aff_ref, pos_ref, *, n_lat, n_ctx):
    aff = aff_ref[0]
    pos_ref[0, :, :n_lat] = _select_slots(aff[:, :n_lat], CAP_FACTOR * n_lat // N_EXPERTS)
    if n_ctx:
        pos_ref[0, :, n_lat:] = _select_slots(aff[:, n_lat:], CAP_FACTOR * n_ctx // N_EXPERTS)


def _select(aff, n_lat, n_ctx):
    bsz, e, nt = aff.shape
    return pl.pallas_call(
        functools.partial(_select_body, n_lat=n_lat, n_ctx=n_ctx),
        grid=(bsz,),
        in_specs=[pl.BlockSpec((1, e, nt), lambda b: (b, 0, 0))],
        out_specs=pl.BlockSpec((1, e, nt), lambda b: (b, 0, 0)),
        out_shape=jax.ShapeDtypeStruct((bsz, e, nt), jnp.int32),
        compiler_params=_cparams("arbitrary"),
        name="moe_select",
    )(aff)


def _one_hot_slots(pos_row, cap):
    slot = lax.broadcasted_iota(jnp.int32, (cap, pos_row.shape[1]), 0)
    return pos_row == slot


def _expert_body(pos_ref, aff_ref, u_ref, wg_ref, wu_ref, wd_ref, y_ref, xs_scr, gate_scr, *, segments):
    pos = pos_ref[0, 0]
    aff = aff_ref[0, 0]
    row = 0
    for start, n, cap in segments:
        hot = _one_hot_slots(pos[:, start:start + n], cap)
        xs_scr[row:row + cap, :] = _dot(jnp.where(hot, 1.0, 0.0).astype(BF16), u_ref[0, start:start + n, :]).astype(BF16)
        gate_scr[row:row + cap, :] = jnp.sum(jnp.where(hot, aff[:, start:start + n], 0.0), axis=1, keepdims=True)
        row += cap
    xs = xs_scr[...]
    ff = wg_ref.shape[2]
    acc = jnp.zeros((xs.shape[0], wd_ref.shape[2]), F32)
    for c in range(ff // FF_CHUNK):
        cols = slice(c * FF_CHUNK, (c + 1) * FF_CHUNK)
        hid = _silu(_dot(xs, wg_ref[0, :, cols])) * _dot(xs, wu_ref[0, :, cols])
        acc = acc + _dot(hid.astype(BF16), wd_ref[0, cols, :])
    y_ref[0, 0] = (acc * gate_scr[...]).astype(y_ref.dtype)


def _segments(n_lat, n_ctx):
    segs = [(0, n_lat, CAP_FACTOR * n_lat // N_EXPERTS)]
    if n_ctx:
        segs.append((n_lat, n_ctx, CAP_FACTOR * n_ctx // N_EXPERTS))
    return tuple(segs)


def _experts(pos, aff, u, wg, wu, wd, n_lat, n_ctx):
    bsz, nt, d = u.shape
    e, _, ff = wg.shape
    segs = _segments(n_lat, n_ctx)
    rows = sum(s[2] for s in segs)
    return pl.pallas_call(
        functools.partial(_expert_body, segments=segs),
        grid=(e, bsz),
        in_specs=[pl.BlockSpec((1, 1, 1, nt), lambda x, b: (b, x, 0, 0)),
                  pl.BlockSpec((1, 1, 1, nt), lambda x, b: (b, x, 0, 0)),
                  pl.BlockSpec((1, nt, d), lambda x, b: (b, 0, 0)),
                  pl.BlockSpec((1, d, ff), lambda x, b: (x, 0, 0)),
                  pl.BlockSpec((1, d, ff), lambda x, b: (x, 0, 0)),
                  pl.BlockSpec((1, ff, d), lambda x, b: (x, 0, 0))],
        out_specs=pl.BlockSpec((1, 1, rows, d), lambda x, b: (b, x, 0, 0)),
        out_shape=jax.ShapeDtypeStruct((bsz, e, rows, d), BF16),
        scratch_shapes=[pltpu.VMEM((rows, d), BF16), pltpu.VMEM((rows, 1), F32)],
        compiler_params=_cparams("arbitrary", "arbitrary"),
        name="moe_experts",
    )(pos.reshape(bsz, e, 1, nt), aff.reshape(bsz, e, 1, nt), u, wg, wu, wd)


def _combine_body(pos_ref, y_ref, h_ref, mod_ref, g_ref, out_ref, *, n_lat, rows, lat_cap):
    is_ctx = pl.program_id(1) * ROW_TILE >= n_lat
    pos = pos_ref[0]
    slot = jnp.where(pos >= 0, pos + jnp.where(is_ctx, lat_cap, 0), -1)
    ids = lax.broadcasted_iota(jnp.int32, (rows, ROW_TILE), 0)
    hot = jnp.concatenate([jnp.where(slot[x:x + 1, :] == ids, 1.0, 0.0).astype(BF16)
                           for x in range(pos.shape[0])], axis=0)
    f = _dot_tn(hot, y_ref[0])
    out_ref[0] = h_ref[0] + mod_ref[0, 0, GT_F:GT_F + 1, :] * _rms(f, g_ref[...])


def _combine(pos, y, h, mods, g_post, n_lat, n_ctx):
    bsz, e, rows, d = y.shape
    nt = n_lat + n_ctx
    lat_tiles = n_lat // ROW_TILE
    return pl.pallas_call(
        functools.partial(_combine_body, n_lat=n_lat, rows=rows, lat_cap=CAP_FACTOR * n_lat // N_EXPERTS),
        grid=(bsz, nt // ROW_TILE),
        in_specs=[pl.BlockSpec((1, e, ROW_TILE), lambda b, i: (b, 0, i)),
                  pl.BlockSpec((1, e * rows, d), lambda b, i: (b, 0, 0)),
                  pl.BlockSpec((1, ROW_TILE, d), lambda b, i: (b, i, 0)),
                  pl.BlockSpec((1, 1, N_MOD, d), lambda b, i: (b, i // lat_tiles, 0, 0)),
                  pl.BlockSpec((1, d), lambda b, i: (0, 0))],
        out_specs=pl.BlockSpec((1, ROW_TILE, d), lambda b, i: (b, i, 0)),
        out_shape=jax.ShapeDtypeStruct((bsz, nt, d), F32),
        compiler_params=_cparams("arbitrary", "arbitrary"),
        name="moe_combine",
    )(pos, y.reshape(bsz, e * rows, d), h, mods, g_post[None, :])


def _moe(h, mods, g_pre, g_post, w_router, wg, wu, wd, n_lat, n_ctx):
    u, aff = _route(h[:, :n_lat + n_ctx], mods, g_pre, w_router.T, n_lat)
    pos = _select(aff, n_lat, n_ctx)
    y = _experts(pos, aff, u, wg, wu, wd, n_lat, n_ctx)
    return _combine(pos, y, h, mods, g_post, n_lat, n_ctx)


def _conv_body(x_ref, w_ref, o_ref, pad_scr, *, n_lat):
    j = pl.program_id(1)
    nt = x_ref.shape[1]
    halo = CONV_HALO
    zeros = jnp.zeros((halo, LANES), F32)
    for lo, hi in ((0, n_lat), (n_lat, nt)):
        n = hi - lo
        if n == 0:
            continue
        pad_scr[0:halo, :] = zeros
        pad_scr[halo:halo + n, :] = x_ref[0, lo:hi, :].astype(F32)
        pad_scr[halo + n:2 * halo + n, :] = zeros
        acc = jnp.zeros((n, LANES), F32)
        for tap in range(CONV_K):
            start = halo + tap - CONV_K // 2
            acc = acc + pad_scr[start:start + n, :] * w_ref[tap:tap + 1, :]
        y = _silu(acc)
        unit = y * lax.rsqrt(jnp.sum(y * y, axis=-1, keepdims=True) + EPS)
        y = jnp.where(j < GDN_HEADS, unit * GDN_HEAD_DIM ** -0.5, jnp.where(j < 2 * GDN_HEADS, unit, y))
        o_ref[0, lo:hi, :] = y


def _short_conv(p, w_conv, n_lat):
    bsz, nt, _ = p.shape
    nblk = 3 * GDN_HEADS
    return pl.pallas_call(
        functools.partial(_conv_body, n_lat=n_lat),
        grid=(bsz, nblk),
        in_specs=[pl.BlockSpec((1, nt, LANES), lambda b, j: (b, 0, j)),
                  pl.BlockSpec((CONV_K, LANES), lambda b, j: (0, j))],
        out_specs=pl.BlockSpec((1, nt, LANES), lambda b, j: (b, 0, j)),
        out_shape=jax.ShapeDtypeStruct((bsz, nt, nblk * LANES), F32),
        scratch_shapes=[pltpu.VMEM((n_lat + 2 * CONV_HALO, LANES), F32)],
        compiler_params=_cparams("arbitrary", "arbitrary"),
        name="gdn_conv",
    )(p, w_conv)


def _gdn_body(q_ref, k_ref, v_ref, gate_ref, gate_t_ref, alog_ref, dt_ref, alog_t_ref, dt_t_ref,
              o_ref, s_scr, gct_scr, *, rev, n_lat_groups):
    hb = pl.program_id(1)
    step = pl.program_id(2)
    r_t = ROW_TILE
    c_sz = GDN_CHUNK
    goff = 2 * GDN_HEADS if rev else 0

    @pl.when(step == 0)
    def _():
        s_scr[...] = jnp.zeros_like(s_scr)

    ii = lax.broadcasted_iota(jnp.int32, (r_t, r_t), 0)
    jj = lax.broadcasted_iota(jnp.int32, (r_t, r_t), 1)
    same = (ii // c_sz) == (jj // c_sz)
    if rev:
        incl, strict = same & (jj >= ii), same & (jj > ii)
    else:
        incl, strict = same & (jj <= ii), same & (jj < ii)
    m_incl = incl.astype(BF16)
    m_same = same.astype(BF16)
    eye = (ii == jj).astype(F32)

    def pieces(x):
        hi = x.astype(BF16)
        rest = x - hi.astype(F32)
        mid = rest.astype(BF16)
        return hi, mid, (rest - mid.astype(F32)).astype(BF16)

    lane = lax.broadcasted_iota(jnp.int32, (1, LANES), 1)
    is_g = (lane >= goff) & (lane < goff + GDN_HEADS)
    raw = gate_ref[0]
    g_all = jnp.where(is_g, -jnp.exp(alog_ref[...]) * _softplus(raw + dt_ref[...]), 0.0)
    beta_all = _sigmoid(raw)
    g_parts = pieces(g_all)
    gc_all = sum(_dot(m_incl, x) for x in g_parts)
    gl_all = sum(_dot(m_same, x) for x in g_parts)
    g_t = -jnp.exp(alog_t_ref[...]) * _softplus(gate_t_ref[0] + dt_t_ref[...])
    gct_scr[...] = sum(_dot_nt(x, m_incl) for x in pieces(g_t))
    lane_f = lax.broadcasted_iota(jnp.int32, (r_t, LANES), 1)

    def column(a, idx):
        return jnp.sum(jnp.where(lane_f == idx, a, 0.0), axis=1, keepdims=True)

    heads = range(GDN_HEADS_PER_STEP)
    n_chunks = r_t // c_sz
    n_levels = int(math.log2(c_sz)) - 1
    st = []
    for hh in heads:
        head = hb * GDN_HEADS_PER_STEP + hh
        cols = slice(hh * GDN_HEAD_DIM, (hh + 1) * GDN_HEAD_DIM)
        q, k, v = q_ref[0, :, cols], k_ref[0, :, cols], v_ref[0, :, cols]
        gc = column(gc_all, goff + head)
        gl = column(gl_all, goff + head)
        beta = column(beta_all, goff + GDN_HEADS + head)
        gc_row = gct_scr[pl.ds(goff + head, 1), :]
        decay = jnp.where(incl, jnp.exp(jnp.where(incl, gc - gc_row, 0.0)), 0.0)
        kb = k * beta
        k_bf = k.astype(BF16)
        a_mat = jnp.where(strict, _dot_nt(kb.astype(BF16), k_bf) * decay, 0.0)
        st.append(dict(
            cols=cols, gl=gl, inv=eye - a_mat, pw=a_mat,
            rhs=jnp.concatenate([v * beta, kb * jnp.exp(gc)], axis=1).astype(BF16),
            intra=(_dot_nt(q.astype(BF16), k_bf) * decay).astype(BF16),
            q_dec=(q * jnp.exp(gc)).astype(BF16),
            k_dec_t=(k * jnp.exp(gl - gc)).T.astype(BF16),
            s=s_scr[hh], outs=[None] * n_chunks))
    for _ in range(n_levels):
        for t in st:
            pw_bf = t["pw"].astype(BF16)
            t["pw"] = _dot(pw_bf, pw_bf)
        for t in st:
            t["inv"] = t["inv"] + _dot(t["inv"].astype(BF16), t["pw"].astype(BF16))
    for t in st:
        uw = _dot(t["inv"].astype(BF16), t["rhs"])
        t["u"], t["w"] = uw[:, :GDN_HEAD_DIM], uw[:, GDN_HEAD_DIM:].astype(BF16)
    for c in (range(n_chunks - 1, -1, -1) if rev else range(n_chunks)):
        rows = slice(c * c_sz, (c + 1) * c_sz)
        for t in st:
            t["s_bf"] = t["s"].astype(BF16)
            t["v_new"] = (t["u"][rows] - _dot(t["w"][rows], t["s_bf"])).astype(BF16)
        for t in st:
            t["outs"][c] = _dot(t["q_dec"][rows], t["s_bf"]) + _dot(t["intra"][rows, rows], t["v_new"])
            t["s"] = (t["s"] * jnp.exp(t["gl"][c * c_sz:c * c_sz + 1, :])
                      + _dot(t["k_dec_t"][:, rows], t["v_new"]))
    for hh, t in enumerate(st):
        s_scr[hh] = t["s"]
        o_ref[0, :, t["cols"]] = jnp.concatenate(t["outs"], axis=0)


def _gdn_scan(qkv, gates, gates_t, a_log_row, dt_row, a_log_col, dt_col, n_lat, rev):
    bsz, nt, _ = qkv.shape
    hps = GDN_HEADS_PER_STEP
    hblocks = GDN_HEADS // hps
    n_groups = nt // ROW_TILE
    lat_groups = n_lat // ROW_TILE
    width = hps * GDN_HEAD_DIM

    def group(i):
        lat = (lat_groups - i) if rev else (i - 1)
        return jnp.where(i == 0, n_groups - 1, lat)

    return pl.pallas_call(
        functools.partial(_gdn_body, rev=rev, n_lat_groups=lat_groups),
        grid=(bsz, hblocks, n_groups),
        in_specs=[pl.BlockSpec((1, ROW_TILE, width), lambda b, h, i: (b, group(i), h)),
                  pl.BlockSpec((1, ROW_TILE, width), lambda b, h, i: (b, group(i), hblocks + h)),
                  pl.BlockSpec((1, ROW_TILE, width), lambda b, h, i: (b, group(i), 2 * hblocks + h)),
                  pl.BlockSpec((1, ROW_TILE, LANES), lambda b, h, i: (b, group(i), 0)),
                  pl.BlockSpec((1, 4 * GDN_HEADS, ROW_TILE), lambda b, h, i: (b, 0, group(i))),
                  pl.BlockSpec((1, LANES), lambda b, h, i: (0, 0)),
                  pl.BlockSpec((1, LANES), lambda b, h, i: (0, 0)),
                  pl.BlockSpec((4 * GDN_HEADS, 1), lambda b, h, i: (0, 0)),
                  pl.BlockSpec((4 * GDN_HEADS, 1), lambda b, h, i: (0, 0))],
        out_specs=pl.BlockSpec((1, ROW_TILE, width), lambda b, h, i: (b, group(i), h)),
        out_shape=jax.ShapeDtypeStruct((bsz, nt, GDN_HEADS * GDN_HEAD_DIM), F32),
        scratch_shapes=[pltpu.VMEM((hps, GDN_HEAD_DIM, GDN_HEAD_DIM), F32),
                        pltpu.VMEM((4 * GDN_HEADS, ROW_TILE), F32)],
        compiler_params=_cparams("arbitrary", "arbitrary", "arbitrary"),
        name="gdn_scan_bwd" if rev else "gdn_scan_fwd",
    )(qkv, qkv, qkv, gates, gates_t, a_log_row, dt_row, a_log_col, dt_col)


def _gdn_out_body(of_ref, ob_ref, z_ref, gn_ref, w_ref, h_ref, mod_ref, g_ref, out_ref, a_scr):
    for hh in range(GDN_HEADS):
        cols = slice(hh * GDN_HEAD_DIM, (hh + 1) * GDN_HEAD_DIM)
        o = of_ref[0, :, cols] + ob_ref[0, :, cols]
        a_scr[:, cols] = (_rms(o, gn_ref[...]) * _silu(z_ref[0, :, cols].astype(F32))).astype(BF16)
    y = _dot(a_scr[...], w_ref[...])
    out_ref[0] = h_ref[0] + mod_ref[0, 0, GT_M:GT_M + 1, :] * _rms(y, g_ref[...])


def _gdn_out(o_f, o_b, p, g_onorm, w_bf, h, mods, g_post, n_lat):
    bsz, _, d = h.shape
    width = GDN_HEADS * GDN_HEAD_DIM
    return pl.pallas_call(
        _gdn_out_body,
        grid=(bsz, n_lat // ROW_TILE),
        in_specs=[pl.BlockSpec((1, ROW_TILE, width), lambda b, i: (b, i, 0)),
                  pl.BlockSpec((1, ROW_TILE, width), lambda b, i: (b, i, 0)),
                  pl.BlockSpec((1, ROW_TILE, width), lambda b, i: (b, i, 3)),
                  pl.BlockSpec((1, GDN_HEAD_DIM), lambda b, i: (0, 0)),
                  pl.BlockSpec((width, d), lambda b, i: (0, 0)),
                  pl.BlockSpec((1, ROW_TILE, d), lambda b, i: (b, i, 0)),
                  pl.BlockSpec((1, 1, N_MOD, d), lambda b, i: (b, 0, 0, 0)),
                  pl.BlockSpec((1, d), lambda b, i: (0, 0))],
        out_specs=pl.BlockSpec((1, ROW_TILE, d), lambda b, i: (b, i, 0)),
        out_shape=jax.ShapeDtypeStruct((bsz, n_lat, d), F32),
        scratch_shapes=[pltpu.VMEM((ROW_TILE, width), BF16)],
        compiler_params=_cparams("arbitrary", "arbitrary"),
        name="gdn_out",
    )(o_f, o_b, p, g_onorm[None, :], w_bf, h, mods, g_post[None, :])


def _gate_vectors(a_log_f, dt_bias_f, a_log_b, dt_bias_b):
    z = jnp.zeros((GDN_HEADS,), F32)
    a = jnp.concatenate([a_log_f, z, a_log_b, z])
    d = jnp.concatenate([dt_bias_f, z, dt_bias_b, z])
    pad = jnp.zeros((LANES - 4 * GDN_HEADS,), F32)
    return (jnp.concatenate([a, pad])[None, :], jnp.concatenate([d, pad])[None, :], a[:, None], d[:, None])


def kernel(x, c, ctx, c_ctx, l0_w_ada, l0_b_ada, l0_g_pre_mix, l0_g_post_mix, l0_g_pre_ffn, l0_g_post_ffn, l0_w_qkv, l0_lambda_q1, l0_lambda_k1, l0_lambda_q2, l0_lambda_k2, l0_g_subln, l0_w_o, l0_w_router, l0_w_gate, l0_w_up, l0_w_down, l1_w_ada, l1_b_ada, l1_g_pre_mix, l1_g_post_mix, l1_g_pre_ffn, l1_g_post_ffn, l1_w_in, l1_w_conv, l1_a_log_f, l1_dt_bias_f, l1_a_log_b, l1_dt_bias_b, l1_g_onorm, l1_w_o, l1_w_router, l1_w_gate, l1_w_up, l1_w_down):
    n_lat, n_ctx = x.shape[1], ctx.shape[1]
    h = jnp.concatenate([x, ctx], axis=1)

    mods = _ada_mods(c, c_ctx, l0_w_ada, l0_b_ada)
    qkv = _project(h, mods, l0_g_pre_mix, l0_w_qkv.astype(BF16), n_lat, rope_tables=_rope_tables(n_lat, n_ctx))
    lam_vecs = jnp.stack([l0_lambda_q1, l0_lambda_k1, l0_lambda_q2, l0_lambda_k2])
    o = _diff_attention(qkv, lam_vecs, l0_g_subln, n_lat, depth=0)
    h = _out_project(o, l0_w_o.astype(BF16), h, mods, l0_g_post_mix, n_lat)
    h = _moe(h, mods, l0_g_pre_ffn, l0_g_post_ffn, l0_w_router, l0_w_gate.astype(BF16), l0_w_up.astype(BF16),
             l0_w_down.astype(BF16), n_lat, n_ctx)

    mods = _ada_mods(c, c_ctx, l1_w_ada, l1_b_ada)
    width = GDN_HEADS * GDN_HEAD_DIM
    w_gates = jnp.pad(l1_w_in[:, 4 * width:], ((0, 0), (0, LANES - 4 * GDN_HEADS))).astype(BF16)
    p, gates = _project(h, mods, l1_g_pre_mix, l1_w_in[:, :4 * width].astype(BF16), n_lat, w_gates=w_gates)
    gates_t = jnp.swapaxes(gates[:, :, :4 * GDN_HEADS], 1, 2)
    qkv = _short_conv(p, l1_w_conv, n_lat)
    gv = _gate_vectors(l1_a_log_f, l1_dt_bias_f, l1_a_log_b, l1_dt_bias_b)
    o_f = _gdn_scan(qkv, gates, gates_t, *gv, n_lat, rev=False)
    o_b = _gdn_scan(qkv, gates, gates_t, *gv, n_lat, rev=True)
    hl = _gdn_out(o_f, o_b, p, l1_g_onorm, l1_w_o.astype(BF16), h, mods, l1_g_post_mix, n_lat)
    return _moe(hl, mods, l1_g_pre_ffn, l1_g_post_ffn, l1_w_router, l1_w_gate.astype(BF16), l1_w_up.astype(BF16),
                l1_w_down.astype(BF16), n_lat, 0)
```

```python
import functools
import math

import jax
import jax.numpy as jnp
from jax import lax
from jax.experimental import pallas as pl
from jax.experimental.pallas import tpu as pltpu

F32 = jnp.float32
BF16 = jnp.bfloat16
HIGHEST = lax.Precision.HIGHEST

EPS = 1e-6
N_MOD = 6
GRID_W = 64
ROPE_BASE = 10000.0
DA_HEADS = 8
DA_HEAD_DIM = 64
GDN_HEADS = 8
GDN_HEAD_DIM = 128
CONV_K = 5
CONV_HALO = 8
N_EXPERTS = 16
CAP_FACTOR = 2

LANES = 128
ROW_TILE = 256
PROJ_ROWS = 768
ATTN_Q_ROWS = 512
ATTN_CHAIN_ROWS = 128
Q_PRESCALE = DA_HEAD_DIM ** -0.5 * math.log2(math.e)
GDN_CHUNK = 64
GDN_HEADS_PER_STEP = 8
FF_CHUNK = 512
VMEM_LIMIT = 56 * 1024 * 1024

SH_M, SC_M, GT_M, SH_F, SC_F, GT_F = range(6)


def _cparams(*sem):
    return pltpu.CompilerParams(dimension_semantics=sem, vmem_limit_bytes=VMEM_LIMIT)


def _dot(a, b):
    return jnp.dot(a, b, preferred_element_type=F32)


def _dot_nt(a, b, precision=None):
    return lax.dot_general(a, b, (((1,), (1,)), ((), ())), preferred_element_type=F32, precision=precision)


def _dot_tn(a, b):
    return lax.dot_general(a, b, (((0,), (0,)), ((), ())), preferred_element_type=F32)


def _rms(x, g):
    return x * lax.rsqrt(jnp.mean(x * x, axis=-1, keepdims=True) + EPS) * g


def _sigmoid(x):
    return 1.0 / (1.0 + jnp.exp(-x))


def _silu(x):
    return x * _sigmoid(x)


def _softplus(x):
    return jnp.maximum(x, 0.0) + jnp.log(1.0 + jnp.exp(-jnp.abs(x)))


def _ada_body(c_ref, w_ref, b_ref, o_ref):
    o_ref[...] = jnp.dot(_silu(c_ref[...]), w_ref[...], precision=HIGHEST, preferred_element_type=F32) + b_ref[...]


def _ada_mods(c, c_ctx, w_ada, b_ada):
    bsz, d = c.shape
    rows = 16
    cc = jnp.concatenate([c, c_ctx[None, :], jnp.zeros((rows - bsz - 1, d), F32)], axis=0)
    tn = 1024
    m = pl.pallas_call(
        _ada_body,
        grid=(N_MOD * d // tn,),
        in_specs=[pl.BlockSpec((rows, d), lambda j: (0, 0)),
                  pl.BlockSpec((d, tn), lambda j: (0, j)),
                  pl.BlockSpec((1, tn), lambda j: (0, j))],
        out_specs=pl.BlockSpec((rows, tn), lambda j: (0, j)),
        out_shape=jax.ShapeDtypeStruct((rows, N_MOD * d), F32),
        compiler_params=_cparams("arbitrary"),
        name="ada_mods",
    )(cc, w_ada, b_ada[None, :])
    lat = m[:bsz].reshape(bsz, 1, N_MOD, d)
    ctx = jnp.broadcast_to(m[bsz].reshape(1, 1, N_MOD, d), (bsz, 1, N_MOD, d))
    return jnp.concatenate([lat, ctx], axis=1)


def _proj_body(h_ref, mod_ref, g_ref, w_ref, *rest, n_lat, rope, gates):
    rest = list(rest)
    cos_ref, sin_ref = (rest.pop(0), rest.pop(0)) if rope else (None, None)
    wg_ref = rest.pop(0) if gates else None
    o_ref = rest.pop(0)
    og_ref = rest.pop(0) if gates else None
    x = h_ref[0]
    tm, d = x.shape
    row = pl.program_id(1) * tm + lax.broadcasted_iota(jnp.int32, (tm, 1), 0)
    is_ctx = row >= n_lat
    scale = jnp.where(is_ctx, mod_ref[0, 1, SC_M:SC_M + 1, :], mod_ref[0, 0, SC_M:SC_M + 1, :])
    shift = jnp.where(is_ctx, mod_ref[0, 1, SH_M:SH_M + 1, :], mod_ref[0, 0, SH_M:SH_M + 1, :])
    u = (_rms(x, g_ref[...]) * (1.0 + scale) + shift).astype(BF16)
    if gates:
        og_ref[0] = _dot(u, wg_ref[...])
    for j in range(w_ref.shape[1] // d):
        cols = slice(j * d, (j + 1) * d)
        acc = _dot(u, w_ref[:, cols])
        if rope and j < 2:
            reps = d // LANES
            cos = jnp.tile(cos_ref[...], (1, reps))
            sin = jnp.tile(sin_ref[...], (1, reps))
            lane = lax.broadcasted_iota(jnp.int32, acc.shape, 1)
            half = DA_HEAD_DIM // 4
            first = (lane % (2 * half)) < half
            partner = jnp.where(first, pltpu.roll(acc, d - half, 1), pltpu.roll(acc, half, 1))
            acc = acc * cos + partner * sin
            if j == 0:
                acc = acc * Q_PRESCALE
        o_ref[0, :, cols] = acc.astype(o_ref.dtype)


def _project(h, mods, g, w_bf, n_lat, rope_tables=None, w_gates=None):
    bsz, nt, d = h.shape
    n_out = w_bf.shape[1]
    tm = PROJ_ROWS
    rope = rope_tables is not None
    gates = w_gates is not None
    in_specs = [pl.BlockSpec((1, tm, d), lambda b, i: (b, i, 0)),
                pl.BlockSpec((1, 2, N_MOD, d), lambda b, i: (b, 0, 0, 0)),
                pl.BlockSpec((1, d), lambda b, i: (0, 0)),
                pl.BlockSpec((d, n_out), lambda b, i: (0, 0))]
    args = [h, mods, g[None, :], w_bf]
    out_specs = [pl.BlockSpec((1, tm, n_out), lambda b, i: (b, i, 0))]
    out_shape = [jax.ShapeDtypeStruct((bsz, nt, n_out), BF16)]
    if rope:
        in_specs += [pl.BlockSpec((tm, LANES), lambda b, i: (i, 0))] * 2
        args += list(rope_tables)
    if gates:
        in_specs.append(pl.BlockSpec((d, LANES), lambda b, i: (0, 0)))
        args.append(w_gates)
        out_specs.append(pl.BlockSpec((1, tm, LANES), lambda b, i: (b, i, 0)))
        out_shape.append(jax.ShapeDtypeStruct((bsz, nt, LANES), F32))
    out = pl.pallas_call(
        functools.partial(_proj_body, n_lat=n_lat, rope=rope, gates=gates),
        grid=(bsz, nt // tm),
        in_specs=in_specs,
        out_specs=out_specs,
        out_shape=out_shape,
        compiler_params=_cparams("arbitrary", "arbitrary"),
        name="mod_project",
    )(*args)
    return out if gates else out[0]


def _rope_tables(n_lat, n_ctx):
    rows = n_lat // GRID_W
    r = jnp.repeat(jnp.arange(rows), GRID_W).astype(F32)
    col = jnp.tile(jnp.arange(GRID_W), rows).astype(F32)
    half = DA_HEAD_DIM // 2
    inv = ROPE_BASE ** (-jnp.arange(0, half, 2, dtype=F32) / half)
    ang_r, ang_c = r[:, None] * inv, col[:, None] * inv
    cos = jnp.concatenate([jnp.cos(ang_r)] * 2 + [jnp.cos(ang_c)] * 2, axis=-1)
    sin = jnp.concatenate([-jnp.sin(ang_r), jnp.sin(ang_r), -jnp.sin(ang_c), jnp.sin(ang_c)], axis=-1)
    cos = jnp.concatenate([jnp.tile(cos, (1, 2)), jnp.ones((n_ctx, LANES), F32)], axis=0)
    sin = jnp.concatenate([jnp.tile(sin, (1, 2)), jnp.zeros((n_ctx, LANES), F32)], axis=0)
    return cos, sin


def _attn_body(lam_ref, q_ref, k_ref, v_ref, gs_ref, o_ref, *, lam_init):
    lv = lam_ref[...]
    lam = (jnp.exp(jnp.sum(lv[0:1] * lv[1:2], axis=-1, keepdims=True))
           - jnp.exp(jnp.sum(lv[2:3] * lv[3:4], axis=-1, keepdims=True)) + lam_init)
    q, k, v = q_ref[0], k_ref[0], v_ref[0]
    lane = lax.broadcasted_iota(jnp.int32, q.shape, 1)
    hw = v.shape[1]
    v_one = jnp.concatenate([v, jnp.ones_like(v)], axis=1)

    rows = min(q.shape[0], ATTN_CHAIN_ROWS)
    chains = [(r, c) for r in range(0, q.shape[0], rows) for c in range(2)]
    qm = [jnp.where((lane >= c * DA_HEAD_DIM) & (lane < (c + 1) * DA_HEAD_DIM), q, jnp.zeros_like(q)) for c in range(2)]
    s = [_dot_nt(qm[c][r:r + rows], k) for r, c in chains]
    e = [jnp.exp2((x - jnp.max(x, axis=-1, keepdims=True)).astype(BF16)) for x in s]
    ov = [_dot(x, v_one) for x in e]
    att = [x[:, :hw] / x[:, hw:hw + 1] for x in ov]
    for n in range(0, len(chains), 2):
        r = chains[n][0]
        o = att[n] - lam * att[n + 1]
        o_ref[0, r:r + rows, :] = (_rms(o, gs_ref[...]) * (1.0 - lam_init)).astype(o_ref.dtype)


def _diff_attention(qkv, lam_vecs, g_subln, n_lat, depth):
    bsz, nt, d3 = qkv.shape
    d = d3 // 3
    hw = 2 * DA_HEAD_DIM
    lam_init = 0.8 - 0.6 * math.exp(-0.3 * depth)
    n_ctx = nt - n_lat

    def call(n_q, tq, q_blk0, n_k, k_blk0):
        return pl.pallas_call(
            functools.partial(_attn_body, lam_init=lam_init),
            grid=(bsz, DA_HEADS, n_q // tq),
            in_specs=[pl.BlockSpec((4, DA_HEAD_DIM), lambda b, h, i: (0, 0)),
                      pl.BlockSpec((1, tq, hw), lambda b, h, i: (b, q_blk0 + i, h)),
                      pl.BlockSpec((1, n_k, hw), lambda b, h, i: (b, k_blk0, DA_HEADS + h)),
                      pl.BlockSpec((1, n_k, hw), lambda b, h, i: (b, k_blk0, 2 * DA_HEADS + h)),
                      pl.BlockSpec((1, hw), lambda b, h, i: (0, 0))],
            out_specs=pl.BlockSpec((1, tq, hw), lambda b, h, i: (b, i, h)),
            out_shape=jax.ShapeDtypeStruct((bsz, n_q, d), BF16),
            compiler_params=_cparams("arbitrary", "arbitrary", "arbitrary"),
            name="diff_attention",
        )(lam_vecs, qkv, qkv, qkv, g_subln[None, :])

    o_lat = call(n_lat, ATTN_Q_ROWS, 0, nt, 0)
    o_ctx = call(n_ctx, n_ctx, n_lat // n_ctx, n_ctx, n_lat // n_ctx)
    return jnp.concatenate([o_lat, o_ctx], axis=1)


def _residual_and_route(h, y, mod, g_post_ref, g_ffn_ref, wr_ref, out_ref, u_ref, aff_ref):
    h_new = h + mod[GT_M:GT_M + 1, :] * _rms(y, g_post_ref[...])
    out_ref[0] = h_new
    u = _rms(h_new, g_ffn_ref[...]) * (1.0 + mod[SC_F:SC_F + 1, :]) + mod[SH_F:SH_F + 1, :]
    u_ref[0] = u.astype(BF16)
    logits = _dot_nt(wr_ref[...], u, precision=HIGHEST)
    e = jnp.exp(logits - jnp.max(logits, axis=0, keepdims=True))
    aff_ref[0] = e / jnp.sum(e, axis=0, keepdims=True)


def _route_specs(bsz, nt, d):
    in_specs = [pl.BlockSpec((1, d), lambda b, i: (0, 0)),
                pl.BlockSpec((N_EXPERTS, d), lambda b, i: (0, 0))]
    out_specs = [pl.BlockSpec((1, ROW_TILE, d), lambda b, i: (b, i, 0)),
                 pl.BlockSpec((1, ROW_TILE, d), lambda b, i: (b, i, 0)),
                 pl.BlockSpec((1, N_EXPERTS, ROW_TILE), lambda b, i: (b, 0, i))]
    out_shape = [jax.ShapeDtypeStruct((bsz, nt, d), F32),
                 jax.ShapeDtypeStruct((bsz, nt, d), BF16),
                 jax.ShapeDtypeStruct((bsz, N_EXPERTS, nt), F32)]
    return in_specs, out_specs, out_shape


def _oproj_body(o_ref, w_ref, h_ref, mod_ref, g_ref, g_ffn_ref, wr_ref, out_ref, u_ref, aff_ref):
    y = _dot(o_ref[0], w_ref[...])
    _residual_and_route(h_ref[0], y, mod_ref[0, 0], g_ref, g_ffn_ref, wr_ref, out_ref, u_ref, aff_ref)


def _out_project(o, w_bf, h, mods, g_post, g_ffn, w_router, n_lat):
    bsz, nt, d = h.shape
    lat_tiles = n_lat // ROW_TILE
    r_in, r_out, r_shape = _route_specs(bsz, nt, d)
    return pl.pallas_call(
        _oproj_body,
        grid=(bsz, nt // ROW_TILE),
        in_specs=[pl.BlockSpec((1, ROW_TILE, d), lambda b, i: (b, i, 0)),
                  pl.BlockSpec((d, d), lambda b, i: (0, 0)),
                  pl.BlockSpec((1, ROW_TILE, d), lambda b, i: (b, i, 0)),
                  pl.BlockSpec((1, 1, N_MOD, d), lambda b, i: (b, i // lat_tiles, 0, 0)),
                  pl.BlockSpec((1, d), lambda b, i: (0, 0))] + r_in,
        out_specs=r_out,
        out_shape=r_shape,
        compiler_params=_cparams("arbitrary", "arbitrary"),
        name="out_project",
    )(o, w_bf, h, mods, g_post[None, :], g_ffn[None, :], w_router.T)


def _lane_cumsum(x):
    n = x.shape[1]
    jj = lax.broadcasted_iota(jnp.int32, (ROW_TILE, ROW_TILE), 0)
    nn = lax.broadcasted_iota(jnp.int32, (ROW_TILE, ROW_TILE), 1)
    tri = (jj <= nn).astype(BF16)
    run = jnp.zeros((x.shape[0], 1), F32)
    parts = []
    for t in range(n // ROW_TILE):
        local = _dot(x[:, t * ROW_TILE:(t + 1) * ROW_TILE].astype(BF16), tri) + run
        parts.append(local)
        run = local[:, ROW_TILE - 1:ROW_TILE]
    return jnp.concatenate(parts, axis=1) if len(parts) > 1 else parts[0]


def _select_slots(aff, cap):
    bits = pltpu.bitcast(aff, jnp.int32)

    def step(t, lo):
        cand = lo | jnp.left_shift(jnp.int32(1), 30 - t)
        cnt = jnp.sum((bits >= cand).astype(F32), axis=1, keepdims=True)
        return jnp.where(cnt >= cap, cand, lo)

    thr = lax.fori_loop(0, 31, step, jnp.zeros((aff.shape[0], 1), jnp.int32))
    gt = bits > thr
    eq = bits == thr
    need = cap - jnp.sum(gt.astype(F32), axis=1, keepdims=True)
    eq_rank = _lane_cumsum(eq.astype(F32))
    sel = gt | (eq & (eq_rank <= need))
    slot = _lane_cumsum(sel.astype(F32)) - 1.0
    return jnp.where(sel, slot, -1.0).astype(jnp.int32)


def _select_body(aff_ref, pos_ref, *, n_lat, n_ctx):
    aff = aff_ref[0]
    pos_ref[0, :, :n_lat] = _select_slots(aff[:, :n_lat], CAP_FACTOR * n_lat // N_EXPERTS)
    if n_ctx:
        pos_ref[0, :, n_lat:] = _select_slots(aff[:, n_lat:], CAP_FACTOR * n_ctx // N_EXPERTS)


def _select(aff, n_lat, n_ctx):
    bsz, e, nt = aff.shape
    return pl.pallas_call(
        functools.partial(_select_body, n_lat=n_lat, n_ctx=n_ctx),
        grid=(bsz,),
        in_specs=[pl.BlockSpec((1, e, nt), lambda b: (b, 0, 0))],
        out_specs=pl.BlockSpec((1, e, nt), lambda b: (b, 0, 0)),
        out_shape=jax.ShapeDtypeStruct((bsz, e, nt), jnp.int32),
        compiler_params=_cparams("arbitrary"),
        name="moe_select",
    )(aff)


def _one_hot_slots(pos_row, cap):
    slot = lax.broadcasted_iota(jnp.int32, (cap, pos_row.shape[1]), 0)
    return pos_row == slot


def _gather_body(pos_ref, aff_ref, u_ref, xs_ref, gate_ref, *, segments):
    pos = pos_ref[0, 0]
    aff = aff_ref[0, 0]
    row = 0
    for start, n, cap in segments:
        hot = _one_hot_slots(pos[:, start:start + n], cap)
        xs_ref[0, 0, row:row + cap, :] = _dot(jnp.where(hot, 1.0, 0.0).astype(BF16),
                                              u_ref[0, start:start + n, :]).astype(BF16)
        gate = jnp.sum(jnp.where(hot, aff[:, start:start + n], 0.0), axis=1, keepdims=True)
        gate_ref[0, 0, row:row + cap, :] = jnp.broadcast_to(gate, (cap, LANES))
        row += cap


def _segments(n_lat, n_ctx):
    segs = [(0, n_lat, CAP_FACTOR * n_lat // N_EXPERTS)]
    if n_ctx:
        segs.append((n_lat, n_ctx, CAP_FACTOR * n_ctx // N_EXPERTS))
    return tuple(segs)


def _gather(pos, aff, u, n_lat, n_ctx):
    bsz, nt, d = u.shape
    e = pos.shape[1]
    segs = _segments(n_lat, n_ctx)
    rows = sum(s[2] for s in segs)
    return pl.pallas_call(
        functools.partial(_gather_body, segments=segs),
        grid=(bsz, e),
        in_specs=[pl.BlockSpec((1, 1, 1, nt), lambda b, x: (b, x, 0, 0)),
                  pl.BlockSpec((1, 1, 1, nt), lambda b, x: (b, x, 0, 0)),
                  pl.BlockSpec((1, nt, d), lambda b, x: (b, 0, 0))],
        out_specs=[pl.BlockSpec((1, 1, rows, d), lambda b, x: (b, x, 0, 0)),
                   pl.BlockSpec((1, 1, rows, LANES), lambda b, x: (b, x, 0, 0))],
        out_shape=[jax.ShapeDtypeStruct((bsz, e, rows, d), BF16),
                   jax.ShapeDtypeStruct((bsz, e, rows, LANES), F32)],
        compiler_params=_cparams("arbitrary", "arbitrary"),
        name="moe_gather",
    )(pos.reshape(bsz, e, 1, nt), aff.reshape(bsz, e, 1, nt), u)


def _expert_body(xs_ref, gate_ref, wg_ref, wu_ref, wd_ref, y_ref, wg_scr, wu_scr, wd_scr, acc_scr):
    c = pl.program_id(1)
    b = pl.program_id(2)

    @pl.when(b == 0)
    def _():
        wg_scr[...] = wg_ref[0].astype(BF16)
        wu_scr[...] = wu_ref[0].astype(BF16)
        wd_scr[...] = wd_ref[0].astype(BF16)

    xs = xs_ref[0, 0]
    part = jnp.zeros((xs.shape[0], wd_scr.shape[1]), F32)
    for s in range(wg_scr.shape[1] // FF_CHUNK):
        cols = slice(s * FF_CHUNK, (s + 1) * FF_CHUNK)
        hid = _silu(_dot(xs, wg_scr[:, cols])) * _dot(xs, wu_scr[:, cols])
        part = part + _dot(hid.astype(BF16), wd_scr[cols, :])

    @pl.when(c == 0)
    def _():
        acc_scr[b] = part

    @pl.when(c != 0)
    def _():
        y_ref[0, 0] = ((acc_scr[b] + part) * gate_ref[0, 0, :, 0:1]).astype(y_ref.dtype)


def _experts(xs, gate, wg, wu, wd):
    bsz, e, rows, d = xs.shape
    ff = wg.shape[2]
    halves = 2
    fh = ff // halves
    return pl.pallas_call(
        _expert_body,
        grid=(e, halves, bsz),
        in_specs=[pl.BlockSpec((1, 1, rows, d), lambda x, c, b: (b, x, 0, 0)),
                  pl.BlockSpec((1, 1, rows, LANES), lambda x, c, b: (b, x, 0, 0)),
                  pl.BlockSpec((1, d, fh), lambda x, c, b: (x, 0, c)),
                  pl.BlockSpec((1, d, fh), lambda x, c, b: (x, 0, c)),
                  pl.BlockSpec((1, fh, d), lambda x, c, b: (x, c, 0))],
        out_specs=pl.BlockSpec((1, 1, rows, d), lambda x, c, b: (jnp.where(c == halves - 1, b, 0), x, 0, 0)),
        out_shape=jax.ShapeDtypeStruct((bsz, e, rows, d), BF16),
        scratch_shapes=[pltpu.VMEM((d, fh), BF16), pltpu.VMEM((d, fh), BF16), pltpu.VMEM((fh, d), BF16),
                        pltpu.VMEM((bsz, rows, d), F32)],
        compiler_params=_cparams("arbitrary", "arbitrary", "arbitrary"),
        name="moe_experts",
    )(xs, gate, wg, wu, wd)


def _combine_body(pos_ref, y_ref, h_ref, mod_ref, g_ref, out_ref, *, n_lat, rows, lat_cap):
    is_ctx = pl.program_id(1) * ROW_TILE >= n_lat
    pos = pos_ref[0]
    slot = jnp.where(pos >= 0, pos + jnp.where(is_ctx, lat_cap, 0), -1)
    ids = lax.broadcasted_iota(jnp.int32, (rows, ROW_TILE), 0)
    hot = jnp.concatenate([jnp.where(slot[x:x + 1, :] == ids, 1.0, 0.0).astype(BF16)
                           for x in range(pos.shape[0])], axis=0)
    f = _dot_tn(hot, y_ref[0])
    out_ref[0] = h_ref[0] + mod_ref[0, 0, GT_F:GT_F + 1, :] * _rms(f, g_ref[...])


def _combine(pos, y, h, mods, g_post, n_lat, n_ctx):
    bsz, e, rows, d = y.shape
    nt = n_lat + n_ctx
    lat_tiles = n_lat // ROW_TILE
    return pl.pallas_call(
        functools.partial(_combine_body, n_lat=n_lat, rows=rows, lat_cap=CAP_FACTOR * n_lat // N_EXPERTS),
        grid=(bsz, nt // ROW_TILE),
        in_specs=[pl.BlockSpec((1, e, ROW_TILE), lambda b, i: (b, 0, i)),
                  pl.BlockSpec((1, e * rows, d), lambda b, i: (b, 0, 0)),
                  pl.BlockSpec((1, ROW_TILE, d), lambda b, i: (b, i, 0)),
                  pl.BlockSpec((1, 1, N_MOD, d), lambda b, i: (b, i // lat_tiles, 0, 0)),
                  pl.BlockSpec((1, d), lambda b, i: (0, 0))],
        out_specs=pl.BlockSpec((1, ROW_TILE, d), lambda b, i: (b, i, 0)),
        out_shape=jax.ShapeDtypeStruct((bsz, nt, d), F32),
        compiler_params=_cparams("arbitrary", "arbitrary"),
        name="moe_combine",
    )(pos, y.reshape(bsz, e * rows, d), h, mods, g_post[None, :])


def _moe(h, u, aff, mods, g_post, wg, wu, wd, n_lat, n_ctx):
    pos = _select(aff, n_lat, n_ctx)
    xs, gate = _gather(pos, aff, u, n_lat, n_ctx)
    y = _experts(xs, gate, wg, wu, wd)
    return _combine(pos, y, h, mods, g_post, n_lat, n_ctx)


def _conv_body(x_ref, w_ref, o_ref, pad_scr, *, n_lat):
    j = pl.program_id(1)
    nt = x_ref.shape[1]
    halo = CONV_HALO
    zeros = jnp.zeros((halo, LANES), F32)
    for lo, hi in ((0, n_lat), (n_lat, nt)):
        n = hi - lo
        if n == 0:
            continue
        pad_scr[0:halo, :] = zeros
        pad_scr[halo:halo + n, :] = x_ref[0, lo:hi, :].astype(F32)
        pad_scr[halo + n:2 * halo + n, :] = zeros
        acc = jnp.zeros((n, LANES), F32)
        for tap in range(CONV_K):
            start = halo + tap - CONV_K // 2
            acc = acc + pad_scr[start:start + n, :] * w_ref[tap:tap + 1, :]
        y = _silu(acc)
        unit = y * lax.rsqrt(jnp.sum(y * y, axis=-1, keepdims=True) + EPS)
        y = jnp.where(j < GDN_HEADS, unit * GDN_HEAD_DIM ** -0.5, jnp.where(j < 2 * GDN_HEADS, unit, y))
        o_ref[0, lo:hi, :] = y


def _short_conv(p, w_conv, n_lat):
    bsz, nt, _ = p.shape
    nblk = 3 * GDN_HEADS
    return pl.pallas_call(
        functools.partial(_conv_body, n_lat=n_lat),
        grid=(bsz, nblk),
        in_specs=[pl.BlockSpec((1, nt, LANES), lambda b, j: (b, 0, j)),
                  pl.BlockSpec((CONV_K, LANES), lambda b, j: (0, j))],
        out_specs=pl.BlockSpec((1, nt, LANES), lambda b, j: (b, 0, j)),
        out_shape=jax.ShapeDtypeStruct((bsz, nt, nblk * LANES), F32),
        scratch_shapes=[pltpu.VMEM((n_lat + 2 * CONV_HALO, LANES), F32)],
        compiler_params=_cparams("arbitrary", "arbitrary"),
        name="gdn_conv",
    )(p, w_conv)


def _gdn_body(q_ref, k_ref, v_ref, gate_ref, gate_t_ref, alog_ref, dt_ref, alog_t_ref, dt_t_ref,
              o_ref, s_scr, gct_scr, *, rev, n_lat_groups):
    hb = pl.program_id(1)
    step = pl.program_id(2)
    r_t = ROW_TILE
    c_sz = GDN_CHUNK
    goff = 2 * GDN_HEADS if rev else 0

    @pl.when(step == 0)
    def _():
        s_scr[...] = jnp.zeros_like(s_scr)

    ii = lax.broadcasted_iota(jnp.int32, (r_t, r_t), 0)
    jj = lax.broadcasted_iota(jnp.int32, (r_t, r_t), 1)
    same = (ii // c_sz) == (jj // c_sz)
    if rev:
        incl, strict = same & (jj >= ii), same & (jj > ii)
    else:
        incl, strict = same & (jj <= ii), same & (jj < ii)
    m_incl = incl.astype(BF16)
    m_same = same.astype(BF16)
    eye = (ii == jj).astype(F32)

    def pieces(x):
        hi = x.astype(BF16)
        rest = x - hi.astype(F32)
        mid = rest.astype(BF16)
        return hi, mid, (rest - mid.astype(F32)).astype(BF16)

    lane = lax.broadcasted_iota(jnp.int32, (1, LANES), 1)
    is_g = (lane >= goff) & (lane < goff + GDN_HEADS)
    raw = gate_ref[0]
    g_all = jnp.where(is_g, -jnp.exp(alog_ref[...]) * _softplus(raw + dt_ref[...]), 0.0)
    beta_all = _sigmoid(raw)
    g_parts = pieces(g_all)
    gc_all = sum(_dot(m_incl, x) for x in g_parts)
    gl_all = sum(_dot(m_same, x) for x in g_parts)
    g_t = -jnp.exp(alog_t_ref[...]) * _softplus(gate_t_ref[0] + dt_t_ref[...])
    gct_scr[...] = sum(_dot_nt(x, m_incl) for x in pieces(g_t))
    lane_f = lax.broadcasted_iota(jnp.int32, (r_t, LANES), 1)

    def column(a, idx):
        return jnp.sum(jnp.where(lane_f == idx, a, 0.0), axis=1, keepdims=True)

    heads = range(GDN_HEADS_PER_STEP)
    n_chunks = r_t // c_sz
    n_levels = int(math.log2(c_sz)) - 1
    st = []
    for hh in heads:
        head = hb * GDN_HEADS_PER_STEP + hh
        cols = slice(hh * GDN_HEAD_DIM, (hh + 1) * GDN_HEAD_DIM)
        q, k, v = q_ref[0, :, cols], k_ref[0, :, cols], v_ref[0, :, cols]
        gc = column(gc_all, goff + head)
        gl = column(gl_all, goff + head)
        beta = column(beta_all, goff + GDN_HEADS + head)
        gc_row = gct_scr[pl.ds(goff + head, 1), :]
        decay = jnp.where(incl, jnp.exp(jnp.where(incl, gc - gc_row, 0.0)), 0.0)
        kb = k * beta
        k_bf = k.astype(BF16)
        a_mat = jnp.where(strict, _dot_nt(kb.astype(BF16), k_bf) * decay, 0.0)
        st.append(dict(
            cols=cols, gl=gl, inv=eye - a_mat, pw=a_mat,
            rhs=jnp.concatenate([v * beta, kb * jnp.exp(gc)], axis=1).astype(BF16),
            intra=(_dot_nt(q.astype(BF16), k_bf) * decay).astype(BF16),
            q_dec=(q * jnp.exp(gc)).astype(BF16),
            k_dec_t=(k * jnp.exp(gl - gc)).T.astype(BF16),
            s=s_scr[hh], outs=[None] * n_chunks))
    for _ in range(n_levels):
        for t in st:
            pw_bf = t["pw"].astype(BF16)
            t["pw"] = _dot(pw_bf, pw_bf)
        for t in st:
            t["inv"] = t["inv"] + _dot(t["inv"].astype(BF16), t["pw"].astype(BF16))
    for t in st:
        uw = _dot(t["inv"].astype(BF16), t["rhs"])
        t["u"], t["w"] = uw[:, :GDN_HEAD_DIM], uw[:, GDN_HEAD_DIM:].astype(BF16)
    for c in (range(n_chunks - 1, -1, -1) if rev else range(n_chunks)):
        rows = slice(c * c_sz, (c + 1) * c_sz)
        for t in st:
            t["s_bf"] = t["s"].astype(BF16)
            t["v_new"] = (t["u"][rows] - _dot(t["w"][rows], t["s_bf"])).astype(BF16)
        for t in st:
            t["outs"][c] = _dot(t["q_dec"][rows], t["s_bf"]) + _dot(t["intra"][rows, rows], t["v_new"])
            t["s"] = (t["s"] * jnp.exp(t["gl"][c * c_sz:c * c_sz + 1, :])
                      + _dot(t["k_dec_t"][:, rows], t["v_new"]))
    for hh, t in enumerate(st):
        s_scr[hh] = t["s"]
        o_ref[0, :, t["cols"]] = jnp.concatenate(t["outs"], axis=0)


def _gdn_scan(qkv, gates, gates_t, a_log_row, dt_row, a_log_col, dt_col, n_lat, rev):
    bsz, nt, _ = qkv.shape
    hps = GDN_HEADS_PER_STEP
    hblocks = GDN_HEADS // hps
    n_groups = nt // ROW_TILE
    lat_groups = n_lat // ROW_TILE
    width = hps * GDN_HEAD_DIM

    def group(i):
        lat = (lat_groups - i) if rev else (i - 1)
        return jnp.where(i == 0, n_groups - 1, lat)

    return pl.pallas_call(
        functools.partial(_gdn_body, rev=rev, n_lat_groups=lat_groups),
        grid=(bsz, hblocks, n_groups),
        in_specs=[pl.BlockSpec((1, ROW_TILE, width), lambda b, h, i: (b, group(i), h)),
                  pl.BlockSpec((1, ROW_TILE, width), lambda b, h, i: (b, group(i), hblocks + h)),
                  pl.BlockSpec((1, ROW_TILE, width), lambda b, h, i: (b, group(i), 2 * hblocks + h)),
                  pl.BlockSpec((1, ROW_TILE, LANES), lambda b, h, i: (b, group(i), 0)),
                  pl.BlockSpec((1, 4 * GDN_HEADS, ROW_TILE), lambda b, h, i: (b, 0, group(i))),
                  pl.BlockSpec((1, LANES), lambda b, h, i: (0, 0)),
                  pl.BlockSpec((1, LANES), lambda b, h, i: (0, 0)),
                  pl.BlockSpec((4 * GDN_HEADS, 1), lambda b, h, i: (0, 0)),
                  pl.BlockSpec((4 * GDN_HEADS, 1), lambda b, h, i: (0, 0))],
        out_specs=pl.BlockSpec((1, ROW_TILE, width), lambda b, h, i: (b, group(i), h)),
        out_shape=jax.ShapeDtypeStruct((bsz, nt, GDN_HEADS * GDN_HEAD_DIM), F32),
        scratch_shapes=[pltpu.VMEM((hps, GDN_HEAD_DIM, GDN_HEAD_DIM), F32),
                        pltpu.VMEM((4 * GDN_HEADS, ROW_TILE), F32)],
        compiler_params=_cparams("arbitrary", "arbitrary", "arbitrary"),
        name="gdn_scan_bwd" if rev else "gdn_scan_fwd",
    )(qkv, qkv, qkv, gates, gates_t, a_log_row, dt_row, a_log_col, dt_col)


def _gdn_out_body(of_ref, ob_ref, z_ref, gn_ref, w_ref, h_ref, mod_ref, g_ref, g_ffn_ref, wr_ref,
                  out_ref, u_ref, aff_ref, a_scr):
    for hh in range(GDN_HEADS):
        cols = slice(hh * GDN_HEAD_DIM, (hh + 1) * GDN_HEAD_DIM)
        o = of_ref[0, :, cols] + ob_ref[0, :, cols]
        a_scr[:, cols] = (_rms(o, gn_ref[...]) * _silu(z_ref[0, :, cols].astype(F32))).astype(BF16)
    y = _dot(a_scr[...], w_ref[...])
    _residual_and_route(h_ref[0], y, mod_ref[0, 0], g_ref, g_ffn_ref, wr_ref, out_ref, u_ref, aff_ref)


def _gdn_out(o_f, o_b, p, g_onorm, w_bf, h, mods, g_post, g_ffn, w_router, n_lat):
    bsz, _, d = h.shape
    width = GDN_HEADS * GDN_HEAD_DIM
    r_in, r_out, r_shape = _route_specs(bsz, n_lat, d)
    return pl.pallas_call(
        _gdn_out_body,
        grid=(bsz, n_lat // ROW_TILE),
        in_specs=[pl.BlockSpec((1, ROW_TILE, width), lambda b, i: (b, i, 0)),
                  pl.BlockSpec((1, ROW_TILE, width), lambda b, i: (b, i, 0)),
                  pl.BlockSpec((1, ROW_TILE, width), lambda b, i: (b, i, 3)),
                  pl.BlockSpec((1, GDN_HEAD_DIM), lambda b, i: (0, 0)),
                  pl.BlockSpec((width, d), lambda b, i: (0, 0)),
                  pl.BlockSpec((1, ROW_TILE, d), lambda b, i: (b, i, 0)),
                  pl.BlockSpec((1, 1, N_MOD, d), lambda b, i: (b, 0, 0, 0)),
                  pl.BlockSpec((1, d), lambda b, i: (0, 0))] + r_in,
        out_specs=r_out,
        out_shape=r_shape,
        scratch_shapes=[pltpu.VMEM((ROW_TILE, width), BF16)],
        compiler_params=_cparams("arbitrary", "arbitrary"),
        name="gdn_out",
    )(o_f, o_b, p, g_onorm[None, :], w_bf, h, mods, g_post[None, :], g_ffn[None, :], w_router.T)


def _gate_vectors(a_log_f, dt_bias_f, a_log_b, dt_bias_b):
    z = jnp.zeros((GDN_HEADS,), F32)
    a = jnp.concatenate([a_log_f, z, a_log_b, z])
    d = jnp.concatenate([dt_bias_f, z, dt_bias_b, z])
    pad = jnp.zeros((LANES - 4 * GDN_HEADS,), F32)
    return (jnp.concatenate([a, pad])[None, :], jnp.concatenate([d, pad])[None, :], a[:, None], d[:, None])


def kernel(x, c, ctx, c_ctx, l0_w_ada, l0_b_ada, l0_g_pre_mix, l0_g_post_mix, l0_g_pre_ffn, l0_g_post_ffn, l0_w_qkv, l0_lambda_q1, l0_lambda_k1, l0_lambda_q2, l0_lambda_k2, l0_g_subln, l0_w_o, l0_w_router, l0_w_gate, l0_w_up, l0_w_down, l1_w_ada, l1_b_ada, l1_g_pre_mix, l1_g_post_mix, l1_g_pre_ffn, l1_g_post_ffn, l1_w_in, l1_w_conv, l1_a_log_f, l1_dt_bias_f, l1_a_log_b, l1_dt_bias_b, l1_g_onorm, l1_w_o, l1_w_router, l1_w_gate, l1_w_up, l1_w_down):
    n_lat, n_ctx = x.shape[1], ctx.shape[1]
    h = jnp.concatenate([x, ctx], axis=1)

    mods = _ada_mods(c, c_ctx, l0_w_ada, l0_b_ada)
    qkv = _project(h, mods, l0_g_pre_mix, l0_w_qkv.astype(BF16), n_lat, rope_tables=_rope_tables(n_lat, n_ctx))
    lam_vecs = jnp.stack([l0_lambda_q1, l0_lambda_k1, l0_lambda_q2, l0_lambda_k2])
    o = _diff_attention(qkv, lam_vecs, l0_g_subln, n_lat, depth=0)
    h, u, aff = _out_project(o, l0_w_o.astype(BF16), h, mods, l0_g_post_mix, l0_g_pre_ffn, l0_w_router, n_lat)
    h = _moe(h, u, aff, mods, l0_g_post_ffn, l0_w_gate, l0_w_up, l0_w_down, n_lat, n_ctx)

    mods = _ada_mods(c, c_ctx, l1_w_ada, l1_b_ada)
    width = GDN_HEADS * GDN_HEAD_DIM
    w_gates = jnp.pad(l1_w_in[:, 4 * width:], ((0, 0), (0, LANES - 4 * GDN_HEADS))).astype(BF16)
    p, gates = _project(h, mods, l1_g_pre_mix, l1_w_in[:, :4 * width].astype(BF16), n_lat, w_gates=w_gates)
    gates_t = jnp.swapaxes(gates[:, :, :4 * GDN_HEADS], 1, 2)
    qkv = _short_conv(p, l1_w_conv, n_lat)
    gv = _gate_vectors(l1_a_log_f, l1_dt_bias_f, l1_a_log_b, l1_dt_bias_b)
    o_f = _gdn_scan(qkv, gates, gates_t, *gv, n_lat, rev=False)
    o_b = _gdn_scan(qkv, gates, gates_t, *gv, n_lat, rev=True)
    hl, u, aff = _gdn_out(o_f, o_b, p, l1_g_onorm, l1_w_o.astype(BF16), h, mods, l1_g_post_mix, l1_g_pre_ffn,
                          l1_w_router, n_lat)
    return _moe(hl, u, aff, mods, l1_g_post_ffn, l1_w_gate, l1_w_up, l1_w_down, n_lat, 0)
```

```python
import functools
import math

import jax
import jax.numpy as jnp
from jax import lax
from jax.experimental import pallas as pl
from jax.experimental.pallas import tpu as pltpu

F32 = jnp.float32
BF16 = jnp.bfloat16
HIGHEST = lax.Precision.HIGHEST

EPS = 1e-6
N_MOD = 6
GRID_W = 64
ROPE_BASE = 10000.0
DA_HEADS = 8
DA_HEAD_DIM = 64
GDN_HEADS = 8
GDN_HEAD_DIM = 128
CONV_K = 5
CONV_HALO = 8
N_EXPERTS = 16
CAP_FACTOR = 2

LANES = 128
ROW_TILE = 256
PROJ_ROWS = 768
ATTN_Q_ROWS = 512
ATTN_CHAIN_ROWS = 128
Q_PRESCALE = DA_HEAD_DIM ** -0.5 * math.log2(math.e)
GDN_CHUNK = 64
GDN_HEADS_PER_STEP = 8
FF_CHUNK = 512
COMBINE_WINDOW = 64
SLOT_ALIGN = 16
TABLE_COUNT = 64
VMEM_LIMIT = 56 * 1024 * 1024

SH_M, SC_M, GT_M, SH_F, SC_F, GT_F = range(6)


def _cparams(*sem):
    return pltpu.CompilerParams(dimension_semantics=sem, vmem_limit_bytes=VMEM_LIMIT)


def _dot(a, b):
    return jnp.dot(a, b, preferred_element_type=F32)


def _dot_nt(a, b, precision=None):
    return lax.dot_general(a, b, (((1,), (1,)), ((), ())), preferred_element_type=F32, precision=precision)


def _dot_tn(a, b):
    return lax.dot_general(a, b, (((0,), (0,)), ((), ())), preferred_element_type=F32)


def _rms(x, g):
    return x * lax.rsqrt(jnp.mean(x * x, axis=-1, keepdims=True) + EPS) * g


def _sigmoid(x):
    return 1.0 / (1.0 + jnp.exp(-x))


def _silu(x):
    return x * _sigmoid(x)


def _softplus(x):
    return jnp.maximum(x, 0.0) + jnp.log(1.0 + jnp.exp(-jnp.abs(x)))


def _ada_body(c_ref, w_ref, b_ref, o_ref):
    o_ref[...] = jnp.dot(_silu(c_ref[...]), w_ref[...], precision=HIGHEST, preferred_element_type=F32) + b_ref[...]


def _ada_mods(c, c_ctx, w_ada, b_ada):
    bsz, d = c.shape
    rows = 16
    cc = jnp.concatenate([c, c_ctx[None, :], jnp.zeros((rows - bsz - 1, d), F32)], axis=0)
    tn = 1024
    m = pl.pallas_call(
        _ada_body,
        grid=(N_MOD * d // tn,),
        in_specs=[pl.BlockSpec((rows, d), lambda j: (0, 0)),
                  pl.BlockSpec((d, tn), lambda j: (0, j)),
                  pl.BlockSpec((1, tn), lambda j: (0, j))],
        out_specs=pl.BlockSpec((rows, tn), lambda j: (0, j)),
        out_shape=jax.ShapeDtypeStruct((rows, N_MOD * d), F32),
        compiler_params=_cparams("arbitrary"),
        name="ada_mods",
    )(cc, w_ada, b_ada[None, :])
    lat = m[:bsz].reshape(bsz, 1, N_MOD, d)
    ctx = jnp.broadcast_to(m[bsz].reshape(1, 1, N_MOD, d), (bsz, 1, N_MOD, d))
    return jnp.concatenate([lat, ctx], axis=1)


def _proj_body(h_ref, mod_ref, g_ref, w_ref, *rest, n_lat, rope, gates):
    rest = list(rest)
    cos_ref, sin_ref = (rest.pop(0), rest.pop(0)) if rope else (None, None)
    wg_ref = rest.pop(0) if gates else None
    o_ref = rest.pop(0)
    og_ref = rest.pop(0) if gates else None
    x = h_ref[0]
    tm, d = x.shape
    row = pl.program_id(1) * tm + lax.broadcasted_iota(jnp.int32, (tm, 1), 0)
    is_ctx = row >= n_lat
    scale = jnp.where(is_ctx, mod_ref[0, 1, SC_M:SC_M + 1, :], mod_ref[0, 0, SC_M:SC_M + 1, :])
    shift = jnp.where(is_ctx, mod_ref[0, 1, SH_M:SH_M + 1, :], mod_ref[0, 0, SH_M:SH_M + 1, :])
    u = (_rms(x, g_ref[...]) * (1.0 + scale) + shift).astype(BF16)
    if gates:
        og_ref[0] = _dot(u, wg_ref[...])
    for j in range(w_ref.shape[1] // d):
        cols = slice(j * d, (j + 1) * d)
        acc = _dot(u, w_ref[:, cols])
        if rope and j < 2:
            reps = d // LANES
            cos = jnp.tile(cos_ref[...], (1, reps))
            sin = jnp.tile(sin_ref[...], (1, reps))
            lane = lax.broadcasted_iota(jnp.int32, acc.shape, 1)
            half = DA_HEAD_DIM // 4
            first = (lane % (2 * half)) < half
            partner = jnp.where(first, pltpu.roll(acc, d - half, 1), pltpu.roll(acc, half, 1))
            acc = acc * cos + partner * sin
            if j == 0:
                acc = acc * Q_PRESCALE
        o_ref[0, :, cols] = acc.astype(o_ref.dtype)


def _project(h, mods, g, w_bf, n_lat, rope_tables=None, w_gates=None):
    bsz, nt, d = h.shape
    n_out = w_bf.shape[1]
    tm = PROJ_ROWS
    rope = rope_tables is not None
    gates = w_gates is not None
    in_specs = [pl.BlockSpec((1, tm, d), lambda b, i: (b, i, 0)),
                pl.BlockSpec((1, 2, N_MOD, d), lambda b, i: (b, 0, 0, 0)),
                pl.BlockSpec((1, d), lambda b, i: (0, 0)),
                pl.BlockSpec((d, n_out), lambda b, i: (0, 0))]
    args = [h, mods, g[None, :], w_bf]
    out_specs = [pl.BlockSpec((1, tm, n_out), lambda b, i: (b, i, 0))]
    out_shape = [jax.ShapeDtypeStruct((bsz, nt, n_out), BF16)]
    if rope:
        in_specs += [pl.BlockSpec((tm, LANES), lambda b, i: (i, 0))] * 2
        args += list(rope_tables)
    if gates:
        in_specs.append(pl.BlockSpec((d, LANES), lambda b, i: (0, 0)))
        args.append(w_gates)
        out_specs.append(pl.BlockSpec((1, tm, LANES), lambda b, i: (b, i, 0)))
        out_shape.append(jax.ShapeDtypeStruct((bsz, nt, LANES), F32))
    out = pl.pallas_call(
        functools.partial(_proj_body, n_lat=n_lat, rope=rope, gates=gates),
        grid=(bsz, nt // tm),
        in_specs=in_specs,
        out_specs=out_specs,
        out_shape=out_shape,
        compiler_params=_cparams("arbitrary", "arbitrary"),
        name="mod_project",
    )(*args)
    return out if gates else out[0]


def _rope_tables(n_lat, n_ctx):
    rows = n_lat // GRID_W
    r = jnp.repeat(jnp.arange(rows), GRID_W).astype(F32)
    col = jnp.tile(jnp.arange(GRID_W), rows).astype(F32)
    half = DA_HEAD_DIM // 2
    inv = ROPE_BASE ** (-jnp.arange(0, half, 2, dtype=F32) / half)
    ang_r, ang_c = r[:, None] * inv, col[:, None] * inv
    cos = jnp.concatenate([jnp.cos(ang_r)] * 2 + [jnp.cos(ang_c)] * 2, axis=-1)
    sin = jnp.concatenate([-jnp.sin(ang_r), jnp.sin(ang_r), -jnp.sin(ang_c), jnp.sin(ang_c)], axis=-1)
    cos = jnp.concatenate([jnp.tile(cos, (1, 2)), jnp.ones((n_ctx, LANES), F32)], axis=0)
    sin = jnp.concatenate([jnp.tile(sin, (1, 2)), jnp.zeros((n_ctx, LANES), F32)], axis=0)
    return cos, sin


def _attn_body(lam_ref, q_ref, k_ref, v_ref, gs_ref, o_ref, *, lam_init):
    lv = lam_ref[...]
    lam = (jnp.exp(jnp.sum(lv[0:1] * lv[1:2], axis=-1, keepdims=True))
           - jnp.exp(jnp.sum(lv[2:3] * lv[3:4], axis=-1, keepdims=True)) + lam_init)
    q, k, v = q_ref[0], k_ref[0], v_ref[0]
    lane = lax.broadcasted_iota(jnp.int32, q.shape, 1)
    hw = v.shape[1]
    v_one = jnp.concatenate([v, jnp.ones_like(v)], axis=1)

    rows = min(q.shape[0], ATTN_CHAIN_ROWS)
    starts = range(0, q.shape[0], rows)
    qm = [jnp.where((lane >= c * DA_HEAD_DIM) & (lane < (c + 1) * DA_HEAD_DIM), q, jnp.zeros_like(q)) for c in range(2)]
    s = [_dot_nt(jnp.concatenate([qm[0][r:r + rows], qm[1][r:r + rows]], axis=0), k) for r in starts]
    e = [jnp.exp2((x - jnp.max(x, axis=-1, keepdims=True)).astype(BF16)) for x in s]
    ov = [_dot(x, v_one) for x in e]
    att = [x[:, :hw] / x[:, hw:hw + 1] for x in ov]
    for r, a in zip(starts, att):
        o = a[:rows] - lam * a[rows:]
        o_ref[0, r:r + rows, :] = (_rms(o, gs_ref[...]) * (1.0 - lam_init)).astype(o_ref.dtype)


def _diff_attention(qkv, lam_vecs, g_subln, n_lat, depth):
    bsz, nt, d3 = qkv.shape
    d = d3 // 3
    hw = 2 * DA_HEAD_DIM
    lam_init = 0.8 - 0.6 * math.exp(-0.3 * depth)
    n_ctx = nt - n_lat

    def call(n_q, tq, q_blk0, n_k, k_blk0):
        return pl.pallas_call(
            functools.partial(_attn_body, lam_init=lam_init),
            grid=(bsz, DA_HEADS, n_q // tq),
            in_specs=[pl.BlockSpec((4, DA_HEAD_DIM), lambda b, h, i: (0, 0)),
                      pl.BlockSpec((1, tq, hw), lambda b, h, i: (b, q_blk0 + i, h)),
                      pl.BlockSpec((1, n_k, hw), lambda b, h, i: (b, k_blk0, DA_HEADS + h)),
                      pl.BlockSpec((1, n_k, hw), lambda b, h, i: (b, k_blk0, 2 * DA_HEADS + h)),
                      pl.BlockSpec((1, hw), lambda b, h, i: (0, 0))],
            out_specs=pl.BlockSpec((1, tq, hw), lambda b, h, i: (b, i, h)),
            out_shape=jax.ShapeDtypeStruct((bsz, n_q, d), BF16),
            compiler_params=_cparams("arbitrary", "arbitrary", "arbitrary"),
            name="diff_attention",
        )(lam_vecs, qkv, qkv, qkv, g_subln[None, :])

    o_lat = call(n_lat, ATTN_Q_ROWS, 0, nt, 0)
    o_ctx = call(n_ctx, n_ctx, n_lat // n_ctx, n_ctx, n_lat // n_ctx)
    return jnp.concatenate([o_lat, o_ctx], axis=1)


def _residual_and_route(h, y, mod, g_post_ref, g_ffn_ref, wr_ref, out_ref, u_ref, aff_ref):
    h_new = h + mod[GT_M:GT_M + 1, :] * _rms(y, g_post_ref[...])
    out_ref[0] = h_new
    u = _rms(h_new, g_ffn_ref[...]) * (1.0 + mod[SC_F:SC_F + 1, :]) + mod[SH_F:SH_F + 1, :]
    u_hi = u.astype(BF16)
    u_ref[0] = u_hi
    u_lo = (u - u_hi.astype(F32)).astype(BF16)
    w = wr_ref[...]
    w_hi = w.astype(BF16)
    w_lo = (w - w_hi.astype(F32)).astype(BF16)
    logits = (_dot(u_hi, w_hi) + _dot(u_hi, w_lo) + _dot(u_lo, w_hi)).T[:N_EXPERTS]
    e = jnp.exp(logits - jnp.max(logits, axis=0, keepdims=True))
    aff_ref[0] = e / jnp.sum(e, axis=0, keepdims=True)


def _pad_lanes(w):
    return jnp.pad(w, ((0, 0), (0, LANES - w.shape[1])))


def _route_specs(bsz, nt, d):
    in_specs = [pl.BlockSpec((1, d), lambda b, i: (0, 0)),
                pl.BlockSpec((d, LANES), lambda b, i: (0, 0))]
    out_specs = [pl.BlockSpec((1, ROW_TILE, d), lambda b, i: (b, i, 0)),
                 pl.BlockSpec((1, ROW_TILE, d), lambda b, i: (b, i, 0)),
                 pl.BlockSpec((1, N_EXPERTS, ROW_TILE), lambda b, i: (b, 0, i))]
    out_shape = [jax.ShapeDtypeStruct((bsz, nt, d), F32),
                 jax.ShapeDtypeStruct((bsz, nt, d), BF16),
                 jax.ShapeDtypeStruct((bsz, N_EXPERTS, nt), F32)]
    return in_specs, out_specs, out_shape


def _oproj_body(o_ref, w_ref, h_ref, mod_ref, g_ref, g_ffn_ref, wr_ref, out_ref, u_ref, aff_ref):
    y = _dot(o_ref[0], w_ref[...])
    _residual_and_route(h_ref[0], y, mod_ref[0, 0], g_ref, g_ffn_ref, wr_ref, out_ref, u_ref, aff_ref)


def _out_project(o, w_bf, h, mods, g_post, g_ffn, w_router, n_lat):
    bsz, nt, d = h.shape
    lat_tiles = n_lat // ROW_TILE
    r_in, r_out, r_shape = _route_specs(bsz, nt, d)
    return pl.pallas_call(
        _oproj_body,
        grid=(bsz, nt // ROW_TILE),
        in_specs=[pl.BlockSpec((1, ROW_TILE, d), lambda b, i: (b, i, 0)),
                  pl.BlockSpec((d, d), lambda b, i: (0, 0)),
                  pl.BlockSpec((1, ROW_TILE, d), lambda b, i: (b, i, 0)),
                  pl.BlockSpec((1, 1, N_MOD, d), lambda b, i: (b, i // lat_tiles, 0, 0)),
                  pl.BlockSpec((1, d), lambda b, i: (0, 0))] + r_in,
        out_specs=r_out,
        out_shape=r_shape,
        compiler_params=_cparams("arbitrary", "arbitrary"),
        name="out_project",
    )(o, w_bf, h, mods, g_post[None, :], g_ffn[None, :], _pad_lanes(w_router))


def _lane_cumsum(x):
    n = x.shape[1]
    jj = lax.broadcasted_iota(jnp.int32, (ROW_TILE, ROW_TILE), 0)
    nn = lax.broadcasted_iota(jnp.int32, (ROW_TILE, ROW_TILE), 1)
    tri = (jj <= nn).astype(BF16)
    run = jnp.zeros((x.shape[0], 1), F32)
    parts = []
    for t in range(n // ROW_TILE):
        local = _dot(x[:, t * ROW_TILE:(t + 1) * ROW_TILE].astype(BF16), tri) + run
        parts.append(local)
        run = local[:, ROW_TILE - 1:ROW_TILE]
    return jnp.concatenate(parts, axis=1) if len(parts) > 1 else parts[0]


def _select_slots(aff, cap):
    bits = pltpu.bitcast(aff, jnp.int32)

    def step(t, lo):
        cand = lo | jnp.left_shift(jnp.int32(1), 30 - t)
        cnt = jnp.sum((bits >= cand).astype(F32), axis=1, keepdims=True)
        return jnp.where(cnt >= cap, cand, lo)

    thr = lax.fori_loop(0, 31, step, jnp.zeros((aff.shape[0], 1), jnp.int32))
    gt = bits > thr
    eq = bits == thr
    need = cap - jnp.sum(gt.astype(F32), axis=1, keepdims=True)
    eq_rank = _lane_cumsum(eq.astype(F32))
    sel = gt | (eq & (eq_rank <= need))
    sel_f = sel.astype(F32)
    slot = _lane_cumsum(sel_f) - 1.0
    return jnp.where(sel, slot, -1.0).astype(jnp.int32), sel_f


def _tile_table(sel_f, tile0):
    lane = lax.broadcasted_iota(jnp.int32, (sel_f.shape[0], LANES), 1)
    run = jnp.zeros((sel_f.shape[0], 1), F32)
    tab = jnp.zeros((sel_f.shape[0], LANES), F32)
    for t in range(sel_f.shape[1] // ROW_TILE):
        inside = jnp.sum(sel_f[:, t * ROW_TILE:(t + 1) * ROW_TILE], axis=1, keepdims=True)
        tab = jnp.where(lane == tile0 + t, run, tab)
        tab = jnp.where(lane == TABLE_COUNT + tile0 + t, inside, tab)
        run = run + inside
    return tab


def _select_body(aff_ref, pos_ref, tab_ref, *, n_lat, n_ctx):
    aff = aff_ref[0]
    pos, sel_f = _select_slots(aff[:, :n_lat], CAP_FACTOR * n_lat // N_EXPERTS)
    pos_ref[0, :, :n_lat] = pos
    tab = _tile_table(sel_f, 0)
    if n_ctx:
        pos, sel_f = _select_slots(aff[:, n_lat:], CAP_FACTOR * n_ctx // N_EXPERTS)
        pos_ref[0, :, n_lat:] = pos
        tab = tab + _tile_table(sel_f, n_lat // ROW_TILE)
    tab_ref[0] = tab.astype(jnp.int32)


def _select(aff, n_lat, n_ctx):
    bsz, e, nt = aff.shape
    n_tiles = nt // ROW_TILE
    pos, tab = pl.pallas_call(
        functools.partial(_select_body, n_lat=n_lat, n_ctx=n_ctx),
        grid=(bsz,),
        in_specs=[pl.BlockSpec((1, e, nt), lambda b: (b, 0, 0))],
        out_specs=[pl.BlockSpec((1, e, nt), lambda b: (b, 0, 0)),
                   pl.BlockSpec((1, e, LANES), lambda b: (b, 0, 0))],
        out_shape=[jax.ShapeDtypeStruct((bsz, e, nt), jnp.int32),
                   jax.ShapeDtypeStruct((bsz, e, LANES), jnp.int32)],
        compiler_params=_cparams("arbitrary"),
        name="moe_select",
    )(aff)
    first = tab[:, :, :n_tiles].reshape(-1)
    count = tab[:, :, TABLE_COUNT:TABLE_COUNT + n_tiles].reshape(-1)
    return pos, first, count


def _one_hot_slots(pos_row, cap):
    slot = lax.broadcasted_iota(jnp.int32, (cap, pos_row.shape[1]), 0)
    return pos_row == slot


def _gather_body(pos_ref, aff_ref, u_ref, xs_ref, gate_ref, *, segments):
    pos = pos_ref[0, 0]
    aff = aff_ref[0, 0]
    row = 0
    for start, n, cap in segments:
        hot = _one_hot_slots(pos[:, start:start + n], cap)
        xs_ref[0, 0, row:row + cap, :] = _dot(jnp.where(hot, 1.0, 0.0).astype(BF16),
                                              u_ref[0, start:start + n, :]).astype(BF16)
        gate = jnp.sum(jnp.where(hot, aff[:, start:start + n], 0.0), axis=1, keepdims=True)
        gate_ref[0, 0, row:row + cap, :] = jnp.broadcast_to(gate, (cap, LANES))
        row += cap


def _segments(n_lat, n_ctx):
    segs = [(0, n_lat, CAP_FACTOR * n_lat // N_EXPERTS)]
    if n_ctx:
        segs.append((n_lat, n_ctx, CAP_FACTOR * n_ctx // N_EXPERTS))
    return tuple(segs)


def _gather(pos, aff, u, n_lat, n_ctx):
    bsz, nt, d = u.shape
    e = pos.shape[1]
    segs = _segments(n_lat, n_ctx)
    rows = sum(s[2] for s in segs)
    return pl.pallas_call(
        functools.partial(_gather_body, segments=segs),
        grid=(bsz, e),
        in_specs=[pl.BlockSpec((1, 1, 1, nt), lambda b, x: (b, x, 0, 0)),
                  pl.BlockSpec((1, 1, 1, nt), lambda b, x: (b, x, 0, 0)),
                  pl.BlockSpec((1, nt, d), lambda b, x: (b, 0, 0))],
        out_specs=[pl.BlockSpec((1, 1, rows, d), lambda b, x: (b, x, 0, 0)),
                   pl.BlockSpec((1, 1, rows, LANES), lambda b, x: (b, x, 0, 0))],
        out_shape=[jax.ShapeDtypeStruct((bsz, e, rows, d), BF16),
                   jax.ShapeDtypeStruct((bsz, e, rows, LANES), F32)],
        compiler_params=_cparams("arbitrary", "arbitrary"),
        name="moe_gather",
    )(pos.reshape(bsz, e, 1, nt), aff.reshape(bsz, e, 1, nt), u)


def _expert_body(xs_ref, gate_ref, wg_ref, wu_ref, wd_ref, y_ref, wg_scr, wu_scr, wd_scr, acc_scr):
    c = pl.program_id(1)
    b = pl.program_id(2)

    @pl.when(b == 0)
    def _():
        wg_scr[...] = wg_ref[0].astype(BF16)
        wu_scr[...] = wu_ref[0].astype(BF16)
        wd_scr[...] = wd_ref[0].astype(BF16)

    xs = xs_ref[0, 0]
    part = jnp.zeros((xs.shape[0], wd_scr.shape[1]), F32)
    for s in range(wg_scr.shape[1] // FF_CHUNK):
        cols = slice(s * FF_CHUNK, (s + 1) * FF_CHUNK)
        hid = _silu(_dot(xs, wg_scr[:, cols])) * _dot(xs, wu_scr[:, cols])
        part = part + _dot(hid.astype(BF16), wd_scr[cols, :])

    @pl.when(c == 0)
    def _():
        acc_scr[b] = part

    @pl.when(c != 0)
    def _():
        y_ref[0, 0] = ((acc_scr[b] + part) * gate_ref[0, 0, :, 0:1]).astype(y_ref.dtype)


def _experts(xs, gate, wg, wu, wd):
    bsz, e, rows, d = xs.shape
    ff = wg.shape[2]
    halves = 2
    fh = ff // halves
    return pl.pallas_call(
        _expert_body,
        grid=(e, halves, bsz),
        in_specs=[pl.BlockSpec((1, 1, rows, d), lambda x, c, b: (b, x, 0, 0)),
                  pl.BlockSpec((1, 1, rows, LANES), lambda x, c, b: (b, x, 0, 0)),
                  pl.BlockSpec((1, d, fh), lambda x, c, b: (x, 0, c)),
                  pl.BlockSpec((1, d, fh), lambda x, c, b: (x, 0, c)),
                  pl.BlockSpec((1, fh, d), lambda x, c, b: (x, c, 0))],
        out_specs=pl.BlockSpec((1, 1, rows, d), lambda x, c, b: (jnp.where(c == halves - 1, b, 0), x, 0, 0)),
        out_shape=jax.ShapeDtypeStruct((bsz, e, rows, d), BF16),
        scratch_shapes=[pltpu.VMEM((d, fh), BF16), pltpu.VMEM((d, fh), BF16), pltpu.VMEM((fh, d), BF16),
                        pltpu.VMEM((bsz, rows, d), F32)],
        compiler_params=_cparams("arbitrary", "arbitrary", "arbitrary"),
        name="moe_experts",
    )(xs, gate, wg, wu, wd)


def _combine_body(first_ref, count_ref, pos_ref, y_ref, h_ref, mod_ref, g_ref, out_ref, acc_scr,
                  *, n_lat, rows, lat_cap):
    b = pl.program_id(0)
    tile = pl.program_id(1)
    n_exp = pos_ref.shape[1]
    n_tiles = pl.num_programs(1)
    win = COMBINE_WINDOW
    base = jnp.where(tile * ROW_TILE >= n_lat, lat_cap, 0)
    pos = pos_ref[0]
    slot = jnp.where(pos >= 0, pos + base, -1)
    starts, rounds = [], 0
    for x in range(n_exp):
        at = (b * n_exp + x) * n_tiles + tile
        first = first_ref[at]
        start = base + (first // SLOT_ALIGN) * SLOT_ALIGN
        starts.append(start)
        rounds = jnp.maximum(rounds, (base + first + count_ref[at] - start + win - 1) // win)
    acc_scr[...] = jnp.zeros_like(acc_scr)
    ids0 = lax.broadcasted_iota(jnp.int32, (win, ROW_TILE), 0)

    def one_round(r, carry):
        group = ROW_TILE // win
        for g0 in range(0, n_exp, group):
            hots, ys = [], []
            for x in range(g0, g0 + group):
                want = starts[x] + r * win
                s0 = pl.multiple_of(jnp.minimum(want, rows - win), SLOT_ALIGN)
                ids = ids0 + s0
                hots.append(jnp.where((slot[x:x + 1, :] == ids) & (ids >= want), 1.0, 0.0).astype(BF16))
                ys.append(y_ref[0, pl.ds(pl.multiple_of(x * rows + s0, SLOT_ALIGN), win), :])
            acc_scr[...] += _dot_tn(jnp.concatenate(hots, axis=0), jnp.concatenate(ys, axis=0))
        return carry

    lax.fori_loop(0, rounds, one_round, 0)
    out_ref[0] = h_ref[0] + mod_ref[0, 0, GT_F:GT_F + 1, :] * _rms(acc_scr[...], g_ref[...])


def _combine(pos, first, count, y, h, mods, g_post, n_lat, n_ctx):
    bsz, e, rows, d = y.shape
    nt = n_lat + n_ctx
    lat_tiles = n_lat // ROW_TILE
    return pl.pallas_call(
        functools.partial(_combine_body, n_lat=n_lat, rows=rows, lat_cap=CAP_FACTOR * n_lat // N_EXPERTS),
        grid_spec=pltpu.PrefetchScalarGridSpec(
            num_scalar_prefetch=2,
            grid=(bsz, nt // ROW_TILE),
            in_specs=[pl.BlockSpec((1, e, ROW_TILE), lambda b, i, *_: (b, 0, i)),
                      pl.BlockSpec((1, e * rows, d), lambda b, i, *_: (b, 0, 0)),
                      pl.BlockSpec((1, ROW_TILE, d), lambda b, i, *_: (b, i, 0)),
                      pl.BlockSpec((1, 1, N_MOD, d), lambda b, i, *_: (b, i // lat_tiles, 0, 0)),
                      pl.BlockSpec((1, d), lambda b, i, *_: (0, 0))],
            out_specs=pl.BlockSpec((1, ROW_TILE, d), lambda b, i, *_: (b, i, 0)),
            scratch_shapes=[pltpu.VMEM((ROW_TILE, d), F32)]),
        out_shape=jax.ShapeDtypeStruct((bsz, nt, d), F32),
        compiler_params=_cparams("arbitrary", "arbitrary"),
        name="moe_combine",
    )(first, count, pos, y.reshape(bsz, e * rows, d), h, mods, g_post[None, :])


def _moe(h, u, aff, mods, g_post, wg, wu, wd, n_lat, n_ctx):
    pos, first, count = _select(aff, n_lat, n_ctx)
    xs, gate = _gather(pos, aff, u, n_lat, n_ctx)
    y = _experts(xs, gate, wg, wu, wd)
    return _combine(pos, first, count, y, h, mods, g_post, n_lat, n_ctx)


def _conv_body(x_ref, w_ref, o_ref, pad_scr, *, n_lat):
    j = pl.program_id(1)
    nt = x_ref.shape[1]
    halo = CONV_HALO
    zeros = jnp.zeros((halo, LANES), F32)
    for lo, hi in ((0, n_lat), (n_lat, nt)):
        n = hi - lo
        if n == 0:
            continue
        pad_scr[0:halo, :] = zeros
        pad_scr[halo:halo + n, :] = x_ref[0, lo:hi, :].astype(F32)
        pad_scr[halo + n:2 * halo + n, :] = zeros
        acc = jnp.zeros((n, LANES), F32)
        for tap in range(CONV_K):
            start = halo + tap - CONV_K // 2
            acc = acc + pad_scr[start:start + n, :] * w_ref[tap:tap + 1, :]
        y = _silu(acc)
        unit = y * lax.rsqrt(jnp.sum(y * y, axis=-1, keepdims=True) + EPS)
        y = jnp.where(j < GDN_HEADS, unit * GDN_HEAD_DIM ** -0.5, jnp.where(j < 2 * GDN_HEADS, unit, y))
        o_ref[0, lo:hi, :] = y


def _short_conv(p, w_conv, n_lat):
    bsz, nt, _ = p.shape
    nblk = 3 * GDN_HEADS
    return pl.pallas_call(
        functools.partial(_conv_body, n_lat=n_lat),
        grid=(bsz, nblk),
        in_specs=[pl.BlockSpec((1, nt, LANES), lambda b, j: (b, 0, j)),
                  pl.BlockSpec((CONV_K, LANES), lambda b, j: (0, j))],
        out_specs=pl.BlockSpec((1, nt, LANES), lambda b, j: (b, 0, j)),
        out_shape=jax.ShapeDtypeStruct((bsz, nt, nblk * LANES), F32),
        scratch_shapes=[pltpu.VMEM((n_lat + 2 * CONV_HALO, LANES), F32)],
        compiler_params=_cparams("arbitrary", "arbitrary"),
        name="gdn_conv",
    )(p, w_conv)


def _gdn_body(q_ref, k_ref, v_ref, gate_ref, gate_t_ref, alog_ref, dt_ref, alog_t_ref, dt_t_ref,
              o_ref, s_scr, gct_scr, *, rev, n_lat_groups):
    hb = pl.program_id(1)
    step = pl.program_id(2)
    r_t = ROW_TILE
    c_sz = GDN_CHUNK
    goff = 2 * GDN_HEADS if rev else 0

    @pl.when(step == 0)
    def _():
        s_scr[...] = jnp.zeros_like(s_scr)

    ii = lax.broadcasted_iota(jnp.int32, (r_t, r_t), 0)
    jj = lax.broadcasted_iota(jnp.int32, (r_t, r_t), 1)
    same = (ii // c_sz) == (jj // c_sz)
    if rev:
        incl, strict = same & (jj >= ii), same & (jj > ii)
    else:
        incl, strict = same & (jj <= ii), same & (jj < ii)
    m_incl = incl.astype(BF16)
    m_same = same.astype(BF16)
    eye = (ii == jj).astype(F32)

    def pieces(x):
        hi = x.astype(BF16)
        rest = x - hi.astype(F32)
        mid = rest.astype(BF16)
        return hi, mid, (rest - mid.astype(F32)).astype(BF16)

    lane = lax.broadcasted_iota(jnp.int32, (1, LANES), 1)
    is_g = (lane >= goff) & (lane < goff + GDN_HEADS)
    raw = gate_ref[0]
    g_all = jnp.where(is_g, -jnp.exp(alog_ref[...]) * _softplus(raw + dt_ref[...]), 0.0)
    beta_all = _sigmoid(raw)
    g_parts = pieces(g_all)
    gc_all = sum(_dot(m_incl, x) for x in g_parts)
    gl_all = sum(_dot(m_same, x) for x in g_parts)
    g_t = -jnp.exp(alog_t_ref[...]) * _softplus(gate_t_ref[0] + dt_t_ref[...])
    gct_scr[...] = sum(_dot_nt(x, m_incl) for x in pieces(g_t))
    lane_f = lax.broadcasted_iota(jnp.int32, (r_t, LANES), 1)

    def column(a, idx):
        return jnp.sum(jnp.where(lane_f == idx, a, 0.0), axis=1, keepdims=True)

    heads = range(GDN_HEADS_PER_STEP)
    n_chunks = r_t // c_sz
    n_levels = int(math.log2(c_sz)) - 1
    st = []
    for hh in heads:
        head = hb * GDN_HEADS_PER_STEP + hh
        cols = slice(hh * GDN_HEAD_DIM, (hh + 1) * GDN_HEAD_DIM)
        q, k, v = q_ref[0, :, cols], k_ref[0, :, cols], v_ref[0, :, cols]
        gc = column(gc_all, goff + head)
        gl = column(gl_all, goff + head)
        beta = column(beta_all, goff + GDN_HEADS + head)
        gc_row = gct_scr[pl.ds(goff + head, 1), :]
        decay = jnp.where(incl, jnp.exp(jnp.where(incl, gc - gc_row, 0.0)), 0.0)
        kb = k * beta
        k_bf = k.astype(BF16)
        a_mat = jnp.where(strict, _dot_nt(kb.astype(BF16), k_bf) * decay, 0.0)
        st.append(dict(
            cols=cols, gl=gl, inv=eye - a_mat, pw=a_mat,
            rhs=jnp.concatenate([v * beta, kb * jnp.exp(gc)], axis=1).astype(BF16),
            intra=(_dot_nt(q.astype(BF16), k_bf) * decay).astype(BF16),
            q_dec=(q * jnp.exp(gc)).astype(BF16),
            k_dec_t=(k * jnp.exp(gl - gc)).T.astype(BF16),
            s=s_scr[hh], outs=[None] * n_chunks))
    for _ in range(n_levels):
        for t in st:
            pw_bf = t["pw"].astype(BF16)
            t["pw"] = _dot(pw_bf, pw_bf)
        for t in st:
            t["inv"] = t["inv"] + _dot(t["inv"].astype(BF16), t["pw"].astype(BF16))
    for t in st:
        uw = _dot(t["inv"].astype(BF16), t["rhs"])
        t["u"], t["w"] = uw[:, :GDN_HEAD_DIM], uw[:, GDN_HEAD_DIM:].astype(BF16)
    for c in (range(n_chunks - 1, -1, -1) if rev else range(n_chunks)):
        rows = slice(c * c_sz, (c + 1) * c_sz)
        for t in st:
            t["s_bf"] = t["s"].astype(BF16)
            t["v_new"] = (t["u"][rows] - _dot(t["w"][rows], t["s_bf"])).astype(BF16)
        for t in st:
            t["outs"][c] = _dot(t["q_dec"][rows], t["s_bf"]) + _dot(t["intra"][rows, rows], t["v_new"])
            t["s"] = (t["s"] * jnp.exp(t["gl"][c * c_sz:c * c_sz + 1, :])
                      + _dot(t["k_dec_t"][:, rows], t["v_new"]))
    for hh, t in enumerate(st):
        s_scr[hh] = t["s"]
        o_ref[0, :, t["cols"]] = jnp.concatenate(t["outs"], axis=0)


def _gdn_scan(qkv, gates, gates_t, a_log_row, dt_row, a_log_col, dt_col, n_lat, rev):
    bsz, nt, _ = qkv.shape
    hps = GDN_HEADS_PER_STEP
    hblocks = GDN_HEADS // hps
    n_groups = nt // ROW_TILE
    lat_groups = n_lat // ROW_TILE
    width = hps * GDN_HEAD_DIM

    def group(i):
        lat = (lat_groups - i) if rev else (i - 1)
        return jnp.where(i == 0, n_groups - 1, lat)

    return pl.pallas_call(
        functools.partial(_gdn_body, rev=rev, n_lat_groups=lat_groups),
        grid=(bsz, hblocks, n_groups),
        in_specs=[pl.BlockSpec((1, ROW_TILE, width), lambda b, h, i: (b, group(i), h)),
                  pl.BlockSpec((1, ROW_TILE, width), lambda b, h, i: (b, group(i), hblocks + h)),
                  pl.BlockSpec((1, ROW_TILE, width), lambda b, h, i: (b, group(i), 2 * hblocks + h)),
                  pl.BlockSpec((1, ROW_TILE, LANES), lambda b, h, i: (b, group(i), 0)),
                  pl.BlockSpec((1, 4 * GDN_HEADS, ROW_TILE), lambda b, h, i: (b, 0, group(i))),
                  pl.BlockSpec((1, LANES), lambda b, h, i: (0, 0)),
                  pl.BlockSpec((1, LANES), lambda b, h, i: (0, 0)),
                  pl.BlockSpec((4 * GDN_HEADS, 1), lambda b, h, i: (0, 0)),
                  pl.BlockSpec((4 * GDN_HEADS, 1), lambda b, h, i: (0, 0))],
        out_specs=pl.BlockSpec((1, ROW_TILE, width), lambda b, h, i: (b, group(i), h)),
        out_shape=jax.ShapeDtypeStruct((bsz, nt, GDN_HEADS * GDN_HEAD_DIM), F32),
        scratch_shapes=[pltpu.VMEM((hps, GDN_HEAD_DIM, GDN_HEAD_DIM), F32),
                        pltpu.VMEM((4 * GDN_HEADS, ROW_TILE), F32)],
        compiler_params=_cparams("arbitrary", "arbitrary", "arbitrary"),
        name="gdn_scan_bwd" if rev else "gdn_scan_fwd",
    )(qkv, qkv, qkv, gates, gates_t, a_log_row, dt_row, a_log_col, dt_col)


def _gdn_out_body(of_ref, ob_ref, z_ref, gn_ref, w_ref, h_ref, mod_ref, g_ref, g_ffn_ref, wr_ref,
                  out_ref, u_ref, aff_ref, a_scr):
    for hh in range(GDN_HEADS):
        cols = slice(hh * GDN_HEAD_DIM, (hh + 1) * GDN_HEAD_DIM)
        o = of_ref[0, :, cols] + ob_ref[0, :, cols]
        a_scr[:, cols] = (_rms(o, gn_ref[...]) * _silu(z_ref[0, :, cols].astype(F32))).astype(BF16)
    y = _dot(a_scr[...], w_ref[...])
    _residual_and_route(h_ref[0], y, mod_ref[0, 0], g_ref, g_ffn_ref, wr_ref, out_ref, u_ref, aff_ref)


def _gdn_out(o_f, o_b, p, g_onorm, w_bf, h, mods, g_post, g_ffn, w_router, n_lat):
    bsz, _, d = h.shape
    width = GDN_HEADS * GDN_HEAD_DIM
    r_in, r_out, r_shape = _route_specs(bsz, n_lat, d)
    return pl.pallas_call(
        _gdn_out_body,
        grid=(bsz, n_lat // ROW_TILE),
        in_specs=[pl.BlockSpec((1, ROW_TILE, width), lambda b, i: (b, i, 0)),
                  pl.BlockSpec((1, ROW_TILE, width), lambda b, i: (b, i, 0)),
                  pl.BlockSpec((1, ROW_TILE, width), lambda b, i: (b, i, 3)),
                  pl.BlockSpec((1, GDN_HEAD_DIM), lambda b, i: (0, 0)),
                  pl.BlockSpec((width, d), lambda b, i: (0, 0)),
                  pl.BlockSpec((1, ROW_TILE, d), lambda b, i: (b, i, 0)),
                  pl.BlockSpec((1, 1, N_MOD, d), lambda b, i: (b, 0, 0, 0)),
                  pl.BlockSpec((1, d), lambda b, i: (0, 0))] + r_in,
        out_specs=r_out,
        out_shape=r_shape,
        scratch_shapes=[pltpu.VMEM((ROW_TILE, width), BF16)],
        compiler_params=_cparams("arbitrary", "arbitrary"),
        name="gdn_out",
    )(o_f, o_b, p, g_onorm[None, :], w_bf, h, mods, g_post[None, :], g_ffn[None, :], _pad_lanes(w_router))


def _gate_vectors(a_log_f, dt_bias_f, a_log_b, dt_bias_b):
    z = jnp.zeros((GDN_HEADS,), F32)
    a = jnp.concatenate([a_log_f, z, a_log_b, z])
    d = jnp.concatenate([dt_bias_f, z, dt_bias_b, z])
    pad = jnp.zeros((LANES - 4 * GDN_HEADS,), F32)
    return (jnp.concatenate([a, pad])[None, :], jnp.concatenate([d, pad])[None, :], a[:, None], d[:, None])


def kernel(x, c, ctx, c_ctx, l0_w_ada, l0_b_ada, l0_g_pre_mix, l0_g_post_mix, l0_g_pre_ffn, l0_g_post_ffn, l0_w_qkv, l0_lambda_q1, l0_lambda_k1, l0_lambda_q2, l0_lambda_k2, l0_g_subln, l0_w_o, l0_w_router, l0_w_gate, l0_w_up, l0_w_down, l1_w_ada, l1_b_ada, l1_g_pre_mix, l1_g_post_mix, l1_g_pre_ffn, l1_g_post_ffn, l1_w_in, l1_w_conv, l1_a_log_f, l1_dt_bias_f, l1_a_log_b, l1_dt_bias_b, l1_g_onorm, l1_w_o, l1_w_router, l1_w_gate, l1_w_up, l1_w_down):
    n_lat, n_ctx = x.shape[1], ctx.shape[1]
    h = jnp.concatenate([x, ctx], axis=1)

    mods = _ada_mods(c, c_ctx, l0_w_ada, l0_b_ada)
    qkv = _project(h, mods, l0_g_pre_mix, l0_w_qkv.astype(BF16), n_lat, rope_tables=_rope_tables(n_lat, n_ctx))
    lam_vecs = jnp.stack([l0_lambda_q1, l0_lambda_k1, l0_lambda_q2, l0_lambda_k2])
    o = _diff_attention(qkv, lam_vecs, l0_g_subln, n_lat, depth=0)
    h, u, aff = _out_project(o, l0_w_o.astype(BF16), h, mods, l0_g_post_mix, l0_g_pre_ffn, l0_w_router, n_lat)
    h = _moe(h, u, aff, mods, l0_g_post_ffn, l0_w_gate, l0_w_up, l0_w_down, n_lat, n_ctx)

    mods = _ada_mods(c, c_ctx, l1_w_ada, l1_b_ada)
    width = GDN_HEADS * GDN_HEAD_DIM
    w_gates = jnp.pad(l1_w_in[:, 4 * width:], ((0, 0), (0, LANES - 4 * GDN_HEADS))).astype(BF16)
    p, gates = _project(h, mods, l1_g_pre_mix, l1_w_in[:, :4 * width].astype(BF16), n_lat, w_gates=w_gates)
    gates_t = jnp.swapaxes(gates[:, :, :4 * GDN_HEADS], 1, 2)
    qkv = _short_conv(p, l1_w_conv, n_lat)
    gv = _gate_vectors(l1_a_log_f, l1_dt_bias_f, l1_a_log_b, l1_dt_bias_b)
    o_f = _gdn_scan(qkv, gates, gates_t, *gv, n_lat, rev=False)
    o_b = _gdn_scan(qkv, gates, gates_t, *gv, n_lat, rev=True)
    hl, u, aff = _gdn_out(o_f, o_b, p, l1_g_onorm, l1_w_o.astype(BF16), h, mods, l1_g_post_mix, l1_g_pre_ffn,
                          l1_w_router, n_lat)
    return _moe(hl, u, aff, mods, l1_g_post_ffn, l1_w_gate, l1_w_up, l1_w_down, n_lat, 0)
```

```python
import functools
import math

import jax
import jax.numpy as jnp
from jax import lax
from jax.experimental import pallas as pl
from jax.experimental.pallas import tpu as pltpu

F32 = jnp.float32
BF16 = jnp.bfloat16
HIGHEST = lax.Precision.HIGHEST

EPS = 1e-6
N_MOD = 6
GRID_W = 64
ROPE_BASE = 10000.0
DA_HEADS = 8
DA_HEAD_DIM = 64
GDN_HEADS = 8
GDN_HEAD_DIM = 128
CONV_K = 5
CONV_HALO = 8
N_EXPERTS = 16
CAP_FACTOR = 2

LANES = 128
ROW_TILE = 256
PROJ_ROWS = 768
ATTN_Q_ROWS = 512
ATTN_CHAIN_ROWS = 128
Q_PRESCALE = DA_HEAD_DIM ** -0.5 * math.log2(math.e)
GDN_CHUNK = 64
GDN_HEADS_PER_STEP = 8
FF_CHUNK = 512
SLOT_WINDOW = 64
SLOT_ALIGN = 16
TABLE_COUNT = 64
VMEM_LIMIT = 56 * 1024 * 1024

SH_M, SC_M, GT_M, SH_F, SC_F, GT_F = range(6)


def _cparams(*sem):
    return pltpu.CompilerParams(dimension_semantics=sem, vmem_limit_bytes=VMEM_LIMIT)


def _dot(a, b):
    return jnp.dot(a, b, preferred_element_type=F32)


def _dot_nt(a, b, precision=None):
    return lax.dot_general(a, b, (((1,), (1,)), ((), ())), preferred_element_type=F32, precision=precision)


def _dot_tn(a, b):
    return lax.dot_general(a, b, (((0,), (0,)), ((), ())), preferred_element_type=F32)


def _rms(x, g):
    return x * lax.rsqrt(jnp.mean(x * x, axis=-1, keepdims=True) + EPS) * g


def _sigmoid(x):
    return 1.0 / (1.0 + jnp.exp(-x))


def _silu(x):
    return x * _sigmoid(x)


def _softplus(x):
    return jnp.maximum(x, 0.0) + jnp.log(1.0 + jnp.exp(-jnp.abs(x)))


def _ada_body(c_ref, w_ref, b_ref, o_ref):
    o_ref[...] = jnp.dot(_silu(c_ref[...]), w_ref[...], precision=HIGHEST, preferred_element_type=F32) + b_ref[...]


def _ada_mods(c, c_ctx, w_ada, b_ada):
    bsz, d = c.shape
    rows = 16
    cc = jnp.concatenate([c, c_ctx[None, :], jnp.zeros((rows - bsz - 1, d), F32)], axis=0)
    tn = 1024
    m = pl.pallas_call(
        _ada_body,
        grid=(N_MOD * d // tn,),
        in_specs=[pl.BlockSpec((rows, d), lambda j: (0, 0)),
                  pl.BlockSpec((d, tn), lambda j: (0, j)),
                  pl.BlockSpec((1, tn), lambda j: (0, j))],
        out_specs=pl.BlockSpec((rows, tn), lambda j: (0, j)),
        out_shape=jax.ShapeDtypeStruct((rows, N_MOD * d), F32),
        compiler_params=_cparams("arbitrary"),
        name="ada_mods",
    )(cc, w_ada, b_ada[None, :])
    lat = m[:bsz].reshape(bsz, 1, N_MOD, d)
    ctx = jnp.broadcast_to(m[bsz].reshape(1, 1, N_MOD, d), (bsz, 1, N_MOD, d))
    return jnp.concatenate([lat, ctx], axis=1)


def _proj_body(h_ref, mod_ref, g_ref, w_ref, *rest, n_lat, rope, gates):
    rest = list(rest)
    cos_ref, sin_ref = (rest.pop(0), rest.pop(0)) if rope else (None, None)
    wg_ref = rest.pop(0) if gates else None
    o_ref = rest.pop(0)
    og_ref = rest.pop(0) if gates else None
    x = h_ref[0]
    tm, d = x.shape
    row = pl.program_id(1) * tm + lax.broadcasted_iota(jnp.int32, (tm, 1), 0)
    is_ctx = row >= n_lat
    scale = jnp.where(is_ctx, mod_ref[0, 1, SC_M:SC_M + 1, :], mod_ref[0, 0, SC_M:SC_M + 1, :])
    shift = jnp.where(is_ctx, mod_ref[0, 1, SH_M:SH_M + 1, :], mod_ref[0, 0, SH_M:SH_M + 1, :])
    u = (_rms(x, g_ref[...]) * (1.0 + scale) + shift).astype(BF16)
    if gates:
        og_ref[0] = _dot(u, wg_ref[...])
    for j in range(w_ref.shape[1] // d):
        cols = slice(j * d, (j + 1) * d)
        acc = _dot(u, w_ref[:, cols])
        if rope and j < 2:
            reps = d // LANES
            cos = jnp.tile(cos_ref[...], (1, reps))
            sin = jnp.tile(sin_ref[...], (1, reps))
            lane = lax.broadcasted_iota(jnp.int32, acc.shape, 1)
            half = DA_HEAD_DIM // 4
            first = (lane % (2 * half)) < half
            partner = jnp.where(first, pltpu.roll(acc, d - half, 1), pltpu.roll(acc, half, 1))
            acc = acc * cos + partner * sin
            if j == 0:
                acc = acc * Q_PRESCALE
        o_ref[0, :, cols] = acc.astype(o_ref.dtype)


def _project(h, mods, g, w_bf, n_lat, rope_tables=None, w_gates=None):
    bsz, nt, d = h.shape
    n_out = w_bf.shape[1]
    tm = PROJ_ROWS
    rope = rope_tables is not None
    gates = w_gates is not None
    in_specs = [pl.BlockSpec((1, tm, d), lambda b, i: (b, i, 0)),
                pl.BlockSpec((1, 2, N_MOD, d), lambda b, i: (b, 0, 0, 0)),
                pl.BlockSpec((1, d), lambda b, i: (0, 0)),
                pl.BlockSpec((d, n_out), lambda b, i: (0, 0))]
    args = [h, mods, g[None, :], w_bf]
    out_specs = [pl.BlockSpec((1, tm, n_out), lambda b, i: (b, i, 0))]
    out_shape = [jax.ShapeDtypeStruct((bsz, nt, n_out), BF16)]
    if rope:
        in_specs += [pl.BlockSpec((tm, LANES), lambda b, i: (i, 0))] * 2
        args += list(rope_tables)
    if gates:
        in_specs.append(pl.BlockSpec((d, LANES), lambda b, i: (0, 0)))
        args.append(w_gates)
        out_specs.append(pl.BlockSpec((1, tm, LANES), lambda b, i: (b, i, 0)))
        out_shape.append(jax.ShapeDtypeStruct((bsz, nt, LANES), F32))
    out = pl.pallas_call(
        functools.partial(_proj_body, n_lat=n_lat, rope=rope, gates=gates),
        grid=(bsz, nt // tm),
        in_specs=in_specs,
        out_specs=out_specs,
        out_shape=out_shape,
        compiler_params=_cparams("arbitrary", "arbitrary"),
        name="mod_project",
    )(*args)
    return out if gates else out[0]


def _rope_tables(n_lat, n_ctx):
    rows = n_lat // GRID_W
    r = jnp.repeat(jnp.arange(rows), GRID_W).astype(F32)
    col = jnp.tile(jnp.arange(GRID_W), rows).astype(F32)
    half = DA_HEAD_DIM // 2
    inv = ROPE_BASE ** (-jnp.arange(0, half, 2, dtype=F32) / half)
    ang_r, ang_c = r[:, None] * inv, col[:, None] * inv
    cos = jnp.concatenate([jnp.cos(ang_r)] * 2 + [jnp.cos(ang_c)] * 2, axis=-1)
    sin = jnp.concatenate([-jnp.sin(ang_r), jnp.sin(ang_r), -jnp.sin(ang_c), jnp.sin(ang_c)], axis=-1)
    cos = jnp.concatenate([jnp.tile(cos, (1, 2)), jnp.ones((n_ctx, LANES), F32)], axis=0)
    sin = jnp.concatenate([jnp.tile(sin, (1, 2)), jnp.zeros((n_ctx, LANES), F32)], axis=0)
    return cos, sin


def _attn_body(lam_ref, q_ref, k_ref, v_ref, gs_ref, o_ref, *, lam_init):
    lv = lam_ref[...]
    lam = (jnp.exp(jnp.sum(lv[0:1] * lv[1:2], axis=-1, keepdims=True))
           - jnp.exp(jnp.sum(lv[2:3] * lv[3:4], axis=-1, keepdims=True)) + lam_init)
    q, k, v = q_ref[0], k_ref[0], v_ref[0]
    lane = lax.broadcasted_iota(jnp.int32, q.shape, 1)
    hw = v.shape[1]
    v_one = jnp.concatenate([v, jnp.ones_like(v)], axis=1)

    rows = min(q.shape[0], ATTN_CHAIN_ROWS)
    starts = range(0, q.shape[0], rows)
    qm = [jnp.where((lane >= c * DA_HEAD_DIM) & (lane < (c + 1) * DA_HEAD_DIM), q, jnp.zeros_like(q)) for c in range(2)]
    s = [_dot_nt(jnp.concatenate([qm[0][r:r + rows], qm[1][r:r + rows]], axis=0), k) for r in starts]
    e = [jnp.exp2((x - jnp.max(x, axis=-1, keepdims=True)).astype(BF16)) for x in s]
    ov = [_dot(x, v_one) for x in e]
    att = [x[:, :hw] / x[:, hw:hw + 1] for x in ov]
    for r, a in zip(starts, att):
        o = a[:rows] - lam * a[rows:]
        o_ref[0, r:r + rows, :] = (_rms(o, gs_ref[...]) * (1.0 - lam_init)).astype(o_ref.dtype)


def _diff_attention(qkv, lam_vecs, g_subln, n_lat, depth):
    bsz, nt, d3 = qkv.shape
    d = d3 // 3
    hw = 2 * DA_HEAD_DIM
    lam_init = 0.8 - 0.6 * math.exp(-0.3 * depth)
    n_ctx = nt - n_lat

    def call(n_q, tq, q_blk0, n_k, k_blk0):
        return pl.pallas_call(
            functools.partial(_attn_body, lam_init=lam_init),
            grid=(bsz, DA_HEADS, n_q // tq),
            in_specs=[pl.BlockSpec((4, DA_HEAD_DIM), lambda b, h, i: (0, 0)),
                      pl.BlockSpec((1, tq, hw), lambda b, h, i: (b, q_blk0 + i, h)),
                      pl.BlockSpec((1, n_k, hw), lambda b, h, i: (b, k_blk0, DA_HEADS + h)),
                      pl.BlockSpec((1, n_k, hw), lambda b, h, i: (b, k_blk0, 2 * DA_HEADS + h)),
                      pl.BlockSpec((1, hw), lambda b, h, i: (0, 0))],
            out_specs=pl.BlockSpec((1, tq, hw), lambda b, h, i: (b, i, h)),
            out_shape=jax.ShapeDtypeStruct((bsz, n_q, d), BF16),
            compiler_params=_cparams("arbitrary", "arbitrary", "arbitrary"),
            name="diff_attention",
        )(lam_vecs, qkv, qkv, qkv, g_subln[None, :])

    o_lat = call(n_lat, ATTN_Q_ROWS, 0, nt, 0)
    o_ctx = call(n_ctx, n_ctx, n_lat // n_ctx, n_ctx, n_lat // n_ctx)
    return jnp.concatenate([o_lat, o_ctx], axis=1)


def _residual_and_route(h, y, mod, g_post_ref, g_ffn_ref, wr_ref, out_ref, u_ref, aff_ref):
    h_new = h + mod[GT_M:GT_M + 1, :] * _rms(y, g_post_ref[...])
    out_ref[0] = h_new
    u = _rms(h_new, g_ffn_ref[...]) * (1.0 + mod[SC_F:SC_F + 1, :]) + mod[SH_F:SH_F + 1, :]
    u_hi = u.astype(BF16)
    u_ref[0] = u_hi
    u_lo = (u - u_hi.astype(F32)).astype(BF16)
    w = wr_ref[...]
    w_hi = w.astype(BF16)
    w_lo = (w - w_hi.astype(F32)).astype(BF16)
    logits = (_dot(u_hi, w_hi) + _dot(u_hi, w_lo) + _dot(u_lo, w_hi)).T[:N_EXPERTS]
    e = jnp.exp(logits - jnp.max(logits, axis=0, keepdims=True))
    aff_ref[0] = e / jnp.sum(e, axis=0, keepdims=True)


def _pad_lanes(w):
    return jnp.pad(w, ((0, 0), (0, LANES - w.shape[1])))


def _route_specs(bsz, nt, d):
    in_specs = [pl.BlockSpec((1, d), lambda b, i: (0, 0)),
                pl.BlockSpec((d, LANES), lambda b, i: (0, 0))]
    out_specs = [pl.BlockSpec((1, ROW_TILE, d), lambda b, i: (b, i, 0)),
                 pl.BlockSpec((1, ROW_TILE, d), lambda b, i: (b, i, 0)),
                 pl.BlockSpec((1, N_EXPERTS, ROW_TILE), lambda b, i: (b, 0, i))]
    out_shape = [jax.ShapeDtypeStruct((bsz, nt, d), F32),
                 jax.ShapeDtypeStruct((bsz, nt, d), BF16),
                 jax.ShapeDtypeStruct((bsz, N_EXPERTS, nt), F32)]
    return in_specs, out_specs, out_shape


def _oproj_body(o_ref, w_ref, h_ref, mod_ref, g_ref, g_ffn_ref, wr_ref, out_ref, u_ref, aff_ref):
    y = _dot(o_ref[0], w_ref[...])
    _residual_and_route(h_ref[0], y, mod_ref[0, 0], g_ref, g_ffn_ref, wr_ref, out_ref, u_ref, aff_ref)


def _out_project(o, w_bf, h, mods, g_post, g_ffn, w_router, n_lat):
    bsz, nt, d = h.shape
    lat_tiles = n_lat // ROW_TILE
    r_in, r_out, r_shape = _route_specs(bsz, nt, d)
    return pl.pallas_call(
        _oproj_body,
        grid=(bsz, nt // ROW_TILE),
        in_specs=[pl.BlockSpec((1, ROW_TILE, d), lambda b, i: (b, i, 0)),
                  pl.BlockSpec((d, d), lambda b, i: (0, 0)),
                  pl.BlockSpec((1, ROW_TILE, d), lambda b, i: (b, i, 0)),
                  pl.BlockSpec((1, 1, N_MOD, d), lambda b, i: (b, i // lat_tiles, 0, 0)),
                  pl.BlockSpec((1, d), lambda b, i: (0, 0))] + r_in,
        out_specs=r_out,
        out_shape=r_shape,
        compiler_params=_cparams("arbitrary", "arbitrary"),
        name="out_project",
    )(o, w_bf, h, mods, g_post[None, :], g_ffn[None, :], _pad_lanes(w_router))


def _lane_cumsum(x):
    n = x.shape[1]
    jj = lax.broadcasted_iota(jnp.int32, (ROW_TILE, ROW_TILE), 0)
    nn = lax.broadcasted_iota(jnp.int32, (ROW_TILE, ROW_TILE), 1)
    tri = (jj <= nn).astype(BF16)
    run = jnp.zeros((x.shape[0], 1), F32)
    parts = []
    for t in range(n // ROW_TILE):
        local = _dot(x[:, t * ROW_TILE:(t + 1) * ROW_TILE].astype(BF16), tri) + run
        parts.append(local)
        run = local[:, ROW_TILE - 1:ROW_TILE]
    return jnp.concatenate(parts, axis=1) if len(parts) > 1 else parts[0]


def _select_slots(aff, cap):
    bits = pltpu.bitcast(aff, jnp.int32)

    def step(t, lo):
        cand = lo | jnp.left_shift(jnp.int32(1), 30 - t)
        cnt = jnp.sum((bits >= cand).astype(F32), axis=1, keepdims=True)
        return jnp.where(cnt >= cap, cand, lo)

    thr = lax.fori_loop(0, 31, step, jnp.zeros((aff.shape[0], 1), jnp.int32))
    gt = bits > thr
    eq = bits == thr
    need = cap - jnp.sum(gt.astype(F32), axis=1, keepdims=True)
    eq_rank = _lane_cumsum(eq.astype(F32))
    sel = gt | (eq & (eq_rank <= need))
    sel_f = sel.astype(F32)
    slot = _lane_cumsum(sel_f) - 1.0
    return jnp.where(sel, slot, -1.0).astype(jnp.int32), sel_f


def _tile_table(sel_f, tile0):
    lane = lax.broadcasted_iota(jnp.int32, (sel_f.shape[0], LANES), 1)
    run = jnp.zeros((sel_f.shape[0], 1), F32)
    tab = jnp.zeros((sel_f.shape[0], LANES), F32)
    for t in range(sel_f.shape[1] // ROW_TILE):
        inside = jnp.sum(sel_f[:, t * ROW_TILE:(t + 1) * ROW_TILE], axis=1, keepdims=True)
        tab = jnp.where(lane == tile0 + t, run, tab)
        tab = jnp.where(lane == TABLE_COUNT + tile0 + t, inside, tab)
        run = run + inside
    return tab


def _select_body(aff_ref, pos_ref, tab_ref, *, n_lat, n_ctx):
    aff = aff_ref[0]
    pos, sel_f = _select_slots(aff[:, :n_lat], CAP_FACTOR * n_lat // N_EXPERTS)
    pos_ref[0, :, :n_lat] = pos
    tab = _tile_table(sel_f, 0)
    if n_ctx:
        pos, sel_f = _select_slots(aff[:, n_lat:], CAP_FACTOR * n_ctx // N_EXPERTS)
        pos_ref[0, :, n_lat:] = pos
        tab = tab + _tile_table(sel_f, n_lat // ROW_TILE)
    tab_ref[0] = tab.astype(jnp.int32)


def _select(aff, n_lat, n_ctx):
    bsz, e, nt = aff.shape
    n_tiles = nt // ROW_TILE
    pos, tab = pl.pallas_call(
        functools.partial(_select_body, n_lat=n_lat, n_ctx=n_ctx),
        grid=(bsz,),
        in_specs=[pl.BlockSpec((1, e, nt), lambda b: (b, 0, 0))],
        out_specs=[pl.BlockSpec((1, e, nt), lambda b: (b, 0, 0)),
                   pl.BlockSpec((1, e, LANES), lambda b: (b, 0, 0))],
        out_shape=[jax.ShapeDtypeStruct((bsz, e, nt), jnp.int32),
                   jax.ShapeDtypeStruct((bsz, e, LANES), jnp.int32)],
        compiler_params=_cparams("arbitrary"),
        name="moe_select",
    )(aff)
    first = tab[:, :, :n_tiles].reshape(-1)
    count = tab[:, :, TABLE_COUNT:TABLE_COUNT + n_tiles].reshape(-1)
    return pos, first, count


class _TileWindows:
    def __init__(self, first_ref, count_ref, pos_ref, n_lat, rows, lat_cap):
        b, tile = pl.program_id(0), pl.program_id(1)
        n_exp = pos_ref.shape[1]
        self.rows, self.n_exp = rows, n_exp
        base = jnp.where(tile * ROW_TILE >= n_lat, lat_cap, 0)
        pos = pos_ref[0]
        self.slot = jnp.where(pos >= 0, pos + base, -1)
        self.starts, self.rounds = [], 0
        for x in range(n_exp):
            at = (b * n_exp + x) * pl.num_programs(1) + tile
            first = first_ref[at]
            start = base + (first // SLOT_ALIGN) * SLOT_ALIGN
            self.starts.append(start)
            self.rounds = jnp.maximum(self.rounds, (base + first + count_ref[at] - start + SLOT_WINDOW - 1) // SLOT_WINDOW)
        self.ids0 = lax.broadcasted_iota(jnp.int32, (SLOT_WINDOW, ROW_TILE), 0)

    def window(self, x, r):
        want = self.starts[x] + r * SLOT_WINDOW
        s0 = jnp.minimum(want, self.rows - SLOT_WINDOW)
        ids = self.ids0 + s0
        hot = (self.slot[x:x + 1, :] == ids) & (ids >= want)
        return pl.multiple_of(x * self.rows + s0, SLOT_ALIGN), hot


def _as_bf16(mask):
    return jnp.where(mask, 1.0, 0.0).astype(BF16)


def _gather_body(first_ref, count_ref, pos_ref, aff_ref, u_ref, xs_ref, gate_ref, *, n_lat, rows, lat_cap):
    @pl.when(pl.program_id(1) == 0)
    def _():
        xs_ref[...] = jnp.zeros_like(xs_ref)
        gate_ref[...] = jnp.zeros_like(gate_ref)

    tw = _TileWindows(first_ref, count_ref, pos_ref, n_lat, rows, lat_cap)
    aff = aff_ref[0]
    u = u_ref[0]

    def one_round(r, carry):
        wins = [tw.window(x, r) for x in range(tw.n_exp)]
        part = _dot(jnp.concatenate([_as_bf16(hot) for _, hot in wins], axis=0), u)
        for x, (off, hot) in enumerate(wins):
            dst = pl.ds(off, SLOT_WINDOW)
            xs_ref[0, dst, :] = xs_ref[0, dst, :] + part[x * SLOT_WINDOW:(x + 1) * SLOT_WINDOW].astype(BF16)
            gate = jnp.sum(jnp.where(hot, aff[x:x + 1, :], 0.0), axis=1, keepdims=True)
            gate_ref[0, dst, :] = gate_ref[0, dst, :] + jnp.broadcast_to(gate, (SLOT_WINDOW, LANES))
        return carry

    lax.fori_loop(0, tw.rounds, one_round, 0)


def _slot_rows(n_lat, n_ctx):
    return CAP_FACTOR * n_lat // N_EXPERTS + CAP_FACTOR * n_ctx // N_EXPERTS


def _gather(pos, first, count, aff, u, n_lat, n_ctx):
    bsz, nt, d = u.shape
    e = pos.shape[1]
    rows = _slot_rows(n_lat, n_ctx)
    xs, gate = pl.pallas_call(
        functools.partial(_gather_body, n_lat=n_lat, rows=rows, lat_cap=CAP_FACTOR * n_lat // N_EXPERTS),
        grid_spec=pltpu.PrefetchScalarGridSpec(
            num_scalar_prefetch=2,
            grid=(bsz, nt // ROW_TILE),
            in_specs=[pl.BlockSpec((1, e, ROW_TILE), lambda b, i, *_: (b, 0, i)),
                      pl.BlockSpec((1, e, ROW_TILE), lambda b, i, *_: (b, 0, i)),
                      pl.BlockSpec((1, ROW_TILE, d), lambda b, i, *_: (b, i, 0))],
            out_specs=[pl.BlockSpec((1, e * rows, d), lambda b, i, *_: (b, 0, 0)),
                       pl.BlockSpec((1, e * rows, LANES), lambda b, i, *_: (b, 0, 0))]),
        out_shape=[jax.ShapeDtypeStruct((bsz, e * rows, d), BF16),
                   jax.ShapeDtypeStruct((bsz, e * rows, LANES), F32)],
        compiler_params=_cparams("arbitrary", "arbitrary"),
        name="moe_gather",
    )(first, count, pos, aff, u)
    return xs.reshape(bsz, e, rows, d), gate.reshape(bsz, e, rows, LANES)


def _expert_body(xs_ref, gate_ref, wg_hbm, wu_hbm, wd_hbm, y_ref, wg_scr, wu_scr, wd_scr,
                 wg_stage, wu_stage, wd_stage, sems):
    x, b = pl.program_id(0), pl.program_id(1)
    n_exp, n_chunks = pl.num_programs(0), pl.num_programs(1)
    rows_in, rows_mid = wg_stage.shape[0], wd_stage.shape[0]

    def chunk_copies(expert, chunk):
        return (pltpu.make_async_copy(wg_hbm.at[expert, pl.ds(chunk * rows_in, rows_in), :], wg_stage, sems.at[0]),
                pltpu.make_async_copy(wu_hbm.at[expert, pl.ds(chunk * rows_in, rows_in), :], wu_stage, sems.at[1]),
                pltpu.make_async_copy(wd_hbm.at[expert, pl.ds(chunk * rows_mid, rows_mid), :], wd_stage, sems.at[2]))

    def land(copy, chunk):
        at_in = pl.ds(pl.multiple_of(chunk * rows_in, SLOT_ALIGN), rows_in)
        at_mid = pl.ds(pl.multiple_of(chunk * rows_mid, SLOT_ALIGN), rows_mid)
        wg_scr[copy, at_in, :] = wg_stage[...].astype(BF16)
        wu_scr[copy, at_in, :] = wu_stage[...].astype(BF16)
        wd_scr[copy, at_mid, :] = wd_stage[...].astype(BF16)

    @pl.when((x == 0) & (b == 0))
    def _():
        def fetch(chunk, carry):
            copies = chunk_copies(0, chunk)
            for cp in copies:
                cp.start()
            for cp in copies:
                cp.wait()
            land(0, chunk)
            return carry

        lax.fori_loop(0, n_chunks, fetch, 0)

    @pl.when(x + 1 < n_exp)
    def _():
        for cp in chunk_copies(x + 1, b):
            cp.start()

    cur = x % 2
    xs = xs_ref[0, 0]
    acc = jnp.zeros((xs.shape[0], wd_scr.shape[2]), F32)
    for s in range(wg_scr.shape[2] // FF_CHUNK):
        cols = slice(s * FF_CHUNK, (s + 1) * FF_CHUNK)
        hid = _silu(_dot(xs, wg_scr[cur, :, cols])) * _dot(xs, wu_scr[cur, :, cols])
        acc = acc + _dot(hid.astype(BF16), wd_scr[cur, cols, :])
    y_ref[0, 0] = (acc * gate_ref[0, 0, :, 0:1]).astype(y_ref.dtype)

    @pl.when(x + 1 < n_exp)
    def _():
        for cp in chunk_copies(x + 1, b):
            cp.wait()
        land(1 - cur, b)


def _experts(xs, gate, wg, wu, wd):
    bsz, e, rows, d = xs.shape
    ff = wg.shape[2]
    return pl.pallas_call(
        _expert_body,
        grid=(e, bsz),
        in_specs=[pl.BlockSpec((1, 1, rows, d), lambda x, b: (b, x, 0, 0)),
                  pl.BlockSpec((1, 1, rows, LANES), lambda x, b: (b, x, 0, 0)),
                  pl.BlockSpec(memory_space=pl.ANY),
                  pl.BlockSpec(memory_space=pl.ANY),
                  pl.BlockSpec(memory_space=pl.ANY)],
        out_specs=pl.BlockSpec((1, 1, rows, d), lambda x, b: (b, x, 0, 0)),
        out_shape=jax.ShapeDtypeStruct((bsz, e, rows, d), BF16),
        scratch_shapes=[pltpu.VMEM((2, d, ff), BF16), pltpu.VMEM((2, d, ff), BF16), pltpu.VMEM((2, ff, d), BF16),
                        pltpu.VMEM((d // bsz, ff), F32), pltpu.VMEM((d // bsz, ff), F32),
                        pltpu.VMEM((ff // bsz, d), F32), pltpu.SemaphoreType.DMA((3,))],
        compiler_params=_cparams("arbitrary", "arbitrary"),
        name="moe_experts",
    )(xs, gate, wg, wu, wd)


def _combine_body(first_ref, count_ref, pos_ref, y_ref, h_ref, mod_ref, g_ref, out_ref, acc_scr,
                  *, n_lat, rows, lat_cap):
    tw = _TileWindows(first_ref, count_ref, pos_ref, n_lat, rows, lat_cap)
    acc_scr[...] = jnp.zeros_like(acc_scr)

    def one_round(r, carry):
        group = ROW_TILE // SLOT_WINDOW
        for g0 in range(0, tw.n_exp, group):
            wins = [tw.window(x, r) for x in range(g0, g0 + group)]
            hot = jnp.concatenate([_as_bf16(h) for _, h in wins], axis=0)
            rows_y = jnp.concatenate([y_ref[0, pl.ds(off, SLOT_WINDOW), :] for off, _ in wins], axis=0)
            acc_scr[...] += _dot_tn(hot, rows_y)
        return carry

    lax.fori_loop(0, tw.rounds, one_round, 0)
    out_ref[0] = h_ref[0] + mod_ref[0, 0, GT_F:GT_F + 1, :] * _rms(acc_scr[...], g_ref[...])


def _combine(pos, first, count, y, h, mods, g_post, n_lat, n_ctx):
    bsz, e, rows, d = y.shape
    nt = n_lat + n_ctx
    lat_tiles = n_lat // ROW_TILE
    return pl.pallas_call(
        functools.partial(_combine_body, n_lat=n_lat, rows=rows, lat_cap=CAP_FACTOR * n_lat // N_EXPERTS),
        grid_spec=pltpu.PrefetchScalarGridSpec(
            num_scalar_prefetch=2,
            grid=(bsz, nt // ROW_TILE),
            in_specs=[pl.BlockSpec((1, e, ROW_TILE), lambda b, i, *_: (b, 0, i)),
                      pl.BlockSpec((1, e * rows, d), lambda b, i, *_: (b, 0, 0)),
                      pl.BlockSpec((1, ROW_TILE, d), lambda b, i, *_: (b, i, 0)),
                      pl.BlockSpec((1, 1, N_MOD, d), lambda b, i, *_: (b, i // lat_tiles, 0, 0)),
                      pl.BlockSpec((1, d), lambda b, i, *_: (0, 0))],
            out_specs=pl.BlockSpec((1, ROW_TILE, d), lambda b, i, *_: (b, i, 0)),
            scratch_shapes=[pltpu.VMEM((ROW_TILE, d), F32)]),
        out_shape=jax.ShapeDtypeStruct((bsz, nt, d), F32),
        compiler_params=_cparams("arbitrary", "arbitrary"),
        name="moe_combine",
    )(first, count, pos, y.reshape(bsz, e * rows, d), h, mods, g_post[None, :])


def _moe(h, u, aff, mods, g_post, wg, wu, wd, n_lat, n_ctx):
    pos, first, count = _select(aff, n_lat, n_ctx)
    xs, gate = _gather(pos, first, count, aff, u, n_lat, n_ctx)
    y = _experts(xs, gate, wg, wu, wd)
    return _combine(pos, first, count, y, h, mods, g_post, n_lat, n_ctx)


def _conv_body(x_ref, w_ref, o_ref, pad_scr, *, n_lat):
    j = pl.program_id(1)
    nt = x_ref.shape[1]
    halo = CONV_HALO
    zeros = jnp.zeros((halo, LANES), F32)
    for lo, hi in ((0, n_lat), (n_lat, nt)):
        n = hi - lo
        if n == 0:
            continue
        pad_scr[0:halo, :] = zeros
        pad_scr[halo:halo + n, :] = x_ref[0, lo:hi, :].astype(F32)
        pad_scr[halo + n:2 * halo + n, :] = zeros
        acc = jnp.zeros((n, LANES), F32)
        for tap in range(CONV_K):
            start = halo + tap - CONV_K // 2
            acc = acc + pad_scr[start:start + n, :] * w_ref[tap:tap + 1, :]
        y = _silu(acc)
        unit = y * lax.rsqrt(jnp.sum(y * y, axis=-1, keepdims=True) + EPS)
        y = jnp.where(j < GDN_HEADS, unit * GDN_HEAD_DIM ** -0.5, jnp.where(j < 2 * GDN_HEADS, unit, y))
        o_ref[0, lo:hi, :] = y


def _short_conv(p, w_conv, n_lat):
    bsz, nt, _ = p.shape
    nblk = 3 * GDN_HEADS
    return pl.pallas_call(
        functools.partial(_conv_body, n_lat=n_lat),
        grid=(bsz, nblk),
        in_specs=[pl.BlockSpec((1, nt, LANES), lambda b, j: (b, 0, j)),
                  pl.BlockSpec((CONV_K, LANES), lambda b, j: (0, j))],
        out_specs=pl.BlockSpec((1, nt, LANES), lambda b, j: (b, 0, j)),
        out_shape=jax.ShapeDtypeStruct((bsz, nt, nblk * LANES), F32),
        scratch_shapes=[pltpu.VMEM((n_lat + 2 * CONV_HALO, LANES), F32)],
        compiler_params=_cparams("arbitrary", "arbitrary"),
        name="gdn_conv",
    )(p, w_conv)


def _gdn_body(q_ref, k_ref, v_ref, gate_ref, gate_t_ref, alog_ref, dt_ref, alog_t_ref, dt_t_ref,
              o_ref, s_scr, gct_scr, *, rev, n_lat_groups):
    hb = pl.program_id(1)
    step = pl.program_id(2)
    r_t = ROW_TILE
    c_sz = GDN_CHUNK
    goff = 2 * GDN_HEADS if rev else 0

    @pl.when(step == 0)
    def _():
        s_scr[...] = jnp.zeros_like(s_scr)

    ii = lax.broadcasted_iota(jnp.int32, (r_t, r_t), 0)
    jj = lax.broadcasted_iota(jnp.int32, (r_t, r_t), 1)
    same = (ii // c_sz) == (jj // c_sz)
    if rev:
        incl, strict = same & (jj >= ii), same & (jj > ii)
    else:
        incl, strict = same & (jj <= ii), same & (jj < ii)
    m_incl = incl.astype(BF16)
    m_same = same.astype(BF16)
    eye = (ii == jj).astype(F32)

    def pieces(x):
        hi = x.astype(BF16)
        rest = x - hi.astype(F32)
        mid = rest.astype(BF16)
        return hi, mid, (rest - mid.astype(F32)).astype(BF16)

    lane = lax.broadcasted_iota(jnp.int32, (1, LANES), 1)
    is_g = (lane >= goff) & (lane < goff + GDN_HEADS)
    raw = gate_ref[0]
    g_all = jnp.where(is_g, -jnp.exp(alog_ref[...]) * _softplus(raw + dt_ref[...]), 0.0)
    beta_all = _sigmoid(raw)
    g_parts = pieces(g_all)
    gc_all = sum(_dot(m_incl, x) for x in g_parts)
    gl_all = sum(_dot(m_same, x) for x in g_parts)
    g_t = -jnp.exp(alog_t_ref[...]) * _softplus(gate_t_ref[0] + dt_t_ref[...])
    gct_scr[...] = sum(_dot_nt(x, m_incl) for x in pieces(g_t))
    lane_f = lax.broadcasted_iota(jnp.int32, (r_t, LANES), 1)

    def column(a, idx):
        return jnp.sum(jnp.where(lane_f == idx, a, 0.0), axis=1, keepdims=True)

    heads = range(GDN_HEADS_PER_STEP)
    n_chunks = r_t // c_sz
    n_levels = int(math.log2(c_sz)) - 1
    st = []
    for hh in heads:
        head = hb * GDN_HEADS_PER_STEP + hh
        cols = slice(hh * GDN_HEAD_DIM, (hh + 1) * GDN_HEAD_DIM)
        q, k, v = q_ref[0, :, cols], k_ref[0, :, cols], v_ref[0, :, cols]
        gc = column(gc_all, goff + head)
        gl = column(gl_all, goff + head)
        beta = column(beta_all, goff + GDN_HEADS + head)
        gc_row = gct_scr[pl.ds(goff + head, 1), :]
        decay = jnp.where(incl, jnp.exp(jnp.where(incl, gc - gc_row, 0.0)), 0.0)
        kb = k * beta
        k_bf = k.astype(BF16)
        a_mat = jnp.where(strict, _dot_nt(kb.astype(BF16), k_bf) * decay, 0.0)
        st.append(dict(
            cols=cols, gl=gl, inv=eye - a_mat, pw=a_mat,
            rhs=jnp.concatenate([v * beta, kb * jnp.exp(gc)], axis=1).astype(BF16),
            intra=(_dot_nt(q.astype(BF16), k_bf) * decay).astype(BF16),
            q_dec=(q * jnp.exp(gc)).astype(BF16),
            k_dec_t=(k * jnp.exp(gl - gc)).T.astype(BF16),
            s=s_scr[hh], outs=[None] * n_chunks))
    for _ in range(n_levels):
        for t in st:
            pw_bf = t["pw"].astype(BF16)
            t["pw"] = _dot(pw_bf, pw_bf)
        for t in st:
            t["inv"] = t["inv"] + _dot(t["inv"].astype(BF16), t["pw"].astype(BF16))
    for t in st:
        uw = _dot(t["inv"].astype(BF16), t["rhs"])
        t["u"], t["w"] = uw[:, :GDN_HEAD_DIM], uw[:, GDN_HEAD_DIM:].astype(BF16)
    for c in (range(n_chunks - 1, -1, -1) if rev else range(n_chunks)):
        rows = slice(c * c_sz, (c + 1) * c_sz)
        for t in st:
            t["s_bf"] = t["s"].astype(BF16)
            t["v_new"] = (t["u"][rows] - _dot(t["w"][rows], t["s_bf"])).astype(BF16)
        for t in st:
            t["outs"][c] = _dot(t["q_dec"][rows], t["s_bf"]) + _dot(t["intra"][rows, rows], t["v_new"])
            t["s"] = (t["s"] * jnp.exp(t["gl"][c * c_sz:c * c_sz + 1, :])
                      + _dot(t["k_dec_t"][:, rows], t["v_new"]))
    for hh, t in enumerate(st):
        s_scr[hh] = t["s"]
        o_ref[0, :, t["cols"]] = jnp.concatenate(t["outs"], axis=0)


def _gdn_scan(qkv, gates, gates_t, a_log_row, dt_row, a_log_col, dt_col, n_lat, rev):
    bsz, nt, _ = qkv.shape
    hps = GDN_HEADS_PER_STEP
    hblocks = GDN_HEADS // hps
    n_groups = nt // ROW_TILE
    lat_groups = n_lat // ROW_TILE
    width = hps * GDN_HEAD_DIM

    def group(i):
        lat = (lat_groups - i) if rev else (i - 1)
        return jnp.where(i == 0, n_groups - 1, lat)

    return pl.pallas_call(
        functools.partial(_gdn_body, rev=rev, n_lat_groups=lat_groups),
        grid=(bsz, hblocks, n_groups),
        in_specs=[pl.BlockSpec((1, ROW_TILE, width), lambda b, h, i: (b, group(i), h)),
                  pl.BlockSpec((1, ROW_TILE, width), lambda b, h, i: (b, group(i), hblocks + h)),
                  pl.BlockSpec((1, ROW_TILE, width), lambda b, h, i: (b, group(i), 2 * hblocks + h)),
                  pl.BlockSpec((1, ROW_TILE, LANES), lambda b, h, i: (b, group(i), 0)),
                  pl.BlockSpec((1, 4 * GDN_HEADS, ROW_TILE), lambda b, h, i: (b, 0, group(i))),
                  pl.BlockSpec((1, LANES), lambda b, h, i: (0, 0)),
                  pl.BlockSpec((1, LANES), lambda b, h, i: (0, 0)),
                  pl.BlockSpec((4 * GDN_HEADS, 1), lambda b, h, i: (0, 0)),
                  pl.BlockSpec((4 * GDN_HEADS, 1), lambda b, h, i: (0, 0))],
        out_specs=pl.BlockSpec((1, ROW_TILE, width), lambda b, h, i: (b, group(i), h)),
        out_shape=jax.ShapeDtypeStruct((bsz, nt, GDN_HEADS * GDN_HEAD_DIM), F32),
        scratch_shapes=[pltpu.VMEM((hps, GDN_HEAD_DIM, GDN_HEAD_DIM), F32),
                        pltpu.VMEM((4 * GDN_HEADS, ROW_TILE), F32)],
        compiler_params=_cparams("arbitrary", "arbitrary", "arbitrary"),
        name="gdn_scan_bwd" if rev else "gdn_scan_fwd",
    )(qkv, qkv, qkv, gates, gates_t, a_log_row, dt_row, a_log_col, dt_col)


def _gdn_out_body(of_ref, ob_ref, z_ref, gn_ref, w_ref, h_ref, mod_ref, g_ref, g_ffn_ref, wr_ref,
                  out_ref, u_ref, aff_ref, a_scr):
    for hh in range(GDN_HEADS):
        cols = slice(hh * GDN_HEAD_DIM, (hh + 1) * GDN_HEAD_DIM)
        o = of_ref[0, :, cols] + ob_ref[0, :, cols]
        a_scr[:, cols] = (_rms(o, gn_ref[...]) * _silu(z_ref[0, :, cols].astype(F32))).astype(BF16)
    y = _dot(a_scr[...], w_ref[...])
    _residual_and_route(h_ref[0], y, mod_ref[0, 0], g_ref, g_ffn_ref, wr_ref, out_ref, u_ref, aff_ref)


def _gdn_out(o_f, o_b, p, g_onorm, w_bf, h, mods, g_post, g_ffn, w_router, n_lat):
    bsz, _, d = h.shape
    width = GDN_HEADS * GDN_HEAD_DIM
    r_in, r_out, r_shape = _route_specs(bsz, n_lat, d)
    return pl.pallas_call(
        _gdn_out_body,
        grid=(bsz, n_lat // ROW_TILE),
        in_specs=[pl.BlockSpec((1, ROW_TILE, width), lambda b, i: (b, i, 0)),
                  pl.BlockSpec((1, ROW_TILE, width), lambda b, i: (b, i, 0)),
                  pl.BlockSpec((1, ROW_TILE, width), lambda b, i: (b, i, 3)),
                  pl.BlockSpec((1, GDN_HEAD_DIM), lambda b, i: (0, 0)),
                  pl.BlockSpec((width, d), lambda b, i: (0, 0)),
                  pl.BlockSpec((1, ROW_TILE, d), lambda b, i: (b, i, 0)),
                  pl.BlockSpec((1, 1, N_MOD, d), lambda b, i: (b, 0, 0, 0)),
                  pl.BlockSpec((1, d), lambda b, i: (0, 0))] + r_in,
        out_specs=r_out,
        out_shape=r_shape,
        scratch_shapes=[pltpu.VMEM((ROW_TILE, width), BF16)],
        compiler_params=_cparams("arbitrary", "arbitrary"),
        name="gdn_out",
    )(o_f, o_b, p, g_onorm[None, :], w_bf, h, mods, g_post[None, :], g_ffn[None, :], _pad_lanes(w_router))


def _gate_vectors(a_log_f, dt_bias_f, a_log_b, dt_bias_b):
    z = jnp.zeros((GDN_HEADS,), F32)
    a = jnp.concatenate([a_log_f, z, a_log_b, z])
    d = jnp.concatenate([dt_bias_f, z, dt_bias_b, z])
    pad = jnp.zeros((LANES - 4 * GDN_HEADS,), F32)
    return (jnp.concatenate([a, pad])[None, :], jnp.concatenate([d, pad])[None, :], a[:, None], d[:, None])


def kernel(x, c, ctx, c_ctx, l0_w_ada, l0_b_ada, l0_g_pre_mix, l0_g_post_mix, l0_g_pre_ffn, l0_g_post_ffn, l0_w_qkv, l0_lambda_q1, l0_lambda_k1, l0_lambda_q2, l0_lambda_k2, l0_g_subln, l0_w_o, l0_w_router, l0_w_gate, l0_w_up, l0_w_down, l1_w_ada, l1_b_ada, l1_g_pre_mix, l1_g_post_mix, l1_g_pre_ffn, l1_g_post_ffn, l1_w_in, l1_w_conv, l1_a_log_f, l1_dt_bias_f, l1_a_log_b, l1_dt_bias_b, l1_g_onorm, l1_w_o, l1_w_router, l1_w_gate, l1_w_up, l1_w_down):
    n_lat, n_ctx = x.shape[1], ctx.shape[1]
    h = jnp.concatenate([x, ctx], axis=1)

    mods = _ada_mods(c, c_ctx, l0_w_ada, l0_b_ada)
    qkv = _project(h, mods, l0_g_pre_mix, l0_w_qkv.astype(BF16), n_lat, rope_tables=_rope_tables(n_lat, n_ctx))
    lam_vecs = jnp.stack([l0_lambda_q1, l0_lambda_k1, l0_lambda_q2, l0_lambda_k2])
    o = _diff_attention(qkv, lam_vecs, l0_g_subln, n_lat, depth=0)
    h, u, aff = _out_project(o, l0_w_o.astype(BF16), h, mods, l0_g_post_mix, l0_g_pre_ffn, l0_w_router, n_lat)
    h = _moe(h, u, aff, mods, l0_g_post_ffn, l0_w_gate, l0_w_up, l0_w_down, n_lat, n_ctx)

    mods = _ada_mods(c, c_ctx, l1_w_ada, l1_b_ada)
    width = GDN_HEADS * GDN_HEAD_DIM
    w_gates = jnp.pad(l1_w_in[:, 4 * width:], ((0, 0), (0, LANES - 4 * GDN_HEADS))).astype(BF16)
    p, gates = _project(h, mods, l1_g_pre_mix, l1_w_in[:, :4 * width].astype(BF16), n_lat, w_gates=w_gates)
    gates_t = jnp.swapaxes(gates[:, :, :4 * GDN_HEADS], 1, 2)
    qkv = _short_conv(p, l1_w_conv, n_lat)
    gv = _gate_vectors(l1_a_log_f, l1_dt_bias_f, l1_a_log_b, l1_dt_bias_b)
    o_f = _gdn_scan(qkv, gates, gates_t, *gv, n_lat, rev=False)
    o_b = _gdn_scan(qkv, gates, gates_t, *gv, n_lat, rev=True)
    hl, u, aff = _gdn_out(o_f, o_b, p, l1_g_onorm, l1_w_o.astype(BF16), h, mods, l1_g_post_mix, l1_g_pre_ffn,
                          l1_w_router, n_lat)
    return _moe(hl, u, aff, mods, l1_g_post_ffn, l1_w_gate, l1_w_up, l1_w_down, n_lat, 0)
```

```python
import functools
import math

import jax
import jax.numpy as jnp
from jax import lax
from jax.experimental import pallas as pl
from jax.experimental.pallas import tpu as pltpu

F32 = jnp.float32
BF16 = jnp.bfloat16
HIGHEST = lax.Precision.HIGHEST

EPS = 1e-6
N_MOD = 6
GRID_W = 64
ROPE_BASE = 10000.0
DA_HEADS = 8
DA_HEAD_DIM = 64
GDN_HEADS = 8
GDN_HEAD_DIM = 128
CONV_K = 5
CONV_HALO = 8
N_EXPERTS = 16
CAP_FACTOR = 2

LANES = 128
ROW_TILE = 256
PROJ_ROWS = 768
ATTN_Q_ROWS = 512
ATTN_CHAIN_ROWS = 128
Q_PRESCALE = DA_HEAD_DIM ** -0.5 * math.log2(math.e)
GDN_CHUNK = 64
GDN_HEADS_PER_STEP = 8
FF_CHUNK = 512
SLOT_WINDOW = 64
SLOT_ALIGN = 16
TABLE_COUNT = 64
VMEM_LIMIT = 56 * 1024 * 1024

SH_M, SC_M, GT_M, SH_F, SC_F, GT_F = range(6)


def _cparams(*sem):
    return pltpu.CompilerParams(dimension_semantics=sem, vmem_limit_bytes=VMEM_LIMIT)


def _dot(a, b):
    return jnp.dot(a, b, preferred_element_type=F32)


def _dot_nt(a, b, precision=None):
    return lax.dot_general(a, b, (((1,), (1,)), ((), ())), preferred_element_type=F32, precision=precision)


def _dot_tn(a, b):
    return lax.dot_general(a, b, (((0,), (0,)), ((), ())), preferred_element_type=F32)


def _rms(x, g):
    return x * lax.rsqrt(jnp.mean(x * x, axis=-1, keepdims=True) + EPS) * g


def _sigmoid(x):
    return 1.0 / (1.0 + jnp.exp(-x))


def _silu(x):
    return x * _sigmoid(x)


def _softplus(x):
    return jnp.maximum(x, 0.0) + jnp.log(1.0 + jnp.exp(-jnp.abs(x)))


def _ada_body(c_ref, w_ref, b_ref, o_ref):
    o_ref[...] = jnp.dot(_silu(c_ref[...]), w_ref[...], precision=HIGHEST, preferred_element_type=F32) + b_ref[...]


def _ada_mods(c, c_ctx, w_ada, b_ada):
    bsz, d = c.shape
    rows = 16
    cc = jnp.concatenate([c, c_ctx[None, :], jnp.zeros((rows - bsz - 1, d), F32)], axis=0)
    tn = 1024
    m = pl.pallas_call(
        _ada_body,
        grid=(N_MOD * d // tn,),
        in_specs=[pl.BlockSpec((rows, d), lambda j: (0, 0)),
                  pl.BlockSpec((d, tn), lambda j: (0, j)),
                  pl.BlockSpec((1, tn), lambda j: (0, j))],
        out_specs=pl.BlockSpec((rows, tn), lambda j: (0, j)),
        out_shape=jax.ShapeDtypeStruct((rows, N_MOD * d), F32),
        compiler_params=_cparams("arbitrary"),
        name="ada_mods",
    )(cc, w_ada, b_ada[None, :])
    lat = m[:bsz].reshape(bsz, 1, N_MOD, d)
    ctx = jnp.broadcast_to(m[bsz].reshape(1, 1, N_MOD, d), (bsz, 1, N_MOD, d))
    return jnp.concatenate([lat, ctx], axis=1)


def _proj_body(h_ref, mod_ref, g_ref, w_ref, *rest, n_lat, rope, gates):
    rest = list(rest)
    cos_ref, sin_ref = (rest.pop(0), rest.pop(0)) if rope else (None, None)
    wg_ref = rest.pop(0) if gates else None
    o_ref = rest.pop(0)
    og_ref = rest.pop(0) if gates else None
    x = h_ref[0]
    tm, d = x.shape
    row = pl.program_id(1) * tm + lax.broadcasted_iota(jnp.int32, (tm, 1), 0)
    is_ctx = row >= n_lat
    scale = jnp.where(is_ctx, mod_ref[0, 1, SC_M:SC_M + 1, :], mod_ref[0, 0, SC_M:SC_M + 1, :])
    shift = jnp.where(is_ctx, mod_ref[0, 1, SH_M:SH_M + 1, :], mod_ref[0, 0, SH_M:SH_M + 1, :])
    u = (_rms(x, g_ref[...]) * (1.0 + scale) + shift).astype(BF16)
    if gates:
        og_ref[0] = _dot(u, wg_ref[...])
    for j in range(w_ref.shape[1] // d):
        cols = slice(j * d, (j + 1) * d)
        acc = _dot(u, w_ref[:, cols])
        if rope and j < 2:
            reps = d // LANES
            cos = jnp.tile(cos_ref[...], (1, reps))
            sin = jnp.tile(sin_ref[...], (1, reps))
            lane = lax.broadcasted_iota(jnp.int32, acc.shape, 1)
            half = DA_HEAD_DIM // 4
            first = (lane % (2 * half)) < half
            partner = jnp.where(first, pltpu.roll(acc, d - half, 1), pltpu.roll(acc, half, 1))
            acc = acc * cos + partner * sin
            if j == 0:
                acc = acc * Q_PRESCALE
        o_ref[0, :, cols] = acc.astype(o_ref.dtype)


def _project(h, mods, g, w_bf, n_lat, rope_tables=None, w_gates=None):
    bsz, nt, d = h.shape
    n_out = w_bf.shape[1]
    tm = PROJ_ROWS
    rope = rope_tables is not None
    gates = w_gates is not None
    in_specs = [pl.BlockSpec((1, tm, d), lambda b, i: (b, i, 0)),
                pl.BlockSpec((1, 2, N_MOD, d), lambda b, i: (b, 0, 0, 0)),
                pl.BlockSpec((1, d), lambda b, i: (0, 0)),
                pl.BlockSpec((d, n_out), lambda b, i: (0, 0))]
    args = [h, mods, g[None, :], w_bf]
    out_specs = [pl.BlockSpec((1, tm, n_out), lambda b, i: (b, i, 0))]
    out_shape = [jax.ShapeDtypeStruct((bsz, nt, n_out), BF16)]
    if rope:
        in_specs += [pl.BlockSpec((tm, LANES), lambda b, i: (i, 0))] * 2
        args += list(rope_tables)
    if gates:
        in_specs.append(pl.BlockSpec((d, LANES), lambda b, i: (0, 0)))
        args.append(w_gates)
        out_specs.append(pl.BlockSpec((1, tm, LANES), lambda b, i: (b, i, 0)))
        out_shape.append(jax.ShapeDtypeStruct((bsz, nt, LANES), F32))
    out = pl.pallas_call(
        functools.partial(_proj_body, n_lat=n_lat, rope=rope, gates=gates),
        grid=(bsz, nt // tm),
        in_specs=in_specs,
        out_specs=out_specs,
        out_shape=out_shape,
        compiler_params=_cparams("arbitrary", "arbitrary"),
        name="mod_project",
    )(*args)
    return out if gates else out[0]


def _rope_tables(n_lat, n_ctx):
    rows = n_lat // GRID_W
    r = jnp.repeat(jnp.arange(rows), GRID_W).astype(F32)
    col = jnp.tile(jnp.arange(GRID_W), rows).astype(F32)
    half = DA_HEAD_DIM // 2
    inv = ROPE_BASE ** (-jnp.arange(0, half, 2, dtype=F32) / half)
    ang_r, ang_c = r[:, None] * inv, col[:, None] * inv
    cos = jnp.concatenate([jnp.cos(ang_r)] * 2 + [jnp.cos(ang_c)] * 2, axis=-1)
    sin = jnp.concatenate([-jnp.sin(ang_r), jnp.sin(ang_r), -jnp.sin(ang_c), jnp.sin(ang_c)], axis=-1)
    cos = jnp.concatenate([jnp.tile(cos, (1, 2)), jnp.ones((n_ctx, LANES), F32)], axis=0)
    sin = jnp.concatenate([jnp.tile(sin, (1, 2)), jnp.zeros((n_ctx, LANES), F32)], axis=0)
    return cos, sin


def _attn_body(lam_ref, q_ref, k_ref, v_ref, gs_ref, o_ref, *, lam_init):
    lv = lam_ref[...]
    lam = (jnp.exp(jnp.sum(lv[0:1] * lv[1:2], axis=-1, keepdims=True))
           - jnp.exp(jnp.sum(lv[2:3] * lv[3:4], axis=-1, keepdims=True)) + lam_init)
    q, k, v = q_ref[0], k_ref[0], v_ref[0]
    lane = lax.broadcasted_iota(jnp.int32, q.shape, 1)
    hw = v.shape[1]
    v_one = jnp.concatenate([v, jnp.ones_like(v)], axis=1)

    rows = min(q.shape[0], ATTN_CHAIN_ROWS)
    starts = range(0, q.shape[0], rows)
    qm = [jnp.where((lane >= c * DA_HEAD_DIM) & (lane < (c + 1) * DA_HEAD_DIM), q, jnp.zeros_like(q)) for c in range(2)]
    s = [_dot_nt(jnp.concatenate([qm[0][r:r + rows], qm[1][r:r + rows]], axis=0), k) for r in starts]
    e = [jnp.exp2((x - jnp.max(x, axis=-1, keepdims=True)).astype(BF16)) for x in s]
    ov = [_dot(x, v_one) for x in e]
    att = [x[:, :hw] / x[:, hw:hw + 1] for x in ov]
    for r, a in zip(starts, att):
        o = a[:rows] - lam * a[rows:]
        o_ref[0, r:r + rows, :] = (_rms(o, gs_ref[...]) * (1.0 - lam_init)).astype(o_ref.dtype)


def _diff_attention(qkv, lam_vecs, g_subln, n_lat, depth):
    bsz, nt, d3 = qkv.shape
    d = d3 // 3
    hw = 2 * DA_HEAD_DIM
    lam_init = 0.8 - 0.6 * math.exp(-0.3 * depth)
    n_ctx = nt - n_lat

    def call(n_q, tq, q_blk0, n_k, k_blk0):
        return pl.pallas_call(
            functools.partial(_attn_body, lam_init=lam_init),
            grid=(bsz, DA_HEADS, n_q // tq),
            in_specs=[pl.BlockSpec((4, DA_HEAD_DIM), lambda b, h, i: (0, 0)),
                      pl.BlockSpec((1, tq, hw), lambda b, h, i: (b, q_blk0 + i, h)),
                      pl.BlockSpec((1, n_k, hw), lambda b, h, i: (b, k_blk0, DA_HEADS + h)),
                      pl.BlockSpec((1, n_k, hw), lambda b, h, i: (b, k_blk0, 2 * DA_HEADS + h)),
                      pl.BlockSpec((1, hw), lambda b, h, i: (0, 0))],
            out_specs=pl.BlockSpec((1, tq, hw), lambda b, h, i: (b, i, h)),
            out_shape=jax.ShapeDtypeStruct((bsz, n_q, d), BF16),
            compiler_params=_cparams("arbitrary", "arbitrary", "arbitrary"),
            name="diff_attention",
        )(lam_vecs, qkv, qkv, qkv, g_subln[None, :])

    o_lat = call(n_lat, ATTN_Q_ROWS, 0, nt, 0)
    o_ctx = call(n_ctx, n_ctx, n_lat // n_ctx, n_ctx, n_lat // n_ctx)
    return jnp.concatenate([o_lat, o_ctx], axis=1)


def _residual_and_route(h, y, mod, g_post_ref, g_ffn_ref, wr_ref, out_ref, u_ref, aff_ref):
    h_new = h + mod[GT_M:GT_M + 1, :] * _rms(y, g_post_ref[...])
    out_ref[0] = h_new
    u = _rms(h_new, g_ffn_ref[...]) * (1.0 + mod[SC_F:SC_F + 1, :]) + mod[SH_F:SH_F + 1, :]
    u_hi = u.astype(BF16)
    u_ref[0] = u_hi
    u_lo = (u - u_hi.astype(F32)).astype(BF16)
    w = wr_ref[...]
    w_hi = w.astype(BF16)
    w_lo = (w - w_hi.astype(F32)).astype(BF16)
    logits = (_dot(u_hi, w_hi) + _dot(u_hi, w_lo) + _dot(u_lo, w_hi)).T[:N_EXPERTS]
    e = jnp.exp(logits - jnp.max(logits, axis=0, keepdims=True))
    aff_ref[0] = e / jnp.sum(e, axis=0, keepdims=True)


def _pad_lanes(w):
    return jnp.pad(w, ((0, 0), (0, LANES - w.shape[1])))


def _route_specs(bsz, nt, d):
    in_specs = [pl.BlockSpec((1, d), lambda b, i: (0, 0)),
                pl.BlockSpec((d, LANES), lambda b, i: (0, 0))]
    out_specs = [pl.BlockSpec((1, ROW_TILE, d), lambda b, i: (b, i, 0)),
                 pl.BlockSpec((1, ROW_TILE, d), lambda b, i: (b, i, 0)),
                 pl.BlockSpec((1, N_EXPERTS, ROW_TILE), lambda b, i: (b, 0, i))]
    out_shape = [jax.ShapeDtypeStruct((bsz, nt, d), F32),
                 jax.ShapeDtypeStruct((bsz, nt, d), BF16),
                 jax.ShapeDtypeStruct((bsz, N_EXPERTS, nt), F32)]
    return in_specs, out_specs, out_shape


def _oproj_body(o_ref, w_ref, h_ref, mod_ref, g_ref, g_ffn_ref, wr_ref, out_ref, u_ref, aff_ref):
    y = _dot(o_ref[0], w_ref[...])
    _residual_and_route(h_ref[0], y, mod_ref[0, 0], g_ref, g_ffn_ref, wr_ref, out_ref, u_ref, aff_ref)


def _out_project(o, w_bf, h, mods, g_post, g_ffn, w_router, n_lat):
    bsz, nt, d = h.shape
    lat_tiles = n_lat // ROW_TILE
    r_in, r_out, r_shape = _route_specs(bsz, nt, d)
    return pl.pallas_call(
        _oproj_body,
        grid=(bsz, nt // ROW_TILE),
        in_specs=[pl.BlockSpec((1, ROW_TILE, d), lambda b, i: (b, i, 0)),
                  pl.BlockSpec((d, d), lambda b, i: (0, 0)),
                  pl.BlockSpec((1, ROW_TILE, d), lambda b, i: (b, i, 0)),
                  pl.BlockSpec((1, 1, N_MOD, d), lambda b, i: (b, i // lat_tiles, 0, 0)),
                  pl.BlockSpec((1, d), lambda b, i: (0, 0))] + r_in,
        out_specs=r_out,
        out_shape=r_shape,
        compiler_params=_cparams("arbitrary", "arbitrary"),
        name="out_project",
    )(o, w_bf, h, mods, g_post[None, :], g_ffn[None, :], _pad_lanes(w_router))


def _lane_cumsum(x):
    n = x.shape[1]
    jj = lax.broadcasted_iota(jnp.int32, (ROW_TILE, ROW_TILE), 0)
    nn = lax.broadcasted_iota(jnp.int32, (ROW_TILE, ROW_TILE), 1)
    tri = (jj <= nn).astype(BF16)
    run = jnp.zeros((x.shape[0], 1), F32)
    parts = []
    for t in range(n // ROW_TILE):
        local = _dot(x[:, t * ROW_TILE:(t + 1) * ROW_TILE].astype(BF16), tri) + run
        parts.append(local)
        run = local[:, ROW_TILE - 1:ROW_TILE]
    return jnp.concatenate(parts, axis=1) if len(parts) > 1 else parts[0]


def _select_slots(aff, cap):
    bits = pltpu.bitcast(aff, jnp.int32)

    def step(t, lo):
        cand = lo | jnp.left_shift(jnp.int32(1), 30 - t)
        cnt = jnp.sum((bits >= cand).astype(F32), axis=1, keepdims=True)
        return jnp.where(cnt >= cap, cand, lo)

    thr = lax.fori_loop(0, 31, step, jnp.zeros((aff.shape[0], 1), jnp.int32))
    gt = bits > thr
    eq = bits == thr
    need = cap - jnp.sum(gt.astype(F32), axis=1, keepdims=True)
    eq_rank = _lane_cumsum(eq.astype(F32))
    sel = gt | (eq & (eq_rank <= need))
    sel_f = sel.astype(F32)
    slot = _lane_cumsum(sel_f) - 1.0
    return jnp.where(sel, slot, -1.0).astype(jnp.int32), sel_f


def _tile_table(sel_f, tile0):
    lane = lax.broadcasted_iota(jnp.int32, (sel_f.shape[0], LANES), 1)
    run = jnp.zeros((sel_f.shape[0], 1), F32)
    tab = jnp.zeros((sel_f.shape[0], LANES), F32)
    for t in range(sel_f.shape[1] // ROW_TILE):
        inside = jnp.sum(sel_f[:, t * ROW_TILE:(t + 1) * ROW_TILE], axis=1, keepdims=True)
        tab = jnp.where(lane == tile0 + t, run, tab)
        tab = jnp.where(lane == TABLE_COUNT + tile0 + t, inside, tab)
        run = run + inside
    return tab


def _select_body(aff_ref, pos_ref, tab_ref, *, n_lat, n_ctx):
    aff = aff_ref[0]
    pos, sel_f = _select_slots(aff[:, :n_lat], CAP_FACTOR * n_lat // N_EXPERTS)
    pos_ref[0, :, :n_lat] = pos
    tab = _tile_table(sel_f, 0)
    if n_ctx:
        pos, sel_f = _select_slots(aff[:, n_lat:], CAP_FACTOR * n_ctx // N_EXPERTS)
        pos_ref[0, :, n_lat:] = pos
        tab = tab + _tile_table(sel_f, n_lat // ROW_TILE)
    tab_ref[0] = tab.astype(jnp.int32)


def _select(aff, n_lat, n_ctx):
    bsz, e, nt = aff.shape
    n_tiles = nt // ROW_TILE
    pos, tab = pl.pallas_call(
        functools.partial(_select_body, n_lat=n_lat, n_ctx=n_ctx),
        grid=(bsz,),
        in_specs=[pl.BlockSpec((1, e, nt), lambda b: (b, 0, 0))],
        out_specs=[pl.BlockSpec((1, e, nt), lambda b: (b, 0, 0)),
                   pl.BlockSpec((1, e, LANES), lambda b: (b, 0, 0))],
        out_shape=[jax.ShapeDtypeStruct((bsz, e, nt), jnp.int32),
                   jax.ShapeDtypeStruct((bsz, e, LANES), jnp.int32)],
        compiler_params=_cparams("arbitrary"),
        name="moe_select",
    )(aff)
    first = tab[:, :, :n_tiles].reshape(-1)
    count = tab[:, :, TABLE_COUNT:TABLE_COUNT + n_tiles].reshape(-1)
    return pos, first, count


class _TileWindows:
    def __init__(self, first_ref, count_ref, pos_ref, n_lat, rows, lat_cap):
        b, tile = pl.program_id(0), pl.program_id(1)
        n_exp = pos_ref.shape[1]
        self.rows, self.n_exp = rows, n_exp
        base = jnp.where(tile * ROW_TILE >= n_lat, lat_cap, 0)
        pos = pos_ref[0]
        self.slot = jnp.where(pos >= 0, pos + base, -1)
        self.starts, self.rounds = [], 0
        for x in range(n_exp):
            at = (b * n_exp + x) * pl.num_programs(1) + tile
            first = first_ref[at]
            start = base + (first // SLOT_ALIGN) * SLOT_ALIGN
            self.starts.append(start)
            self.rounds = jnp.maximum(self.rounds, (base + first + count_ref[at] - start + SLOT_WINDOW - 1) // SLOT_WINDOW)
        self.ids0 = lax.broadcasted_iota(jnp.int32, (SLOT_WINDOW, ROW_TILE), 0)

    def window(self, x, r):
        want = self.starts[x] + r * SLOT_WINDOW
        s0 = jnp.minimum(want, self.rows - SLOT_WINDOW)
        ids = self.ids0 + s0
        hot = (self.slot[x:x + 1, :] == ids) & (ids >= want)
        return pl.multiple_of(x * self.rows + s0, SLOT_ALIGN), hot


def _as_bf16(mask):
    return jnp.where(mask, 1.0, 0.0).astype(BF16)


def _gather_body(first_ref, count_ref, pos_ref, aff_ref, u_ref, xs_ref, gate_ref, *, n_lat, rows, lat_cap):
    @pl.when(pl.program_id(1) == 0)
    def _():
        xs_ref[...] = jnp.zeros_like(xs_ref)
        gate_ref[...] = jnp.zeros_like(gate_ref)

    tw = _TileWindows(first_ref, count_ref, pos_ref, n_lat, rows, lat_cap)
    aff = aff_ref[0]
    u = u_ref[0]

    def one_round(r, carry):
        wins = [tw.window(x, r) for x in range(tw.n_exp)]
        part = _dot(jnp.concatenate([_as_bf16(hot) for _, hot in wins], axis=0), u)
        for x, (off, hot) in enumerate(wins):
            dst = pl.ds(off, SLOT_WINDOW)
            xs_ref[0, dst, :] = xs_ref[0, dst, :] + part[x * SLOT_WINDOW:(x + 1) * SLOT_WINDOW].astype(BF16)
            gate = jnp.sum(jnp.where(hot, aff[x:x + 1, :], 0.0), axis=1, keepdims=True)
            gate_ref[0, dst, :] = gate_ref[0, dst, :] + jnp.broadcast_to(gate, (SLOT_WINDOW, LANES))
        return carry

    lax.fori_loop(0, tw.rounds, one_round, 0)


def _slot_rows(n_lat, n_ctx):
    return CAP_FACTOR * n_lat // N_EXPERTS + CAP_FACTOR * n_ctx // N_EXPERTS


def _gather(pos, first, count, aff, u, n_lat, n_ctx):
    bsz, nt, d = u.shape
    e = pos.shape[1]
    rows = _slot_rows(n_lat, n_ctx)
    xs, gate = pl.pallas_call(
        functools.partial(_gather_body, n_lat=n_lat, rows=rows, lat_cap=CAP_FACTOR * n_lat // N_EXPERTS),
        grid_spec=pltpu.PrefetchScalarGridSpec(
            num_scalar_prefetch=2,
            grid=(bsz, nt // ROW_TILE),
            in_specs=[pl.BlockSpec((1, e, ROW_TILE), lambda b, i, *_: (b, 0, i)),
                      pl.BlockSpec((1, e, ROW_TILE), lambda b, i, *_: (b, 0, i)),
                      pl.BlockSpec((1, ROW_TILE, d), lambda b, i, *_: (b, i, 0))],
            out_specs=[pl.BlockSpec((1, e * rows, d), lambda b, i, *_: (b, 0, 0)),
                       pl.BlockSpec((1, e * rows, LANES), lambda b, i, *_: (b, 0, 0))]),
        out_shape=[jax.ShapeDtypeStruct((bsz, e * rows, d), BF16),
                   jax.ShapeDtypeStruct((bsz, e * rows, LANES), F32)],
        compiler_params=_cparams("arbitrary", "arbitrary"),
        name="moe_gather",
    )(first, count, pos, aff, u)
    return xs.reshape(bsz, e, rows, d), gate.reshape(bsz, e, rows, LANES)


def _expert_body(xs_ref, gate_ref, wg_hbm, wu_hbm, wd_hbm, y_ref, wg_scr, wu_scr, wd_scr,
                 wg_stage, wu_stage, wd_stage, sems):
    x, b = pl.program_id(0), pl.program_id(1)
    n_exp, n_chunks = pl.num_programs(0), pl.num_programs(1)
    rows_in, rows_mid = wg_stage.shape[0], wd_stage.shape[0]

    def chunk_copies(expert, chunk):
        return (pltpu.make_async_copy(wg_hbm.at[expert, pl.ds(chunk * rows_in, rows_in), :], wg_stage, sems.at[0]),
                pltpu.make_async_copy(wu_hbm.at[expert, pl.ds(chunk * rows_in, rows_in), :], wu_stage, sems.at[1]),
                pltpu.make_async_copy(wd_hbm.at[expert, pl.ds(chunk * rows_mid, rows_mid), :], wd_stage, sems.at[2]))

    def land(copy, chunk):
        at_in = pl.ds(pl.multiple_of(chunk * rows_in, SLOT_ALIGN), rows_in)
        at_mid = pl.ds(pl.multiple_of(chunk * rows_mid, SLOT_ALIGN), rows_mid)
        wg_scr[copy, at_in, :] = wg_stage[...].astype(BF16)
        wu_scr[copy, at_in, :] = wu_stage[...].astype(BF16)
        wd_scr[copy, at_mid, :] = wd_stage[...].astype(BF16)

    @pl.when((x == 0) & (b == 0))
    def _():
        def fetch(chunk, carry):
            copies = chunk_copies(0, chunk)
            for cp in copies:
                cp.start()
            for cp in copies:
                cp.wait()
            land(0, chunk)
            return carry

        lax.fori_loop(0, n_chunks, fetch, 0)

    def start_next(_, carry):
        for cp in chunk_copies(x + 1, b):
            cp.start()
        return carry

    lax.fori_loop(0, jnp.where(x + 1 < n_exp, 1, 0), start_next, 0)

    cur = x % 2
    xs = xs_ref[0, 0]
    acc = jnp.zeros((xs.shape[0], wd_scr.shape[2]), F32)
    for s in range(wg_scr.shape[2] // FF_CHUNK):
        cols = slice(s * FF_CHUNK, (s + 1) * FF_CHUNK)
        hid = _silu(_dot(xs, wg_scr[cur, :, cols])) * _dot(xs, wu_scr[cur, :, cols])
        acc = acc + _dot(hid.astype(BF16), wd_scr[cur, cols, :])
    y_ref[0, 0] = (acc * gate_ref[0, 0, :, 0:1]).astype(y_ref.dtype)

    @pl.when(x + 1 < n_exp)
    def _():
        for cp in chunk_copies(x + 1, b):
            cp.wait()
        land(1 - cur, b)


def _experts(xs, gate, wg, wu, wd):
    bsz, e, rows, d = xs.shape
    ff = wg.shape[2]
    return pl.pallas_call(
        _expert_body,
        grid=(e, bsz),
        in_specs=[pl.BlockSpec((1, 1, rows, d), lambda x, b: (b, x, 0, 0)),
                  pl.BlockSpec((1, 1, rows, LANES), lambda x, b: (b, x, 0, 0)),
                  pl.BlockSpec(memory_space=pl.ANY),
                  pl.BlockSpec(memory_space=pl.ANY),
                  pl.BlockSpec(memory_space=pl.ANY)],
        out_specs=pl.BlockSpec((1, 1, rows, d), lambda x, b: (b, x, 0, 0)),
        out_shape=jax.ShapeDtypeStruct((bsz, e, rows, d), BF16),
        scratch_shapes=[pltpu.VMEM((2, d, ff), BF16), pltpu.VMEM((2, d, ff), BF16), pltpu.VMEM((2, ff, d), BF16),
                        pltpu.VMEM((d // bsz, ff), F32), pltpu.VMEM((d // bsz, ff), F32),
                        pltpu.VMEM((ff // bsz, d), F32), pltpu.SemaphoreType.DMA((3,))],
        compiler_params=_cparams("arbitrary", "arbitrary"),
        name="moe_experts",
    )(xs, gate, wg, wu, wd)


def _combine_body(first_ref, count_ref, pos_ref, y_ref, h_ref, mod_ref, g_ref, out_ref, acc_scr,
                  *, n_lat, rows, lat_cap):
    tw = _TileWindows(first_ref, count_ref, pos_ref, n_lat, rows, lat_cap)
    acc_scr[...] = jnp.zeros_like(acc_scr)

    def one_round(r, carry):
        group = ROW_TILE // SLOT_WINDOW
        for g0 in range(0, tw.n_exp, group):
            wins = [tw.window(x, r) for x in range(g0, g0 + group)]
            hot = jnp.concatenate([_as_bf16(h) for _, h in wins], axis=0)
            rows_y = jnp.concatenate([y_ref[0, pl.ds(off, SLOT_WINDOW), :] for off, _ in wins], axis=0)
            acc_scr[...] += _dot_tn(hot, rows_y)
        return carry

    lax.fori_loop(0, tw.rounds, one_round, 0)
    out_ref[0] = h_ref[0] + mod_ref[0, 0, GT_F:GT_F + 1, :] * _rms(acc_scr[...], g_ref[...])


def _combine(pos, first, count, y, h, mods, g_post, n_lat, n_ctx):
    bsz, e, rows, d = y.shape
    nt = n_lat + n_ctx
    lat_tiles = n_lat // ROW_TILE
    return pl.pallas_call(
        functools.partial(_combine_body, n_lat=n_lat, rows=rows, lat_cap=CAP_FACTOR * n_lat // N_EXPERTS),
        grid_spec=pltpu.PrefetchScalarGridSpec(
            num_scalar_prefetch=2,
            grid=(bsz, nt // ROW_TILE),
            in_specs=[pl.BlockSpec((1, e, ROW_TILE), lambda b, i, *_: (b, 0, i)),
                      pl.BlockSpec((1, e * rows, d), lambda b, i, *_: (b, 0, 0)),
                      pl.BlockSpec((1, ROW_TILE, d), lambda b, i, *_: (b, i, 0)),
                      pl.BlockSpec((1, 1, N_MOD, d), lambda b, i, *_: (b, i // lat_tiles, 0, 0)),
                      pl.BlockSpec((1, d), lambda b, i, *_: (0, 0))],
            out_specs=pl.BlockSpec((1, ROW_TILE, d), lambda b, i, *_: (b, i, 0)),
            scratch_shapes=[pltpu.VMEM((ROW_TILE, d), F32)]),
        out_shape=jax.ShapeDtypeStruct((bsz, nt, d), F32),
        compiler_params=_cparams("arbitrary", "arbitrary"),
        name="moe_combine",
    )(first, count, pos, y.reshape(bsz, e * rows, d), h, mods, g_post[None, :])


def _moe(h, u, aff, mods, g_post, wg, wu, wd, n_lat, n_ctx):
    pos, first, count = _select(aff, n_lat, n_ctx)
    xs, gate = _gather(pos, first, count, aff, u, n_lat, n_ctx)
    y = _experts(xs, gate, wg, wu, wd)
    return _combine(pos, first, count, y, h, mods, g_post, n_lat, n_ctx)


def _conv_body(x_ref, w_ref, o_ref, pad_scr, *, n_lat):
    j = pl.program_id(1)
    nt = x_ref.shape[1]
    halo = CONV_HALO
    zeros = jnp.zeros((halo, LANES), F32)
    for lo, hi in ((0, n_lat), (n_lat, nt)):
        n = hi - lo
        if n == 0:
            continue
        pad_scr[0:halo, :] = zeros
        pad_scr[halo:halo + n, :] = x_ref[0, lo:hi, :].astype(F32)
        pad_scr[halo + n:2 * halo + n, :] = zeros
        acc = jnp.zeros((n, LANES), F32)
        for tap in range(CONV_K):
            start = halo + tap - CONV_K // 2
            acc = acc + pad_scr[start:start + n, :] * w_ref[tap:tap + 1, :]
        y = _silu(acc)
        unit = y * lax.rsqrt(jnp.sum(y * y, axis=-1, keepdims=True) + EPS)
        y = jnp.where(j < GDN_HEADS, unit * GDN_HEAD_DIM ** -0.5, jnp.where(j < 2 * GDN_HEADS, unit, y))
        o_ref[0, lo:hi, :] = y


def _short_conv(p, w_conv, n_lat):
    bsz, nt, _ = p.shape
    nblk = 3 * GDN_HEADS
    return pl.pallas_call(
        functools.partial(_conv_body, n_lat=n_lat),
        grid=(bsz, nblk),
        in_specs=[pl.BlockSpec((1, nt, LANES), lambda b, j: (b, 0, j)),
                  pl.BlockSpec((CONV_K, LANES), lambda b, j: (0, j))],
        out_specs=pl.BlockSpec((1, nt, LANES), lambda b, j: (b, 0, j)),
        out_shape=jax.ShapeDtypeStruct((bsz, nt, nblk * LANES), F32),
        scratch_shapes=[pltpu.VMEM((n_lat + 2 * CONV_HALO, LANES), F32)],
        compiler_params=_cparams("arbitrary", "arbitrary"),
        name="gdn_conv",
    )(p, w_conv)


def _gdn_body(q_ref, k_ref, v_ref, gate_ref, gate_t_ref, alog_ref, dt_ref, alog_t_ref, dt_t_ref,
              o_ref, s_scr, gct_scr, *, rev, n_lat_groups):
    hb = pl.program_id(1)
    step = pl.program_id(2)
    r_t = ROW_TILE
    c_sz = GDN_CHUNK
    goff = 2 * GDN_HEADS if rev else 0

    @pl.when(step == 0)
    def _():
        s_scr[...] = jnp.zeros_like(s_scr)

    ii = lax.broadcasted_iota(jnp.int32, (r_t, r_t), 0)
    jj = lax.broadcasted_iota(jnp.int32, (r_t, r_t), 1)
    same = (ii // c_sz) == (jj // c_sz)
    if rev:
        incl, strict = same & (jj >= ii), same & (jj > ii)
    else:
        incl, strict = same & (jj <= ii), same & (jj < ii)
    m_incl = incl.astype(BF16)
    m_same = same.astype(BF16)
    eye = (ii == jj).astype(F32)

    def pieces(x):
        hi = x.astype(BF16)
        rest = x - hi.astype(F32)
        mid = rest.astype(BF16)
        return hi, mid, (rest - mid.astype(F32)).astype(BF16)

    lane = lax.broadcasted_iota(jnp.int32, (1, LANES), 1)
    is_g = (lane >= goff) & (lane < goff + GDN_HEADS)
    raw = gate_ref[0]
    g_all = jnp.where(is_g, -jnp.exp(alog_ref[...]) * _softplus(raw + dt_ref[...]), 0.0)
    beta_all = _sigmoid(raw)
    g_parts = pieces(g_all)
    gc_all = sum(_dot(m_incl, x) for x in g_parts)
    gl_all = sum(_dot(m_same, x) for x in g_parts)
    g_t = -jnp.exp(alog_t_ref[...]) * _softplus(gate_t_ref[0] + dt_t_ref[...])
    gct_scr[...] = sum(_dot_nt(x, m_incl) for x in pieces(g_t))
    lane_f = lax.broadcasted_iota(jnp.int32, (r_t, LANES), 1)

    def column(a, idx):
        return jnp.sum(jnp.where(lane_f == idx, a, 0.0), axis=1, keepdims=True)

    heads = range(GDN_HEADS_PER_STEP)
    n_chunks = r_t // c_sz
    n_levels = int(math.log2(c_sz)) - 1
    st = []
    for hh in heads:
        head = hb * GDN_HEADS_PER_STEP + hh
        cols = slice(hh * GDN_HEAD_DIM, (hh + 1) * GDN_HEAD_DIM)
        q, k, v = q_ref[0, :, cols], k_ref[0, :, cols], v_ref[0, :, cols]
        gc = column(gc_all, goff + head)
        gl = column(gl_all, goff + head)
        beta = column(beta_all, goff + GDN_HEADS + head)
        gc_row = gct_scr[pl.ds(goff + head, 1), :]
        decay = jnp.where(incl, jnp.exp(jnp.where(incl, gc - gc_row, 0.0)), 0.0)
        kb = k * beta
        k_bf = k.astype(BF16)
        a_mat = jnp.where(strict, _dot_nt(kb.astype(BF16), k_bf) * decay, 0.0)
        st.append(dict(
            cols=cols, gl=gl, inv=eye - a_mat, pw=a_mat,
            rhs=jnp.concatenate([v * beta, kb * jnp.exp(gc)], axis=1).astype(BF16),
            intra=(_dot_nt(q.astype(BF16), k_bf) * decay).astype(BF16),
            q_dec=(q * jnp.exp(gc)).astype(BF16),
            k_dec_t=(k * jnp.exp(gl - gc)).T.astype(BF16),
            s=s_scr[hh], outs=[None] * n_chunks))
    for _ in range(n_levels):
        for t in st:
            pw_bf = t["pw"].astype(BF16)
            t["pw"] = _dot(pw_bf, pw_bf)
        for t in st:
            t["inv"] = t["inv"] + _dot(t["inv"].astype(BF16), t["pw"].astype(BF16))
    for t in st:
        uw = _dot(t["inv"].astype(BF16), t["rhs"])
        t["u"], t["w"] = uw[:, :GDN_HEAD_DIM], uw[:, GDN_HEAD_DIM:].astype(BF16)
    for c in (range(n_chunks - 1, -1, -1) if rev else range(n_chunks)):
        rows = slice(c * c_sz, (c + 1) * c_sz)
        for t in st:
            t["s_bf"] = t["s"].astype(BF16)
            t["v_new"] = (t["u"][rows] - _dot(t["w"][rows], t["s_bf"])).astype(BF16)
        for t in st:
            t["outs"][c] = _dot(t["q_dec"][rows], t["s_bf"]) + _dot(t["intra"][rows, rows], t["v_new"])
            t["s"] = (t["s"] * jnp.exp(t["gl"][c * c_sz:c * c_sz + 1, :])
                      + _dot(t["k_dec_t"][:, rows], t["v_new"]))
    for hh, t in enumerate(st):
        s_scr[hh] = t["s"]
        o_ref[0, :, t["cols"]] = jnp.concatenate(t["outs"], axis=0)


def _gdn_scan(qkv, gates, gates_t, a_log_row, dt_row, a_log_col, dt_col, n_lat, rev):
    bsz, nt, _ = qkv.shape
    hps = GDN_HEADS_PER_STEP
    hblocks = GDN_HEADS // hps
    n_groups = nt // ROW_TILE
    lat_groups = n_lat // ROW_TILE
    width = hps * GDN_HEAD_DIM

    def group(i):
        lat = (lat_groups - i) if rev else (i - 1)
        return jnp.where(i == 0, n_groups - 1, lat)

    return pl.pallas_call(
        functools.partial(_gdn_body, rev=rev, n_lat_groups=lat_groups),
        grid=(bsz, hblocks, n_groups),
        in_specs=[pl.BlockSpec((1, ROW_TILE, width), lambda b, h, i: (b, group(i), h)),
                  pl.BlockSpec((1, ROW_TILE, width), lambda b, h, i: (b, group(i), hblocks + h)),
                  pl.BlockSpec((1, ROW_TILE, width), lambda b, h, i: (b, group(i), 2 * hblocks + h)),
                  pl.BlockSpec((1, ROW_TILE, LANES), lambda b, h, i: (b, group(i), 0)),
                  pl.BlockSpec((1, 4 * GDN_HEADS, ROW_TILE), lambda b, h, i: (b, 0, group(i))),
                  pl.BlockSpec((1, LANES), lambda b, h, i: (0, 0)),
                  pl.BlockSpec((1, LANES), lambda b, h, i: (0, 0)),
                  pl.BlockSpec((4 * GDN_HEADS, 1), lambda b, h, i: (0, 0)),
                  pl.BlockSpec((4 * GDN_HEADS, 1), lambda b, h, i: (0, 0))],
        out_specs=pl.BlockSpec((1, ROW_TILE, width), lambda b, h, i: (b, group(i), h)),
        out_shape=jax.ShapeDtypeStruct((bsz, nt, GDN_HEADS * GDN_HEAD_DIM), F32),
        scratch_shapes=[pltpu.VMEM((hps, GDN_HEAD_DIM, GDN_HEAD_DIM), F32),
                        pltpu.VMEM((4 * GDN_HEADS, ROW_TILE), F32)],
        compiler_params=_cparams("arbitrary", "arbitrary", "arbitrary"),
        name="gdn_scan_bwd" if rev else "gdn_scan_fwd",
    )(qkv, qkv, qkv, gates, gates_t, a_log_row, dt_row, a_log_col, dt_col)


def _gdn_out_body(of_ref, ob_ref, z_ref, gn_ref, w_ref, h_ref, mod_ref, g_ref, g_ffn_ref, wr_ref,
                  out_ref, u_ref, aff_ref, a_scr):
    for hh in range(GDN_HEADS):
        cols = slice(hh * GDN_HEAD_DIM, (hh + 1) * GDN_HEAD_DIM)
        o = of_ref[0, :, cols] + ob_ref[0, :, cols]
        a_scr[:, cols] = (_rms(o, gn_ref[...]) * _silu(z_ref[0, :, cols].astype(F32))).astype(BF16)
    y = _dot(a_scr[...], w_ref[...])
    _residual_and_route(h_ref[0], y, mod_ref[0, 0], g_ref, g_ffn_ref, wr_ref, out_ref, u_ref, aff_ref)


def _gdn_out(o_f, o_b, p, g_onorm, w_bf, h, mods, g_post, g_ffn, w_router, n_lat):
    bsz, _, d = h.shape
    width = GDN_HEADS * GDN_HEAD_DIM
    r_in, r_out, r_shape = _route_specs(bsz, n_lat, d)
    return pl.pallas_call(
        _gdn_out_body,
        grid=(bsz, n_lat // ROW_TILE),
        in_specs=[pl.BlockSpec((1, ROW_TILE, width), lambda b, i: (b, i, 0)),
                  pl.BlockSpec((1, ROW_TILE, width), lambda b, i: (b, i, 0)),
                  pl.BlockSpec((1, ROW_TILE, width), lambda b, i: (b, i, 3)),
                  pl.BlockSpec((1, GDN_HEAD_DIM), lambda b, i: (0, 0)),
                  pl.BlockSpec((width, d), lambda b, i: (0, 0)),
                  pl.BlockSpec((1, ROW_TILE, d), lambda b, i: (b, i, 0)),
                  pl.BlockSpec((1, 1, N_MOD, d), lambda b, i: (b, 0, 0, 0)),
                  pl.BlockSpec((1, d), lambda b, i: (0, 0))] + r_in,
        out_specs=r_out,
        out_shape=r_shape,
        scratch_shapes=[pltpu.VMEM((ROW_TILE, width), BF16)],
        compiler_params=_cparams("arbitrary", "arbitrary"),
        name="gdn_out",
    )(o_f, o_b, p, g_onorm[None, :], w_bf, h, mods, g_post[None, :], g_ffn[None, :], _pad_lanes(w_router))


def _gate_vectors(a_log_f, dt_bias_f, a_log_b, dt_bias_b):
    z = jnp.zeros((GDN_HEADS,), F32)
    a = jnp.concatenate([a_log_f, z, a_log_b, z])
    d = jnp.concatenate([dt_bias_f, z, dt_bias_b, z])
    pad = jnp.zeros((LANES - 4 * GDN_HEADS,), F32)
    return (jnp.concatenate([a, pad])[None, :], jnp.concatenate([d, pad])[None, :], a[:, None], d[:, None])


def kernel(x, c, ctx, c_ctx, l0_w_ada, l0_b_ada, l0_g_pre_mix, l0_g_post_mix, l0_g_pre_ffn, l0_g_post_ffn, l0_w_qkv, l0_lambda_q1, l0_lambda_k1, l0_lambda_q2, l0_lambda_k2, l0_g_subln, l0_w_o, l0_w_router, l0_w_gate, l0_w_up, l0_w_down, l1_w_ada, l1_b_ada, l1_g_pre_mix, l1_g_post_mix, l1_g_pre_ffn, l1_g_post_ffn, l1_w_in, l1_w_conv, l1_a_log_f, l1_dt_bias_f, l1_a_log_b, l1_dt_bias_b, l1_g_onorm, l1_w_o, l1_w_router, l1_w_gate, l1_w_up, l1_w_down):
    n_lat, n_ctx = x.shape[1], ctx.shape[1]
    h = jnp.concatenate([x, ctx], axis=1)

    mods = _ada_mods(c, c_ctx, l0_w_ada, l0_b_ada)
    qkv = _project(h, mods, l0_g_pre_mix, l0_w_qkv.astype(BF16), n_lat, rope_tables=_rope_tables(n_lat, n_ctx))
    lam_vecs = jnp.stack([l0_lambda_q1, l0_lambda_k1, l0_lambda_q2, l0_lambda_k2])
    o = _diff_attention(qkv, lam_vecs, l0_g_subln, n_lat, depth=0)
    h, u, aff = _out_project(o, l0_w_o.astype(BF16), h, mods, l0_g_post_mix, l0_g_pre_ffn, l0_w_router, n_lat)
    h = _moe(h, u, aff, mods, l0_g_post_ffn, l0_w_gate, l0_w_up, l0_w_down, n_lat, n_ctx)

    mods = _ada_mods(c, c_ctx, l1_w_ada, l1_b_ada)
    width = GDN_HEADS * GDN_HEAD_DIM
    w_gates = jnp.pad(l1_w_in[:, 4 * width:], ((0, 0), (0, LANES - 4 * GDN_HEADS))).astype(BF16)
    p, gates = _project(h, mods, l1_g_pre_mix, l1_w_in[:, :4 * width].astype(BF16), n_lat, w_gates=w_gates)
    gates_t = jnp.swapaxes(gates[:, :, :4 * GDN_HEADS], 1, 2)
    qkv = _short_conv(p, l1_w_conv, n_lat)
    gv = _gate_vectors(l1_a_log_f, l1_dt_bias_f, l1_a_log_b, l1_dt_bias_b)
    o_f = _gdn_scan(qkv, gates, gates_t, *gv, n_lat, rev=False)
    o_b = _gdn_scan(qkv, gates, gates_t, *gv, n_lat, rev=True)
    hl, u, aff = _gdn_out(o_f, o_b, p, l1_g_onorm, l1_w_o.astype(BF16), h, mods, l1_g_post_mix, l1_g_pre_ffn,
                          l1_w_router, n_lat)
    return _moe(hl, u, aff, mods, l1_g_post_ffn, l1_w_gate, l1_w_up, l1_w_down, n_lat, 0)
```

```python
import functools
import math

import jax
import jax.numpy as jnp
from jax import lax
from jax.experimental import pallas as pl
from jax.experimental.pallas import tpu as pltpu

F32 = jnp.float32
BF16 = jnp.bfloat16
HIGHEST = lax.Precision.HIGHEST

EPS = 1e-6
N_MOD = 6
GRID_W = 64
ROPE_BASE = 10000.0
DA_HEADS = 8
DA_HEAD_DIM = 64
GDN_HEADS = 8
GDN_HEAD_DIM = 128
CONV_K = 5
CONV_HALO = 8
N_EXPERTS = 16
CAP_FACTOR = 2

LANES = 128
ROW_TILE = 256
PROJ_ROWS = 768
ATTN_Q_ROWS = 512
ATTN_CHAIN_ROWS = 128
Q_PRESCALE = DA_HEAD_DIM ** -0.5 * math.log2(math.e)
GDN_CHUNK = 64
GDN_HEADS_PER_STEP = 8
FF_CHUNK = 512
SLOT_WINDOW = 64
SLOT_ALIGN = 16
TABLE_COUNT = 64
VMEM_LIMIT = 56 * 1024 * 1024

SH_M, SC_M, GT_M, SH_F, SC_F, GT_F = range(6)


def _cparams(*sem):
    return pltpu.CompilerParams(dimension_semantics=sem, vmem_limit_bytes=VMEM_LIMIT)


def _dot(a, b):
    return jnp.dot(a, b, preferred_element_type=F32)


def _dot_nt(a, b, precision=None):
    return lax.dot_general(a, b, (((1,), (1,)), ((), ())), preferred_element_type=F32, precision=precision)


def _dot_tn(a, b):
    return lax.dot_general(a, b, (((0,), (0,)), ((), ())), preferred_element_type=F32)


def _rms(x, g):
    return x * lax.rsqrt(jnp.mean(x * x, axis=-1, keepdims=True) + EPS) * g


def _sigmoid(x):
    return 1.0 / (1.0 + jnp.exp(-x))


def _silu(x):
    return x * _sigmoid(x)


def _softplus(x):
    return jnp.maximum(x, 0.0) + jnp.log(1.0 + jnp.exp(-jnp.abs(x)))


def _ada_body(c_ref, w_ref, b_ref, o_ref):
    o_ref[...] = jnp.dot(_silu(c_ref[...]), w_ref[...], precision=HIGHEST, preferred_element_type=F32) + b_ref[...]


def _ada_mods(c, c_ctx, w_ada, b_ada):
    bsz, d = c.shape
    rows = 16
    cc = jnp.concatenate([c, c_ctx[None, :], jnp.zeros((rows - bsz - 1, d), F32)], axis=0)
    tn = 1024
    m = pl.pallas_call(
        _ada_body,
        grid=(N_MOD * d // tn,),
        in_specs=[pl.BlockSpec((rows, d), lambda j: (0, 0)),
                  pl.BlockSpec((d, tn), lambda j: (0, j)),
                  pl.BlockSpec((1, tn), lambda j: (0, j))],
        out_specs=pl.BlockSpec((rows, tn), lambda j: (0, j)),
        out_shape=jax.ShapeDtypeStruct((rows, N_MOD * d), F32),
        compiler_params=_cparams("arbitrary"),
        name="ada_mods",
    )(cc, w_ada, b_ada[None, :])
    lat = m[:bsz].reshape(bsz, 1, N_MOD, d)
    ctx = jnp.broadcast_to(m[bsz].reshape(1, 1, N_MOD, d), (bsz, 1, N_MOD, d))
    return jnp.concatenate([lat, ctx], axis=1)


def _proj_body(h_ref, mod_ref, g_ref, w_ref, *rest, n_lat, rope, gates):
    rest = list(rest)
    cos_ref, sin_ref = (rest.pop(0), rest.pop(0)) if rope else (None, None)
    wg_ref = rest.pop(0) if gates else None
    o_ref = rest.pop(0)
    og_ref = rest.pop(0) if gates else None
    x = h_ref[0]
    tm, d = x.shape
    row = pl.program_id(1) * tm + lax.broadcasted_iota(jnp.int32, (tm, 1), 0)
    is_ctx = row >= n_lat
    scale = jnp.where(is_ctx, mod_ref[0, 1, SC_M:SC_M + 1, :], mod_ref[0, 0, SC_M:SC_M + 1, :])
    shift = jnp.where(is_ctx, mod_ref[0, 1, SH_M:SH_M + 1, :], mod_ref[0, 0, SH_M:SH_M + 1, :])
    u = (_rms(x, g_ref[...]) * (1.0 + scale) + shift).astype(BF16)
    if gates:
        og_ref[0] = _dot(u, wg_ref[...])
    for j in range(w_ref.shape[1] // d):
        cols = slice(j * d, (j + 1) * d)
        acc = _dot(u, w_ref[:, cols])
        if rope and j < 2:
            reps = d // LANES
            cos = jnp.tile(cos_ref[...], (1, reps))
            sin = jnp.tile(sin_ref[...], (1, reps))
            lane = lax.broadcasted_iota(jnp.int32, acc.shape, 1)
            half = DA_HEAD_DIM // 4
            first = (lane % (2 * half)) < half
            partner = jnp.where(first, pltpu.roll(acc, d - half, 1), pltpu.roll(acc, half, 1))
            acc = acc * cos + partner * sin
            if j == 0:
                acc = acc * Q_PRESCALE
        o_ref[0, :, cols] = acc.astype(o_ref.dtype)


def _project(h, mods, g, w_bf, n_lat, rope_tables=None, w_gates=None):
    bsz, nt, d = h.shape
    n_out = w_bf.shape[1]
    tm = PROJ_ROWS
    rope = rope_tables is not None
    gates = w_gates is not None
    in_specs = [pl.BlockSpec((1, tm, d), lambda b, i: (b, i, 0)),
                pl.BlockSpec((1, 2, N_MOD, d), lambda b, i: (b, 0, 0, 0)),
                pl.BlockSpec((1, d), lambda b, i: (0, 0)),
                pl.BlockSpec((d, n_out), lambda b, i: (0, 0))]
    args = [h, mods, g[None, :], w_bf]
    out_specs = [pl.BlockSpec((1, tm, n_out), lambda b, i: (b, i, 0))]
    out_shape = [jax.ShapeDtypeStruct((bsz, nt, n_out), BF16)]
    if rope:
        in_specs += [pl.BlockSpec((tm, LANES), lambda b, i: (i, 0))] * 2
        args += list(rope_tables)
    if gates:
        in_specs.append(pl.BlockSpec((d, LANES), lambda b, i: (0, 0)))
        args.append(w_gates)
        out_specs.append(pl.BlockSpec((1, tm, LANES), lambda b, i: (b, i, 0)))
        out_shape.append(jax.ShapeDtypeStruct((bsz, nt, LANES), F32))
    out = pl.pallas_call(
        functools.partial(_proj_body, n_lat=n_lat, rope=rope, gates=gates),
        grid=(bsz, nt // tm),
        in_specs=in_specs,
        out_specs=out_specs,
        out_shape=out_shape,
        compiler_params=_cparams("arbitrary", "arbitrary"),
        name="mod_project",
    )(*args)
    return out if gates else out[0]


def _rope_tables(n_lat, n_ctx):
    rows = n_lat // GRID_W
    r = jnp.repeat(jnp.arange(rows), GRID_W).astype(F32)
    col = jnp.tile(jnp.arange(GRID_W), rows).astype(F32)
    half = DA_HEAD_DIM // 2
    inv = ROPE_BASE ** (-jnp.arange(0, half, 2, dtype=F32) / half)
    ang_r, ang_c = r[:, None] * inv, col[:, None] * inv
    cos = jnp.concatenate([jnp.cos(ang_r)] * 2 + [jnp.cos(ang_c)] * 2, axis=-1)
    sin = jnp.concatenate([-jnp.sin(ang_r), jnp.sin(ang_r), -jnp.sin(ang_c), jnp.sin(ang_c)], axis=-1)
    cos = jnp.concatenate([jnp.tile(cos, (1, 2)), jnp.ones((n_ctx, LANES), F32)], axis=0)
    sin = jnp.concatenate([jnp.tile(sin, (1, 2)), jnp.zeros((n_ctx, LANES), F32)], axis=0)
    return cos, sin


def _attn_body(lam_ref, q_ref, k_ref, v_ref, gs_ref, o_ref, *, lam_init):
    lv = lam_ref[...]
    lam = (jnp.exp(jnp.sum(lv[0:1] * lv[1:2], axis=-1, keepdims=True))
           - jnp.exp(jnp.sum(lv[2:3] * lv[3:4], axis=-1, keepdims=True)) + lam_init)
    q, k, v = q_ref[0], k_ref[0], v_ref[0]
    lane = lax.broadcasted_iota(jnp.int32, q.shape, 1)
    hw = v.shape[1]
    v_one = jnp.concatenate([v, jnp.ones_like(v)], axis=1)

    rows = min(q.shape[0], ATTN_CHAIN_ROWS)
    starts = range(0, q.shape[0], rows)
    qm = [jnp.where((lane >= c * DA_HEAD_DIM) & (lane < (c + 1) * DA_HEAD_DIM), q, jnp.zeros_like(q)) for c in range(2)]
    s = [_dot_nt(jnp.concatenate([qm[0][r:r + rows], qm[1][r:r + rows]], axis=0), k) for r in starts]
    e = [jnp.exp2((x - jnp.max(x, axis=-1, keepdims=True)).astype(BF16)) for x in s]
    ov = [_dot(x, v_one) for x in e]
    att = [x[:, :hw] / x[:, hw:hw + 1] for x in ov]
    for r, a in zip(starts, att):
        o = a[:rows] - lam * a[rows:]
        o_ref[0, r:r + rows, :] = (_rms(o, gs_ref[...]) * (1.0 - lam_init)).astype(o_ref.dtype)


def _diff_attention(qkv, lam_vecs, g_subln, n_lat, depth):
    bsz, nt, d3 = qkv.shape
    d = d3 // 3
    hw = 2 * DA_HEAD_DIM
    lam_init = 0.8 - 0.6 * math.exp(-0.3 * depth)
    n_ctx = nt - n_lat

    def call(n_q, tq, q_blk0, n_k, k_blk0):
        return pl.pallas_call(
            functools.partial(_attn_body, lam_init=lam_init),
            grid=(bsz, DA_HEADS, n_q // tq),
            in_specs=[pl.BlockSpec((4, DA_HEAD_DIM), lambda b, h, i: (0, 0)),
                      pl.BlockSpec((1, tq, hw), lambda b, h, i: (b, q_blk0 + i, h)),
                      pl.BlockSpec((1, n_k, hw), lambda b, h, i: (b, k_blk0, DA_HEADS + h)),
                      pl.BlockSpec((1, n_k, hw), lambda b, h, i: (b, k_blk0, 2 * DA_HEADS + h)),
                      pl.BlockSpec((1, hw), lambda b, h, i: (0, 0))],
            out_specs=pl.BlockSpec((1, tq, hw), lambda b, h, i: (b, i, h)),
            out_shape=jax.ShapeDtypeStruct((bsz, n_q, d), BF16),
            compiler_params=_cparams("arbitrary", "arbitrary", "arbitrary"),
            name="diff_attention",
        )(lam_vecs, qkv, qkv, qkv, g_subln[None, :])

    o_lat = call(n_lat, ATTN_Q_ROWS, 0, nt, 0)
    o_ctx = call(n_ctx, n_ctx, n_lat // n_ctx, n_ctx, n_lat // n_ctx)
    return jnp.concatenate([o_lat, o_ctx], axis=1)


def _residual_and_route(h, y, mod, g_post_ref, g_ffn_ref, wr_ref, out_ref, u_ref, aff_ref):
    h_new = h + mod[GT_M:GT_M + 1, :] * _rms(y, g_post_ref[...])
    out_ref[0] = h_new
    u = _rms(h_new, g_ffn_ref[...]) * (1.0 + mod[SC_F:SC_F + 1, :]) + mod[SH_F:SH_F + 1, :]
    u_hi = u.astype(BF16)
    u_ref[0] = u_hi
    u_lo = (u - u_hi.astype(F32)).astype(BF16)
    w = wr_ref[...]
    w_hi = w.astype(BF16)
    w_lo = (w - w_hi.astype(F32)).astype(BF16)
    logits = (_dot(u_hi, w_hi) + _dot(u_hi, w_lo) + _dot(u_lo, w_hi)).T[:N_EXPERTS]
    e = jnp.exp(logits - jnp.max(logits, axis=0, keepdims=True))
    aff_ref[0] = e / jnp.sum(e, axis=0, keepdims=True)


def _pad_lanes(w):
    return jnp.pad(w, ((0, 0), (0, LANES - w.shape[1])))


def _route_specs(bsz, nt, d):
    in_specs = [pl.BlockSpec((1, d), lambda b, i: (0, 0)),
                pl.BlockSpec((d, LANES), lambda b, i: (0, 0))]
    out_specs = [pl.BlockSpec((1, ROW_TILE, d), lambda b, i: (b, i, 0)),
                 pl.BlockSpec((1, ROW_TILE, d), lambda b, i: (b, i, 0)),
                 pl.BlockSpec((1, N_EXPERTS, ROW_TILE), lambda b, i: (b, 0, i))]
    out_shape = [jax.ShapeDtypeStruct((bsz, nt, d), F32),
                 jax.ShapeDtypeStruct((bsz, nt, d), BF16),
                 jax.ShapeDtypeStruct((bsz, N_EXPERTS, nt), F32)]
    return in_specs, out_specs, out_shape


def _oproj_body(o_ref, w_ref, h_ref, mod_ref, g_ref, g_ffn_ref, wr_ref, out_ref, u_ref, aff_ref):
    y = _dot(o_ref[0], w_ref[...])
    _residual_and_route(h_ref[0], y, mod_ref[0, 0], g_ref, g_ffn_ref, wr_ref, out_ref, u_ref, aff_ref)


def _out_project(o, w_bf, h, mods, g_post, g_ffn, w_router, n_lat):
    bsz, nt, d = h.shape
    lat_tiles = n_lat // ROW_TILE
    r_in, r_out, r_shape = _route_specs(bsz, nt, d)
    return pl.pallas_call(
        _oproj_body,
        grid=(bsz, nt // ROW_TILE),
        in_specs=[pl.BlockSpec((1, ROW_TILE, d), lambda b, i: (b, i, 0)),
                  pl.BlockSpec((d, d), lambda b, i: (0, 0)),
                  pl.BlockSpec((1, ROW_TILE, d), lambda b, i: (b, i, 0)),
                  pl.BlockSpec((1, 1, N_MOD, d), lambda b, i: (b, i // lat_tiles, 0, 0)),
                  pl.BlockSpec((1, d), lambda b, i: (0, 0))] + r_in,
        out_specs=r_out,
        out_shape=r_shape,
        compiler_params=_cparams("arbitrary", "arbitrary"),
        name="out_project",
    )(o, w_bf, h, mods, g_post[None, :], g_ffn[None, :], _pad_lanes(w_router))


def _lane_cumsum(x):
    n = x.shape[1]
    jj = lax.broadcasted_iota(jnp.int32, (ROW_TILE, ROW_TILE), 0)
    nn = lax.broadcasted_iota(jnp.int32, (ROW_TILE, ROW_TILE), 1)
    tri = (jj <= nn).astype(BF16)
    run = jnp.zeros((x.shape[0], 1), F32)
    parts = []
    for t in range(n // ROW_TILE):
        local = _dot(x[:, t * ROW_TILE:(t + 1) * ROW_TILE].astype(BF16), tri) + run
        parts.append(local)
        run = local[:, ROW_TILE - 1:ROW_TILE]
    return jnp.concatenate(parts, axis=1) if len(parts) > 1 else parts[0]


def _select_slots(aff, cap):
    bits = pltpu.bitcast(aff, jnp.int32)

    def step(t, lo):
        cand = lo | jnp.left_shift(jnp.int32(1), 30 - t)
        cnt = jnp.sum((bits >= cand).astype(F32), axis=1, keepdims=True)
        return jnp.where(cnt >= cap, cand, lo)

    thr = lax.fori_loop(0, 31, step, jnp.zeros((aff.shape[0], 1), jnp.int32))
    gt = bits > thr
    eq = bits == thr
    need = cap - jnp.sum(gt.astype(F32), axis=1, keepdims=True)
    eq_rank = _lane_cumsum(eq.astype(F32))
    sel = gt | (eq & (eq_rank <= need))
    sel_f = sel.astype(F32)
    slot = _lane_cumsum(sel_f) - 1.0
    return jnp.where(sel, slot, -1.0).astype(jnp.int32), sel_f


def _tile_table(sel_f, tile0):
    lane = lax.broadcasted_iota(jnp.int32, (sel_f.shape[0], LANES), 1)
    run = jnp.zeros((sel_f.shape[0], 1), F32)
    tab = jnp.zeros((sel_f.shape[0], LANES), F32)
    for t in range(sel_f.shape[1] // ROW_TILE):
        inside = jnp.sum(sel_f[:, t * ROW_TILE:(t + 1) * ROW_TILE], axis=1, keepdims=True)
        tab = jnp.where(lane == tile0 + t, run, tab)
        tab = jnp.where(lane == TABLE_COUNT + tile0 + t, inside, tab)
        run = run + inside
    return tab


def _select_body(aff_ref, pos_ref, tab_ref, *, n_lat, n_ctx):
    aff = aff_ref[0]
    pos, sel_f = _select_slots(aff[:, :n_lat], CAP_FACTOR * n_lat // N_EXPERTS)
    pos_ref[0, :, :n_lat] = pos
    tab = _tile_table(sel_f, 0)
    if n_ctx:
        pos, sel_f = _select_slots(aff[:, n_lat:], CAP_FACTOR * n_ctx // N_EXPERTS)
        pos_ref[0, :, n_lat:] = pos
        tab = tab + _tile_table(sel_f, n_lat // ROW_TILE)
    tab_ref[0] = tab.astype(jnp.int32)


def _select(aff, n_lat, n_ctx):
    bsz, e, nt = aff.shape
    n_tiles = nt // ROW_TILE
    pos, tab = pl.pallas_call(
        functools.partial(_select_body, n_lat=n_lat, n_ctx=n_ctx),
        grid=(bsz,),
        in_specs=[pl.BlockSpec((1, e, nt), lambda b: (b, 0, 0))],
        out_specs=[pl.BlockSpec((1, e, nt), lambda b: (b, 0, 0)),
                   pl.BlockSpec((1, e, LANES), lambda b: (b, 0, 0))],
        out_shape=[jax.ShapeDtypeStruct((bsz, e, nt), jnp.int32),
                   jax.ShapeDtypeStruct((bsz, e, LANES), jnp.int32)],
        compiler_params=_cparams("arbitrary"),
        name="moe_select",
    )(aff)
    first = tab[:, :, :n_tiles].reshape(-1)
    count = tab[:, :, TABLE_COUNT:TABLE_COUNT + n_tiles].reshape(-1)
    return pos, first, count


class _TileWindows:
    def __init__(self, first_ref, count_ref, pos_ref, n_lat, rows, lat_cap):
        b, tile = pl.program_id(0), pl.program_id(1)
        n_exp = pos_ref.shape[1]
        self.rows, self.n_exp = rows, n_exp
        base = jnp.where(tile * ROW_TILE >= n_lat, lat_cap, 0)
        pos = pos_ref[0]
        self.slot = jnp.where(pos >= 0, pos + base, -1)
        self.starts, self.rounds = [], 0
        for x in range(n_exp):
            at = (b * n_exp + x) * pl.num_programs(1) + tile
            first = first_ref[at]
            start = base + (first // SLOT_ALIGN) * SLOT_ALIGN
            self.starts.append(start)
            self.rounds = jnp.maximum(self.rounds, (base + first + count_ref[at] - start + SLOT_WINDOW - 1) // SLOT_WINDOW)
        self.ids0 = lax.broadcasted_iota(jnp.int32, (SLOT_WINDOW, ROW_TILE), 0)

    def window(self, x, r):
        want = self.starts[x] + r * SLOT_WINDOW
        s0 = jnp.minimum(want, self.rows - SLOT_WINDOW)
        ids = self.ids0 + s0
        hot = (self.slot[x:x + 1, :] == ids) & (ids >= want)
        return pl.multiple_of(x * self.rows + s0, SLOT_ALIGN), hot


def _as_bf16(mask):
    return jnp.where(mask, 1.0, 0.0).astype(BF16)


def _gather_body(first_ref, count_ref, pos_ref, aff_ref, u_ref, xs_ref, gate_ref, *, n_lat, rows, lat_cap):
    @pl.when(pl.program_id(1) == 0)
    def _():
        xs_ref[...] = jnp.zeros_like(xs_ref)
        gate_ref[...] = jnp.zeros_like(gate_ref)

    tw = _TileWindows(first_ref, count_ref, pos_ref, n_lat, rows, lat_cap)
    aff = aff_ref[0]
    u = u_ref[0]

    def one_round(r, carry):
        wins = [tw.window(x, r) for x in range(tw.n_exp)]
        part = _dot(jnp.concatenate([_as_bf16(hot) for _, hot in wins], axis=0), u)
        for x, (off, hot) in enumerate(wins):
            dst = pl.ds(off, SLOT_WINDOW)
            xs_ref[0, dst, :] = xs_ref[0, dst, :] + part[x * SLOT_WINDOW:(x + 1) * SLOT_WINDOW].astype(BF16)
            gate = jnp.sum(jnp.where(hot, aff[x:x + 1, :], 0.0), axis=1, keepdims=True)
            gate_ref[0, dst, :] = gate_ref[0, dst, :] + jnp.broadcast_to(gate, (SLOT_WINDOW, LANES))
        return carry

    lax.fori_loop(0, tw.rounds, one_round, 0)


def _slot_rows(n_lat, n_ctx):
    return CAP_FACTOR * n_lat // N_EXPERTS + CAP_FACTOR * n_ctx // N_EXPERTS


def _gather(pos, first, count, aff, u, n_lat, n_ctx):
    bsz, nt, d = u.shape
    e = pos.shape[1]
    rows = _slot_rows(n_lat, n_ctx)
    xs, gate = pl.pallas_call(
        functools.partial(_gather_body, n_lat=n_lat, rows=rows, lat_cap=CAP_FACTOR * n_lat // N_EXPERTS),
        grid_spec=pltpu.PrefetchScalarGridSpec(
            num_scalar_prefetch=2,
            grid=(bsz, nt // ROW_TILE),
            in_specs=[pl.BlockSpec((1, e, ROW_TILE), lambda b, i, *_: (b, 0, i)),
                      pl.BlockSpec((1, e, ROW_TILE), lambda b, i, *_: (b, 0, i)),
                      pl.BlockSpec((1, ROW_TILE, d), lambda b, i, *_: (b, i, 0))],
            out_specs=[pl.BlockSpec((1, e * rows, d), lambda b, i, *_: (b, 0, 0)),
                       pl.BlockSpec((1, e * rows, LANES), lambda b, i, *_: (b, 0, 0))]),
        out_shape=[jax.ShapeDtypeStruct((bsz, e * rows, d), BF16),
                   jax.ShapeDtypeStruct((bsz, e * rows, LANES), F32)],
        compiler_params=_cparams("arbitrary", "arbitrary"),
        name="moe_gather",
    )(first, count, pos, aff, u)
    return xs.reshape(bsz, e, rows, d), gate.reshape(bsz, e, rows, LANES)


def _expert_body(xs_ref, gate_ref, wg_hbm, wu_hbm, wd_hbm, y_ref, wg_scr, wu_scr, wd_scr,
                 wg_stage, wu_stage, wd_stage, sems):
    x, b = pl.program_id(0), pl.program_id(1)
    n_exp, n_chunks = pl.num_programs(0), pl.num_programs(1)
    rows_in, rows_mid = wg_stage.shape[0], wd_stage.shape[0]

    def chunk_copies(expert, chunk):
        return (pltpu.make_async_copy(wg_hbm.at[expert, pl.ds(chunk * rows_in, rows_in), :], wg_stage, sems.at[0]),
                pltpu.make_async_copy(wu_hbm.at[expert, pl.ds(chunk * rows_in, rows_in), :], wu_stage, sems.at[1]),
                pltpu.make_async_copy(wd_hbm.at[expert, pl.ds(chunk * rows_mid, rows_mid), :], wd_stage, sems.at[2]))

    def land(copy, chunk):
        at_in = pl.ds(pl.multiple_of(chunk * rows_in, SLOT_ALIGN), rows_in)
        at_mid = pl.ds(pl.multiple_of(chunk * rows_mid, SLOT_ALIGN), rows_mid)
        wg_scr[copy, at_in, :] = wg_stage[...].astype(BF16)
        wu_scr[copy, at_in, :] = wu_stage[...].astype(BF16)
        wd_scr[copy, at_mid, :] = wd_stage[...].astype(BF16)

    @pl.when((x == 0) & (b == 0))
    def _():
        def fetch(chunk, carry):
            copies = chunk_copies(0, chunk)
            for cp in copies:
                cp.start()
            for cp in copies:
                cp.wait()
            land(0, chunk)
            return carry

        lax.fori_loop(0, n_chunks, fetch, 0)

    def start_next(_, carry):
        for cp in chunk_copies(x + 1, b):
            cp.start()
        return carry

    lax.fori_loop(0, jnp.where(x + 1 < n_exp, 1, 0), start_next, 0)

    cur = x % 2
    xs = xs_ref[0, 0]
    acc = jnp.zeros((xs.shape[0], wd_scr.shape[2]), F32)
    for s in range(wg_scr.shape[2] // FF_CHUNK):
        cols = slice(s * FF_CHUNK, (s + 1) * FF_CHUNK)
        hid = _silu(_dot(xs, wg_scr[cur, :, cols])) * _dot(xs, wu_scr[cur, :, cols])
        acc = acc + _dot(hid.astype(BF16), wd_scr[cur, cols, :])
    y_ref[0, 0] = (acc * gate_ref[0, 0, :, 0:1]).astype(y_ref.dtype)

    @pl.when(x + 1 < n_exp)
    def _():
        for cp in chunk_copies(x + 1, b):
            cp.wait()
        land(1 - cur, b)


def _experts(xs, gate, wg, wu, wd):
    bsz, e, rows, d = xs.shape
    ff = wg.shape[2]
    return pl.pallas_call(
        _expert_body,
        grid=(e, bsz),
        in_specs=[pl.BlockSpec((1, 1, rows, d), lambda x, b: (b, x, 0, 0)),
                  pl.BlockSpec((1, 1, rows, LANES), lambda x, b: (b, x, 0, 0)),
                  pl.BlockSpec(memory_space=pl.ANY),
                  pl.BlockSpec(memory_space=pl.ANY),
                  pl.BlockSpec(memory_space=pl.ANY)],
        out_specs=pl.BlockSpec((1, 1, rows, d), lambda x, b: (b, x, 0, 0)),
        out_shape=jax.ShapeDtypeStruct((bsz, e, rows, d), BF16),
        scratch_shapes=[pltpu.VMEM((2, d, ff), BF16), pltpu.VMEM((2, d, ff), BF16), pltpu.VMEM((2, ff, d), BF16),
                        pltpu.VMEM((d // bsz, ff), F32), pltpu.VMEM((d // bsz, ff), F32),
                        pltpu.VMEM((ff // bsz, d), F32), pltpu.SemaphoreType.DMA((3,))],
        compiler_params=_cparams("arbitrary", "arbitrary"),
        name="moe_experts",
    )(xs, gate, wg, wu, wd)


def _combine_body(first_ref, count_ref, pos_ref, y_ref, h_ref, mod_ref, g_ref, out_ref, acc_scr,
                  *, n_lat, rows, lat_cap):
    tw = _TileWindows(first_ref, count_ref, pos_ref, n_lat, rows, lat_cap)
    acc_scr[...] = jnp.zeros_like(acc_scr)

    def one_round(r, carry):
        group = ROW_TILE // SLOT_WINDOW
        for g0 in range(0, tw.n_exp, group):
            wins = [tw.window(x, r) for x in range(g0, g0 + group)]
            hot = jnp.concatenate([_as_bf16(h) for _, h in wins], axis=0)
            rows_y = jnp.concatenate([y_ref[0, pl.ds(off, SLOT_WINDOW), :] for off, _ in wins], axis=0)
            acc_scr[...] += _dot_tn(hot, rows_y)
        return carry

    lax.fori_loop(0, tw.rounds, one_round, 0)
    out_ref[0] = h_ref[0] + mod_ref[0, 0, GT_F:GT_F + 1, :] * _rms(acc_scr[...], g_ref[...])


def _combine(pos, first, count, y, h, mods, g_post, n_lat, n_ctx):
    bsz, e, rows, d = y.shape
    nt = n_lat + n_ctx
    lat_tiles = n_lat // ROW_TILE
    return pl.pallas_call(
        functools.partial(_combine_body, n_lat=n_lat, rows=rows, lat_cap=CAP_FACTOR * n_lat // N_EXPERTS),
        grid_spec=pltpu.PrefetchScalarGridSpec(
            num_scalar_prefetch=2,
            grid=(bsz, nt // ROW_TILE),
            in_specs=[pl.BlockSpec((1, e, ROW_TILE), lambda b, i, *_: (b, 0, i)),
                      pl.BlockSpec((1, e * rows, d), lambda b, i, *_: (b, 0, 0)),
                      pl.BlockSpec((1, ROW_TILE, d), lambda b, i, *_: (b, i, 0)),
                      pl.BlockSpec((1, 1, N_MOD, d), lambda b, i, *_: (b, i // lat_tiles, 0, 0)),
                      pl.BlockSpec((1, d), lambda b, i, *_: (0, 0))],
            out_specs=pl.BlockSpec((1, ROW_TILE, d), lambda b, i, *_: (b, i, 0)),
            scratch_shapes=[pltpu.VMEM((ROW_TILE, d), F32)]),
        out_shape=jax.ShapeDtypeStruct((bsz, nt, d), F32),
        compiler_params=_cparams("arbitrary", "arbitrary"),
        name="moe_combine",
    )(first, count, pos, y.reshape(bsz, e * rows, d), h, mods, g_post[None, :])


def _moe(h, u, aff, mods, g_post, wg, wu, wd, n_lat, n_ctx):
    pos, first, count = _select(aff, n_lat, n_ctx)
    xs, gate = _gather(pos, first, count, aff, u, n_lat, n_ctx)
    y = _experts(xs, gate, wg, wu, wd)
    return _combine(pos, first, count, y, h, mods, g_post, n_lat, n_ctx)


def _conv_body(x_ref, w_ref, o_ref, pad_scr, *, n_lat):
    j = pl.program_id(1)
    nt = x_ref.shape[1]
    halo = CONV_HALO
    zeros = jnp.zeros((halo, LANES), F32)
    for lo, hi in ((0, n_lat), (n_lat, nt)):
        n = hi - lo
        if n == 0:
            continue
        pad_scr[0:halo, :] = zeros
        pad_scr[halo:halo + n, :] = x_ref[0, lo:hi, :].astype(F32)
        pad_scr[halo + n:2 * halo + n, :] = zeros
        acc = jnp.zeros((n, LANES), F32)
        for tap in range(CONV_K):
            start = halo + tap - CONV_K // 2
            acc = acc + pad_scr[start:start + n, :] * w_ref[tap:tap + 1, :]
        y = _silu(acc)
        unit = y * lax.rsqrt(jnp.sum(y * y, axis=-1, keepdims=True) + EPS)
        y = jnp.where(j < GDN_HEADS, unit * GDN_HEAD_DIM ** -0.5, jnp.where(j < 2 * GDN_HEADS, unit, y))
        o_ref[0, lo:hi, :] = y


def _short_conv(p, w_conv, n_lat):
    bsz, nt, _ = p.shape
    nblk = 3 * GDN_HEADS
    return pl.pallas_call(
        functools.partial(_conv_body, n_lat=n_lat),
        grid=(bsz, nblk),
        in_specs=[pl.BlockSpec((1, nt, LANES), lambda b, j: (b, 0, j)),
                  pl.BlockSpec((CONV_K, LANES), lambda b, j: (0, j))],
        out_specs=pl.BlockSpec((1, nt, LANES), lambda b, j: (b, 0, j)),
        out_shape=jax.ShapeDtypeStruct((bsz, nt, nblk * LANES), F32),
        scratch_shapes=[pltpu.VMEM((n_lat + 2 * CONV_HALO, LANES), F32)],
        compiler_params=_cparams("arbitrary", "arbitrary"),
        name="gdn_conv",
    )(p, w_conv)


def _gdn_body(q_ref, k_ref, v_ref, gate_ref, gate_t_ref, alog_ref, dt_ref, alog_t_ref, dt_t_ref,
              o_ref, s_scr, gct_scr, *, rev, n_lat_groups):
    hb = pl.program_id(1)
    step = pl.program_id(2)
    r_t = ROW_TILE
    c_sz = GDN_CHUNK
    goff = 2 * GDN_HEADS if rev else 0

    @pl.when(step == 0)
    def _():
        s_scr[...] = jnp.zeros_like(s_scr)

    ii = lax.broadcasted_iota(jnp.int32, (r_t, r_t), 0)
    jj = lax.broadcasted_iota(jnp.int32, (r_t, r_t), 1)
    same = (ii // c_sz) == (jj // c_sz)
    if rev:
        incl, strict = same & (jj >= ii), same & (jj > ii)
    else:
        incl, strict = same & (jj <= ii), same & (jj < ii)
    m_incl = incl.astype(BF16)
    m_same = same.astype(BF16)
    eye = (ii == jj).astype(F32)

    def pieces(x):
        hi = x.astype(BF16)
        rest = x - hi.astype(F32)
        mid = rest.astype(BF16)
        return hi, mid, (rest - mid.astype(F32)).astype(BF16)

    lane = lax.broadcasted_iota(jnp.int32, (1, LANES), 1)
    is_g = (lane >= goff) & (lane < goff + GDN_HEADS)
    raw = gate_ref[0]
    g_all = jnp.where(is_g, -jnp.exp(alog_ref[...]) * _softplus(raw + dt_ref[...]), 0.0)
    beta_all = _sigmoid(raw)
    g_parts = pieces(g_all)
    gc_all = sum(_dot(m_incl, x) for x in g_parts)
    gl_all = sum(_dot(m_same, x) for x in g_parts)
    g_t = -jnp.exp(alog_t_ref[...]) * _softplus(gate_t_ref[0] + dt_t_ref[...])
    gct_scr[...] = sum(_dot_nt(x, m_incl) for x in pieces(g_t))
    lane_f = lax.broadcasted_iota(jnp.int32, (r_t, LANES), 1)

    def packed(m):
        return sum(m[c * c_sz:(c + 1) * c_sz] for c in range(r_t // c_sz))

    def blockdiag(p):
        return jnp.where(same, jnp.tile(p, (r_t // c_sz, 1)), 0.0)

    def column(a, idx):
        return jnp.sum(jnp.where(lane_f == idx, a, 0.0), axis=1, keepdims=True)

    heads = range(GDN_HEADS_PER_STEP)
    n_chunks = r_t // c_sz
    n_levels = int(math.log2(c_sz)) - 1
    st = []
    for hh in heads:
        head = hb * GDN_HEADS_PER_STEP + hh
        cols = slice(hh * GDN_HEAD_DIM, (hh + 1) * GDN_HEAD_DIM)
        q, k, v = q_ref[0, :, cols], k_ref[0, :, cols], v_ref[0, :, cols]
        gc = column(gc_all, goff + head)
        gl = column(gl_all, goff + head)
        beta = column(beta_all, goff + GDN_HEADS + head)
        gc_row = gct_scr[pl.ds(goff + head, 1), :]
        decay = jnp.where(incl, jnp.exp(jnp.where(incl, gc - gc_row, 0.0)), 0.0)
        kb = k * beta
        k_bf = k.astype(BF16)
        a_mat = jnp.where(strict, _dot_nt(kb.astype(BF16), k_bf) * decay, 0.0)
        a_pk = packed(a_mat)
        st.append(dict(
            cols=cols, gl=gl, inv=packed(eye) - a_pk, pw=a_pk.astype(BF16), pw_bd=a_mat.astype(BF16),
            rhs=jnp.concatenate([v * beta, kb * jnp.exp(gc)], axis=1).astype(BF16),
            intra=(_dot_nt(q.astype(BF16), k_bf) * decay).astype(BF16),
            q_dec=(q * jnp.exp(gc)).astype(BF16),
            k_dec_t=(k * jnp.exp(gl - gc)).T.astype(BF16),
            s=s_scr[hh], outs=[None] * n_chunks))
    for t in st:
        sq = _dot(t["pw"], t["pw_bd"])
        t["pw"], t["pw_bd"] = sq.astype(BF16), blockdiag(sq).astype(BF16)
    for level in range(n_levels):
        last = level == n_levels - 1
        for t in st:
            inv_bf = t["inv"].astype(BF16)
            both = _dot(inv_bf if last else jnp.concatenate([inv_bf, t["pw"]], axis=0), t["pw_bd"])
            t["inv"] = t["inv"] + both[:c_sz]
            if not last:
                t["pw"], t["pw_bd"] = both[c_sz:].astype(BF16), blockdiag(both[c_sz:]).astype(BF16)
    for t in st:
        uw = _dot(blockdiag(t["inv"]).astype(BF16), t["rhs"])
        t["u"], t["w"] = uw[:, :GDN_HEAD_DIM], uw[:, GDN_HEAD_DIM:].astype(BF16)
    for c in (range(n_chunks - 1, -1, -1) if rev else range(n_chunks)):
        rows = slice(c * c_sz, (c + 1) * c_sz)
        for t in st:
            t["s_bf"] = t["s"].astype(BF16)
            t["v_new"] = (t["u"][rows] - _dot(t["w"][rows], t["s_bf"])).astype(BF16)
        for t in st:
            t["outs"][c] = _dot(t["q_dec"][rows], t["s_bf"]) + _dot(t["intra"][rows, rows], t["v_new"])
            t["s"] = (t["s"] * jnp.exp(t["gl"][c * c_sz:c * c_sz + 1, :])
                      + _dot(t["k_dec_t"][:, rows], t["v_new"]))
    for hh, t in enumerate(st):
        s_scr[hh] = t["s"]
        o_ref[0, :, t["cols"]] = jnp.concatenate(t["outs"], axis=0)


def _gdn_scan(qkv, gates, gates_t, a_log_row, dt_row, a_log_col, dt_col, n_lat, rev):
    bsz, nt, _ = qkv.shape
    hps = GDN_HEADS_PER_STEP
    hblocks = GDN_HEADS // hps
    n_groups = nt // ROW_TILE
    lat_groups = n_lat // ROW_TILE
    width = hps * GDN_HEAD_DIM

    def group(i):
        lat = (lat_groups - i) if rev else (i - 1)
        return jnp.where(i == 0, n_groups - 1, lat)

    return pl.pallas_call(
        functools.partial(_gdn_body, rev=rev, n_lat_groups=lat_groups),
        grid=(bsz, hblocks, n_groups),
        in_specs=[pl.BlockSpec((1, ROW_TILE, width), lambda b, h, i: (b, group(i), h)),
                  pl.BlockSpec((1, ROW_TILE, width), lambda b, h, i: (b, group(i), hblocks + h)),
                  pl.BlockSpec((1, ROW_TILE, width), lambda b, h, i: (b, group(i), 2 * hblocks + h)),
                  pl.BlockSpec((1, ROW_TILE, LANES), lambda b, h, i: (b, group(i), 0)),
                  pl.BlockSpec((1, 4 * GDN_HEADS, ROW_TILE), lambda b, h, i: (b, 0, group(i))),
                  pl.BlockSpec((1, LANES), lambda b, h, i: (0, 0)),
                  pl.BlockSpec((1, LANES), lambda b, h, i: (0, 0)),
                  pl.BlockSpec((4 * GDN_HEADS, 1), lambda b, h, i: (0, 0)),
                  pl.BlockSpec((4 * GDN_HEADS, 1), lambda b, h, i: (0, 0))],
        out_specs=pl.BlockSpec((1, ROW_TILE, width), lambda b, h, i: (b, group(i), h)),
        out_shape=jax.ShapeDtypeStruct((bsz, nt, GDN_HEADS * GDN_HEAD_DIM), F32),
        scratch_shapes=[pltpu.VMEM((hps, GDN_HEAD_DIM, GDN_HEAD_DIM), F32),
                        pltpu.VMEM((4 * GDN_HEADS, ROW_TILE), F32)],
        compiler_params=_cparams("arbitrary", "arbitrary", "arbitrary"),
        name="gdn_scan_bwd" if rev else "gdn_scan_fwd",
    )(qkv, qkv, qkv, gates, gates_t, a_log_row, dt_row, a_log_col, dt_col)


def _gdn_out_body(of_ref, ob_ref, z_ref, gn_ref, w_ref, h_ref, mod_ref, g_ref, g_ffn_ref, wr_ref,
                  out_ref, u_ref, aff_ref, a_scr):
    for hh in range(GDN_HEADS):
        cols = slice(hh * GDN_HEAD_DIM, (hh + 1) * GDN_HEAD_DIM)
        o = of_ref[0, :, cols] + ob_ref[0, :, cols]
        a_scr[:, cols] = (_rms(o, gn_ref[...]) * _silu(z_ref[0, :, cols].astype(F32))).astype(BF16)
    y = _dot(a_scr[...], w_ref[...])
    _residual_and_route(h_ref[0], y, mod_ref[0, 0], g_ref, g_ffn_ref, wr_ref, out_ref, u_ref, aff_ref)


def _gdn_out(o_f, o_b, p, g_onorm, w_bf, h, mods, g_post, g_ffn, w_router, n_lat):
    bsz, _, d = h.shape
    width = GDN_HEADS * GDN_HEAD_DIM
    r_in, r_out, r_shape = _route_specs(bsz, n_lat, d)
    return pl.pallas_call(
        _gdn_out_body,
        grid=(bsz, n_lat // ROW_TILE),
        in_specs=[pl.BlockSpec((1, ROW_TILE, width), lambda b, i: (b, i, 0)),
                  pl.BlockSpec((1, ROW_TILE, width), lambda b, i: (b, i, 0)),
                  pl.BlockSpec((1, ROW_TILE, width), lambda b, i: (b, i, 3)),
                  pl.BlockSpec((1, GDN_HEAD_DIM), lambda b, i: (0, 0)),
                  pl.BlockSpec((width, d), lambda b, i: (0, 0)),
                  pl.BlockSpec((1, ROW_TILE, d), lambda b, i: (b, i, 0)),
                  pl.BlockSpec((1, 1, N_MOD, d), lambda b, i: (b, 0, 0, 0)),
                  pl.BlockSpec((1, d), lambda b, i: (0, 0))] + r_in,
        out_specs=r_out,
        out_shape=r_shape,
        scratch_shapes=[pltpu.VMEM((ROW_TILE, width), BF16)],
        compiler_params=_cparams("arbitrary", "arbitrary"),
        name="gdn_out",
    )(o_f, o_b, p, g_onorm[None, :], w_bf, h, mods, g_post[None, :], g_ffn[None, :], _pad_lanes(w_router))


def _gate_vectors(a_log_f, dt_bias_f, a_log_b, dt_bias_b):
    z = jnp.zeros((GDN_HEADS,), F32)
    a = jnp.concatenate([a_log_f, z, a_log_b, z])
    d = jnp.concatenate([dt_bias_f, z, dt_bias_b, z])
    pad = jnp.zeros((LANES - 4 * GDN_HEADS,), F32)
    return (jnp.concatenate([a, pad])[None, :], jnp.concatenate([d, pad])[None, :], a[:, None], d[:, None])


def kernel(x, c, ctx, c_ctx, l0_w_ada, l0_b_ada, l0_g_pre_mix, l0_g_post_mix, l0_g_pre_ffn, l0_g_post_ffn, l0_w_qkv, l0_lambda_q1, l0_lambda_k1, l0_lambda_q2, l0_lambda_k2, l0_g_subln, l0_w_o, l0_w_router, l0_w_gate, l0_w_up, l0_w_down, l1_w_ada, l1_b_ada, l1_g_pre_mix, l1_g_post_mix, l1_g_pre_ffn, l1_g_post_ffn, l1_w_in, l1_w_conv, l1_a_log_f, l1_dt_bias_f, l1_a_log_b, l1_dt_bias_b, l1_g_onorm, l1_w_o, l1_w_router, l1_w_gate, l1_w_up, l1_w_down):
    n_lat, n_ctx = x.shape[1], ctx.shape[1]
    h = jnp.concatenate([x, ctx], axis=1)

    mods = _ada_mods(c, c_ctx, l0_w_ada, l0_b_ada)
    qkv = _project(h, mods, l0_g_pre_mix, l0_w_qkv.astype(BF16), n_lat, rope_tables=_rope_tables(n_lat, n_ctx))
    lam_vecs = jnp.stack([l0_lambda_q1, l0_lambda_k1, l0_lambda_q2, l0_lambda_k2])
    o = _diff_attention(qkv, lam_vecs, l0_g_subln, n_lat, depth=0)
    h, u, aff = _out_project(o, l0_w_o.astype(BF16), h, mods, l0_g_post_mix, l0_g_pre_ffn, l0_w_router, n_lat)
    h = _moe(h, u, aff, mods, l0_g_post_ffn, l0_w_gate, l0_w_up, l0_w_down, n_lat, n_ctx)

    mods = _ada_mods(c, c_ctx, l1_w_ada, l1_b_ada)
    width = GDN_HEADS * GDN_HEAD_DIM
    w_gates = jnp.pad(l1_w_in[:, 4 * width:], ((0, 0), (0, LANES - 4 * GDN_HEADS))).astype(BF16)
    p, gates = _project(h, mods, l1_g_pre_mix, l1_w_in[:, :4 * width].astype(BF16), n_lat, w_gates=w_gates)
    gates_t = jnp.swapaxes(gates[:, :, :4 * GDN_HEADS], 1, 2)
    qkv = _short_conv(p, l1_w_conv, n_lat)
    gv = _gate_vectors(l1_a_log_f, l1_dt_bias_f, l1_a_log_b, l1_dt_bias_b)
    o_f = _gdn_scan(qkv, gates, gates_t, *gv, n_lat, rev=False)
    o_b = _gdn_scan(qkv, gates, gates_t, *gv, n_lat, rev=True)
    hl, u, aff = _gdn_out(o_f, o_b, p, l1_g_onorm, l1_w_o.astype(BF16), h, mods, l1_g_post_mix, l1_g_pre_ffn,
                          l1_w_router, n_lat)
    return _moe(hl, u, aff, mods, l1_g_post_ffn, l1_w_gate, l1_w_up, l1_w_down, n_lat, 0)
```

```python
import functools
import math

import jax
import jax.numpy as jnp
from jax import lax
from jax.experimental import pallas as pl
from jax.experimental.pallas import tpu as pltpu

F32 = jnp.float32
BF16 = jnp.bfloat16
HIGHEST = lax.Precision.HIGHEST

EPS = 1e-6
N_MOD = 6
GRID_W = 64
ROPE_BASE = 10000.0
DA_HEADS = 8
DA_HEAD_DIM = 64
GDN_HEADS = 8
GDN_HEAD_DIM = 128
CONV_K = 5
CONV_HALO = 8
N_EXPERTS = 16
CAP_FACTOR = 2

LANES = 128
ROW_TILE = 256
PROJ_ROWS = 768
ATTN_Q_ROWS = 1024
ATTN_CHAIN_ROWS = 128
Q_PRESCALE = DA_HEAD_DIM ** -0.5 * math.log2(math.e)
GDN_CHUNK = 64
GDN_HEADS_PER_STEP = 8
FF_CHUNK = 512
SLOT_WINDOW = 64
SLOT_ALIGN = 16
TABLE_COUNT = 64
VMEM_LIMIT = 56 * 1024 * 1024

SH_M, SC_M, GT_M, SH_F, SC_F, GT_F = range(6)


def _cparams(*sem):
    return pltpu.CompilerParams(dimension_semantics=sem, vmem_limit_bytes=VMEM_LIMIT)


def _dot(a, b):
    return jnp.dot(a, b, preferred_element_type=F32)


def _dot_nt(a, b, precision=None):
    return lax.dot_general(a, b, (((1,), (1,)), ((), ())), preferred_element_type=F32, precision=precision)


def _dot_tn(a, b):
    return lax.dot_general(a, b, (((0,), (0,)), ((), ())), preferred_element_type=F32)


def _rms(x, g):
    return x * lax.rsqrt(jnp.mean(x * x, axis=-1, keepdims=True) + EPS) * g


def _sigmoid(x):
    return 1.0 / (1.0 + jnp.exp(-x))


def _silu(x):
    return x * _sigmoid(x)


def _softplus(x):
    return jnp.maximum(x, 0.0) + jnp.log(1.0 + jnp.exp(-jnp.abs(x)))


def _ada_body(c_ref, w_ref, b_ref, o_ref):
    o_ref[...] = jnp.dot(_silu(c_ref[...]), w_ref[...], precision=HIGHEST, preferred_element_type=F32) + b_ref[...]


def _ada_mods(c, c_ctx, w_ada, b_ada):
    bsz, d = c.shape
    rows = 16
    cc = jnp.concatenate([c, c_ctx[None, :], jnp.zeros((rows - bsz - 1, d), F32)], axis=0)
    tn = 1024
    m = pl.pallas_call(
        _ada_body,
        grid=(N_MOD * d // tn,),
        in_specs=[pl.BlockSpec((rows, d), lambda j: (0, 0)),
                  pl.BlockSpec((d, tn), lambda j: (0, j)),
                  pl.BlockSpec((1, tn), lambda j: (0, j))],
        out_specs=pl.BlockSpec((rows, tn), lambda j: (0, j)),
        out_shape=jax.ShapeDtypeStruct((rows, N_MOD * d), F32),
        compiler_params=_cparams("arbitrary"),
        name="ada_mods",
    )(cc, w_ada, b_ada[None, :])
    lat = m[:bsz].reshape(bsz, 1, N_MOD, d)
    ctx = jnp.broadcast_to(m[bsz].reshape(1, 1, N_MOD, d), (bsz, 1, N_MOD, d))
    return jnp.concatenate([lat, ctx], axis=1)


def _proj_body(h_ref, mod_ref, g_ref, w_ref, *rest, n_lat, rope, gates):
    rest = list(rest)
    cos_ref, sin_ref = (rest.pop(0), rest.pop(0)) if rope else (None, None)
    wg_ref = rest.pop(0) if gates else None
    o_ref = rest.pop(0)
    og_ref = rest.pop(0) if gates else None
    x = h_ref[0]
    tm, d = x.shape
    row = pl.program_id(1) * tm + lax.broadcasted_iota(jnp.int32, (tm, 1), 0)
    is_ctx = row >= n_lat
    scale = jnp.where(is_ctx, mod_ref[0, 1, SC_M:SC_M + 1, :], mod_ref[0, 0, SC_M:SC_M + 1, :])
    shift = jnp.where(is_ctx, mod_ref[0, 1, SH_M:SH_M + 1, :], mod_ref[0, 0, SH_M:SH_M + 1, :])
    u = (_rms(x, g_ref[...]) * (1.0 + scale) + shift).astype(BF16)
    if gates:
        og_ref[0] = _dot(u, wg_ref[...])
    for j in range(w_ref.shape[1] // d):
        cols = slice(j * d, (j + 1) * d)
        acc = _dot(u, w_ref[:, cols])
        if rope and j < 2:
            reps = d // LANES
            cos = jnp.tile(cos_ref[...], (1, reps))
            sin = jnp.tile(sin_ref[...], (1, reps))
            lane = lax.broadcasted_iota(jnp.int32, acc.shape, 1)
            half = DA_HEAD_DIM // 4
            first = (lane % (2 * half)) < half
            partner = jnp.where(first, pltpu.roll(acc, d - half, 1), pltpu.roll(acc, half, 1))
            acc = acc * cos + partner * sin
            if j == 0:
                acc = acc * Q_PRESCALE
        o_ref[0, :, cols] = acc.astype(o_ref.dtype)


def _project(h, mods, g, w_bf, n_lat, rope_tables=None, w_gates=None):
    bsz, nt, d = h.shape
    n_out = w_bf.shape[1]
    tm = PROJ_ROWS
    assert nt % tm == 0 and n_lat % ROW_TILE == 0 and (nt - n_lat) == ROW_TILE
    rope = rope_tables is not None
    gates = w_gates is not None
    in_specs = [pl.BlockSpec((1, tm, d), lambda b, i: (b, i, 0)),
                pl.BlockSpec((1, 2, N_MOD, d), lambda b, i: (b, 0, 0, 0)),
                pl.BlockSpec((1, d), lambda b, i: (0, 0)),
                pl.BlockSpec((d, n_out), lambda b, i: (0, 0))]
    args = [h, mods, g[None, :], w_bf]
    out_specs = [pl.BlockSpec((1, tm, n_out), lambda b, i: (b, i, 0))]
    out_shape = [jax.ShapeDtypeStruct((bsz, nt, n_out), BF16)]
    if rope:
        in_specs += [pl.BlockSpec((tm, LANES), lambda b, i: (i, 0))] * 2
        args += list(rope_tables)
    if gates:
        in_specs.append(pl.BlockSpec((d, LANES), lambda b, i: (0, 0)))
        args.append(w_gates)
        out_specs.append(pl.BlockSpec((1, tm, LANES), lambda b, i: (b, i, 0)))
        out_shape.append(jax.ShapeDtypeStruct((bsz, nt, LANES), F32))
    out = pl.pallas_call(
        functools.partial(_proj_body, n_lat=n_lat, rope=rope, gates=gates),
        grid=(bsz, nt // tm),
        in_specs=in_specs,
        out_specs=out_specs,
        out_shape=out_shape,
        compiler_params=_cparams("arbitrary", "arbitrary"),
        name="mod_project",
    )(*args)
    return out if gates else out[0]


def _rope_tables(n_lat, n_ctx):
    rows = n_lat // GRID_W
    r = jnp.repeat(jnp.arange(rows), GRID_W).astype(F32)
    col = jnp.tile(jnp.arange(GRID_W), rows).astype(F32)
    half = DA_HEAD_DIM // 2
    inv = ROPE_BASE ** (-jnp.arange(0, half, 2, dtype=F32) / half)
    ang_r, ang_c = r[:, None] * inv, col[:, None] * inv
    cos = jnp.concatenate([jnp.cos(ang_r)] * 2 + [jnp.cos(ang_c)] * 2, axis=-1)
    sin = jnp.concatenate([-jnp.sin(ang_r), jnp.sin(ang_r), -jnp.sin(ang_c), jnp.sin(ang_c)], axis=-1)
    cos = jnp.concatenate([jnp.tile(cos, (1, 2)), jnp.ones((n_ctx, LANES), F32)], axis=0)
    sin = jnp.concatenate([jnp.tile(sin, (1, 2)), jnp.zeros((n_ctx, LANES), F32)], axis=0)
    return cos, sin


def _attn_body(lam_ref, q_ref, k_ref, v_ref, gs_ref, o_ref, *, lam_init):
    lv = lam_ref[...]
    lam = (jnp.exp(jnp.sum(lv[0:1] * lv[1:2], axis=-1, keepdims=True))
           - jnp.exp(jnp.sum(lv[2:3] * lv[3:4], axis=-1, keepdims=True)) + lam_init)
    hw = 2 * DA_HEAD_DIM
    n_q = q_ref.shape[1]
    lane = lax.broadcasted_iota(jnp.int32, (n_q, hw), 1)
    rows = min(n_q, ATTN_CHAIN_ROWS)
    chains = [(hh, r) for hh in range(q_ref.shape[2] // hw) for r in range(0, n_q, rows)]
    lhs, v_one = {}, {}
    for hh in range(q_ref.shape[2] // hw):
        cols = slice(hh * hw, (hh + 1) * hw)
        q, v = q_ref[0, :, cols], v_ref[0, :, cols]
        v_one[hh] = jnp.concatenate([v, jnp.ones_like(v)], axis=1)
        qm = [jnp.where((lane >= c * DA_HEAD_DIM) & (lane < (c + 1) * DA_HEAD_DIM), q, jnp.zeros_like(q))
              for c in range(2)]
        for r in range(0, n_q, rows):
            lhs[hh, r] = jnp.concatenate([qm[0][r:r + rows], qm[1][r:r + rows]], axis=0)
    s = [_dot_nt(lhs[hh, r], k_ref[0, :, hh * hw:(hh + 1) * hw]) for hh, r in chains]
    e = [jnp.exp2((x - jnp.max(x, axis=-1, keepdims=True)).astype(BF16)) for x in s]
    ov = [_dot(x, v_one[hh]) for x, (hh, r) in zip(e, chains)]
    att = [x[:, :hw] / x[:, hw:hw + 1] for x in ov]
    for (hh, r), a in zip(chains, att):
        o = a[:rows] - lam * a[rows:]
        o_ref[0, r:r + rows, hh * hw:(hh + 1) * hw] = (_rms(o, gs_ref[...]) * (1.0 - lam_init)).astype(o_ref.dtype)


def _diff_attention(qkv, lam_vecs, g_subln, n_lat, depth):
    bsz, nt, d3 = qkv.shape
    d = d3 // 3
    hw = 2 * DA_HEAD_DIM
    lam_init = 0.8 - 0.6 * math.exp(-0.3 * depth)
    n_ctx = nt - n_lat

    def call(n_q, tq, q_blk0, n_k, k_blk0, heads):
        hblocks = DA_HEADS // heads
        width = heads * hw
        return pl.pallas_call(
            functools.partial(_attn_body, lam_init=lam_init),
            grid=(bsz, hblocks, n_q // tq),
            in_specs=[pl.BlockSpec((4, DA_HEAD_DIM), lambda b, h, i: (0, 0)),
                      pl.BlockSpec((1, tq, width), lambda b, h, i: (b, q_blk0 + i, h)),
                      pl.BlockSpec((1, n_k, width), lambda b, h, i: (b, k_blk0, hblocks + h)),
                      pl.BlockSpec((1, n_k, width), lambda b, h, i: (b, k_blk0, 2 * hblocks + h)),
                      pl.BlockSpec((1, hw), lambda b, h, i: (0, 0))],
            out_specs=pl.BlockSpec((1, tq, width), lambda b, h, i: (b, i, h)),
            out_shape=jax.ShapeDtypeStruct((bsz, n_q, d), BF16),
            compiler_params=_cparams("arbitrary", "arbitrary", "arbitrary"),
            name="diff_attention",
        )(lam_vecs, qkv, qkv, qkv, g_subln[None, :])

    o_lat = call(n_lat, min(ATTN_Q_ROWS, n_lat), 0, nt, 0, heads=1)
    o_ctx = call(n_ctx, n_ctx, n_lat // n_ctx, n_ctx, n_lat // n_ctx, heads=DA_HEADS)
    return jnp.concatenate([o_lat, o_ctx], axis=1)


def _residual_and_route(h, y, mod, g_post_ref, g_ffn_ref, wr_ref, out_ref, u_ref, aff_ref):
    h_new = h + mod[GT_M:GT_M + 1, :] * _rms(y, g_post_ref[...])
    out_ref[0] = h_new
    u = _rms(h_new, g_ffn_ref[...]) * (1.0 + mod[SC_F:SC_F + 1, :]) + mod[SH_F:SH_F + 1, :]
    u_hi = u.astype(BF16)
    u_ref[0] = u_hi
    u_lo = (u - u_hi.astype(F32)).astype(BF16)
    w = wr_ref[...]
    w_hi = w.astype(BF16)
    w_lo = (w - w_hi.astype(F32)).astype(BF16)
    logits = (_dot(u_hi, w_hi) + _dot(u_hi, w_lo) + _dot(u_lo, w_hi)).T[:N_EXPERTS]
    e = jnp.exp(logits - jnp.max(logits, axis=0, keepdims=True))
    aff_ref[0] = e / jnp.sum(e, axis=0, keepdims=True)


def _pad_lanes(w):
    return jnp.pad(w, ((0, 0), (0, LANES - w.shape[1])))


def _route_specs(bsz, nt, d):
    in_specs = [pl.BlockSpec((1, d), lambda b, i: (0, 0)),
                pl.BlockSpec((d, LANES), lambda b, i: (0, 0))]
    out_specs = [pl.BlockSpec((1, ROW_TILE, d), lambda b, i: (b, i, 0)),
                 pl.BlockSpec((1, ROW_TILE, d), lambda b, i: (b, i, 0)),
                 pl.BlockSpec((1, N_EXPERTS, ROW_TILE), lambda b, i: (b, 0, i))]
    out_shape = [jax.ShapeDtypeStruct((bsz, nt, d), F32),
                 jax.ShapeDtypeStruct((bsz, nt, d), BF16),
                 jax.ShapeDtypeStruct((bsz, N_EXPERTS, nt), F32)]
    return in_specs, out_specs, out_shape


def _oproj_body(o_ref, w_ref, h_ref, mod_ref, g_ref, g_ffn_ref, wr_ref, out_ref, u_ref, aff_ref):
    y = _dot(o_ref[0], w_ref[...])
    _residual_and_route(h_ref[0], y, mod_ref[0, 0], g_ref, g_ffn_ref, wr_ref, out_ref, u_ref, aff_ref)


def _out_project(o, w_bf, h, mods, g_post, g_ffn, w_router, n_lat):
    bsz, nt, d = h.shape
    lat_tiles = n_lat // ROW_TILE
    r_in, r_out, r_shape = _route_specs(bsz, nt, d)
    return pl.pallas_call(
        _oproj_body,
        grid=(bsz, nt // ROW_TILE),
        in_specs=[pl.BlockSpec((1, ROW_TILE, d), lambda b, i: (b, i, 0)),
                  pl.BlockSpec((d, d), lambda b, i: (0, 0)),
                  pl.BlockSpec((1, ROW_TILE, d), lambda b, i: (b, i, 0)),
                  pl.BlockSpec((1, 1, N_MOD, d), lambda b, i: (b, i // lat_tiles, 0, 0)),
                  pl.BlockSpec((1, d), lambda b, i: (0, 0))] + r_in,
        out_specs=r_out,
        out_shape=r_shape,
        compiler_params=_cparams("arbitrary", "arbitrary"),
        name="out_project",
    )(o, w_bf, h, mods, g_post[None, :], g_ffn[None, :], _pad_lanes(w_router))


def _lane_cumsum(x):
    n = x.shape[1]
    jj = lax.broadcasted_iota(jnp.int32, (ROW_TILE, ROW_TILE), 0)
    nn = lax.broadcasted_iota(jnp.int32, (ROW_TILE, ROW_TILE), 1)
    tri = (jj <= nn).astype(BF16)
    run = jnp.zeros((x.shape[0], 1), F32)
    parts = []
    for t in range(n // ROW_TILE):
        local = _dot(x[:, t * ROW_TILE:(t + 1) * ROW_TILE].astype(BF16), tri) + run
        parts.append(local)
        run = local[:, ROW_TILE - 1:ROW_TILE]
    return jnp.concatenate(parts, axis=1) if len(parts) > 1 else parts[0]


def _select_slots(aff, cap):
    bits = pltpu.bitcast(aff, jnp.int32)

    def step(t, lo):
        cand = lo | jnp.left_shift(jnp.int32(1), 30 - t)
        cnt = jnp.sum((bits >= cand).astype(F32), axis=1, keepdims=True)
        return jnp.where(cnt >= cap, cand, lo)

    thr = lax.fori_loop(0, 31, step, jnp.zeros((aff.shape[0], 1), jnp.int32))
    gt = bits > thr
    eq = bits == thr
    need = cap - jnp.sum(gt.astype(F32), axis=1, keepdims=True)
    eq_rank = _lane_cumsum(eq.astype(F32))
    sel = gt | (eq & (eq_rank <= need))
    sel_f = sel.astype(F32)
    slot = _lane_cumsum(sel_f) - 1.0
    return jnp.where(sel, slot, -1.0).astype(jnp.int32), sel_f


def _tile_table(sel_f, tile0):
    lane = lax.broadcasted_iota(jnp.int32, (sel_f.shape[0], LANES), 1)
    run = jnp.zeros((sel_f.shape[0], 1), F32)
    tab = jnp.zeros((sel_f.shape[0], LANES), F32)
    for t in range(sel_f.shape[1] // ROW_TILE):
        inside = jnp.sum(sel_f[:, t * ROW_TILE:(t + 1) * ROW_TILE], axis=1, keepdims=True)
        tab = jnp.where(lane == tile0 + t, run, tab)
        tab = jnp.where(lane == TABLE_COUNT + tile0 + t, inside, tab)
        run = run + inside
    return tab


def _select_body(aff_ref, pos_ref, tab_ref, *, n_lat, n_ctx):
    aff = aff_ref[0]
    pos, sel_f = _select_slots(aff[:, :n_lat], CAP_FACTOR * n_lat // N_EXPERTS)
    pos_ref[0, :, :n_lat] = pos
    tab = _tile_table(sel_f, 0)
    if n_ctx:
        pos, sel_f = _select_slots(aff[:, n_lat:], CAP_FACTOR * n_ctx // N_EXPERTS)
        pos_ref[0, :, n_lat:] = pos
        tab = tab + _tile_table(sel_f, n_lat // ROW_TILE)
    tab_ref[0] = tab.astype(jnp.int32)


def _select(aff, n_lat, n_ctx):
    bsz, e, nt = aff.shape
    n_tiles = nt // ROW_TILE
    pos, tab = pl.pallas_call(
        functools.partial(_select_body, n_lat=n_lat, n_ctx=n_ctx),
        grid=(bsz,),
        in_specs=[pl.BlockSpec((1, e, nt), lambda b: (b, 0, 0))],
        out_specs=[pl.BlockSpec((1, e, nt), lambda b: (b, 0, 0)),
                   pl.BlockSpec((1, e, LANES), lambda b: (b, 0, 0))],
        out_shape=[jax.ShapeDtypeStruct((bsz, e, nt), jnp.int32),
                   jax.ShapeDtypeStruct((bsz, e, LANES), jnp.int32)],
        compiler_params=_cparams("arbitrary"),
        name="moe_select",
    )(aff)
    first = tab[:, :, :n_tiles].reshape(-1)
    count = tab[:, :, TABLE_COUNT:TABLE_COUNT + n_tiles].reshape(-1)
    return pos, first, count


class _TileWindows:
    def __init__(self, first_ref, count_ref, pos_ref, n_lat, rows, lat_cap):
        b, tile = pl.program_id(0), pl.program_id(1)
        n_exp = pos_ref.shape[1]
        self.rows, self.n_exp = rows, n_exp
        base = jnp.where(tile * ROW_TILE >= n_lat, lat_cap, 0)
        pos = pos_ref[0]
        self.slot = jnp.where(pos >= 0, pos + base, -1)
        self.starts, self.rounds = [], 0
        for x in range(n_exp):
            at = (b * n_exp + x) * pl.num_programs(1) + tile
            first = first_ref[at]
            start = base + (first // SLOT_ALIGN) * SLOT_ALIGN
            self.starts.append(start)
            self.rounds = jnp.maximum(self.rounds, (base + first + count_ref[at] - start + SLOT_WINDOW - 1) // SLOT_WINDOW)
        self.ids0 = lax.broadcasted_iota(jnp.int32, (SLOT_WINDOW, ROW_TILE), 0)

    def window(self, x, r):
        want = self.starts[x] + r * SLOT_WINDOW
        s0 = jnp.minimum(want, self.rows - SLOT_WINDOW)
        ids = self.ids0 + s0
        hot = (self.slot[x:x + 1, :] == ids) & (ids >= want)
        return pl.multiple_of(x * self.rows + s0, SLOT_ALIGN), hot


def _as_bf16(mask):
    return jnp.where(mask, 1.0, 0.0).astype(BF16)


def _gather_body(first_ref, count_ref, pos_ref, aff_ref, u_ref, xs_ref, gate_ref, *, n_lat, rows, lat_cap):
    @pl.when(pl.program_id(1) == 0)
    def _():
        xs_ref[...] = jnp.zeros_like(xs_ref)
        gate_ref[...] = jnp.zeros_like(gate_ref)

    tw = _TileWindows(first_ref, count_ref, pos_ref, n_lat, rows, lat_cap)
    aff = aff_ref[0]
    u = u_ref[0]

    def one_round(r, carry):
        wins = [tw.window(x, r) for x in range(tw.n_exp)]
        part = _dot(jnp.concatenate([_as_bf16(hot) for _, hot in wins], axis=0), u)
        for x, (off, hot) in enumerate(wins):
            dst = pl.ds(off, SLOT_WINDOW)
            xs_ref[0, dst, :] = xs_ref[0, dst, :] + part[x * SLOT_WINDOW:(x + 1) * SLOT_WINDOW].astype(BF16)
            gate = jnp.sum(jnp.where(hot, aff[x:x + 1, :], 0.0), axis=1, keepdims=True)
            gate_ref[0, dst, :] = gate_ref[0, dst, :] + jnp.broadcast_to(gate, (SLOT_WINDOW, LANES))
        return carry

    lax.fori_loop(0, tw.rounds, one_round, 0)


def _slot_rows(n_lat, n_ctx):
    return CAP_FACTOR * n_lat // N_EXPERTS + CAP_FACTOR * n_ctx // N_EXPERTS


def _gather(pos, first, count, aff, u, n_lat, n_ctx):
    bsz, nt, d = u.shape
    e = pos.shape[1]
    rows = _slot_rows(n_lat, n_ctx)
    xs, gate = pl.pallas_call(
        functools.partial(_gather_body, n_lat=n_lat, rows=rows, lat_cap=CAP_FACTOR * n_lat // N_EXPERTS),
        grid_spec=pltpu.PrefetchScalarGridSpec(
            num_scalar_prefetch=2,
            grid=(bsz, nt // ROW_TILE),
            in_specs=[pl.BlockSpec((1, e, ROW_TILE), lambda b, i, *_: (b, 0, i)),
                      pl.BlockSpec((1, e, ROW_TILE), lambda b, i, *_: (b, 0, i)),
                      pl.BlockSpec((1, ROW_TILE, d), lambda b, i, *_: (b, i, 0))],
            out_specs=[pl.BlockSpec((1, e * rows, d), lambda b, i, *_: (b, 0, 0)),
                       pl.BlockSpec((1, e * rows, LANES), lambda b, i, *_: (b, 0, 0))]),
        out_shape=[jax.ShapeDtypeStruct((bsz, e * rows, d), BF16),
                   jax.ShapeDtypeStruct((bsz, e * rows, LANES), F32)],
        compiler_params=_cparams("arbitrary", "arbitrary"),
        name="moe_gather",
    )(first, count, pos, aff, u)
    return xs.reshape(bsz, e, rows, d), gate.reshape(bsz, e, rows, LANES)


def _expert_body(xs_ref, gate_ref, wg_hbm, wu_hbm, wd_hbm, y_ref, wg_scr, wu_scr, wd_scr,
                 wg_stage, wu_stage, wd_stage, sems):
    x, b = pl.program_id(0), pl.program_id(1)
    n_exp, n_chunks = pl.num_programs(0), pl.num_programs(1)
    rows_in, rows_mid = wg_stage.shape[0], wd_stage.shape[0]

    def chunk_copies(expert, chunk):
        return (pltpu.make_async_copy(wg_hbm.at[expert, pl.ds(chunk * rows_in, rows_in), :], wg_stage, sems.at[0]),
                pltpu.make_async_copy(wu_hbm.at[expert, pl.ds(chunk * rows_in, rows_in), :], wu_stage, sems.at[1]),
                pltpu.make_async_copy(wd_hbm.at[expert, pl.ds(chunk * rows_mid, rows_mid), :], wd_stage, sems.at[2]))

    def land(copy, chunk):
        at_in = pl.ds(pl.multiple_of(chunk * rows_in, SLOT_ALIGN), rows_in)
        at_mid = pl.ds(pl.multiple_of(chunk * rows_mid, SLOT_ALIGN), rows_mid)
        wg_scr[copy, at_in, :] = wg_stage[...].astype(BF16)
        wu_scr[copy, at_in, :] = wu_stage[...].astype(BF16)
        wd_scr[copy, at_mid, :] = wd_stage[...].astype(BF16)

    @pl.when((x == 0) & (b == 0))
    def _():
        def fetch(chunk, carry):
            copies = chunk_copies(0, chunk)
            for cp in copies:
                cp.start()
            for cp in copies:
                cp.wait()
            land(0, chunk)
            return carry

        lax.fori_loop(0, n_chunks, fetch, 0)

    def start_next(_, carry):
        for cp in chunk_copies(x + 1, b):
            cp.start()
        return carry

    lax.fori_loop(0, jnp.where(x + 1 < n_exp, 1, 0), start_next, 0)

    cur = x % 2
    xs = xs_ref[0, 0]
    acc = jnp.zeros((xs.shape[0], wd_scr.shape[2]), F32)
    for s in range(wg_scr.shape[2] // FF_CHUNK):
        cols = slice(s * FF_CHUNK, (s + 1) * FF_CHUNK)
        hid = _silu(_dot(xs, wg_scr[cur, :, cols])) * _dot(xs, wu_scr[cur, :, cols])
        acc = acc + _dot(hid.astype(BF16), wd_scr[cur, cols, :])
    y_ref[0, 0] = (acc * gate_ref[0, 0, :, 0:1]).astype(y_ref.dtype)

    @pl.when(x + 1 < n_exp)
    def _():
        for cp in chunk_copies(x + 1, b):
            cp.wait()
        land(1 - cur, b)


def _experts(xs, gate, wg, wu, wd):
    bsz, e, rows, d = xs.shape
    ff = wg.shape[2]
    return pl.pallas_call(
        _expert_body,
        grid=(e, bsz),
        in_specs=[pl.BlockSpec((1, 1, rows, d), lambda x, b: (b, x, 0, 0)),
                  pl.BlockSpec((1, 1, rows, LANES), lambda x, b: (b, x, 0, 0)),
                  pl.BlockSpec(memory_space=pl.ANY),
                  pl.BlockSpec(memory_space=pl.ANY),
                  pl.BlockSpec(memory_space=pl.ANY)],
        out_specs=pl.BlockSpec((1, 1, rows, d), lambda x, b: (b, x, 0, 0)),
        out_shape=jax.ShapeDtypeStruct((bsz, e, rows, d), BF16),
        scratch_shapes=[pltpu.VMEM((2, d, ff), BF16), pltpu.VMEM((2, d, ff), BF16), pltpu.VMEM((2, ff, d), BF16),
                        pltpu.VMEM((d // bsz, ff), F32), pltpu.VMEM((d // bsz, ff), F32),
                        pltpu.VMEM((ff // bsz, d), F32), pltpu.SemaphoreType.DMA((3,))],
        compiler_params=_cparams("arbitrary", "arbitrary"),
        name="moe_experts",
    )(xs, gate, wg, wu, wd)


def _combine_body(first_ref, count_ref, pos_ref, y_ref, h_ref, mod_ref, g_ref, out_ref, acc_scr,
                  *, n_lat, rows, lat_cap):
    tw = _TileWindows(first_ref, count_ref, pos_ref, n_lat, rows, lat_cap)
    acc_scr[...] = jnp.zeros_like(acc_scr)

    def one_round(r, carry):
        group = ROW_TILE // SLOT_WINDOW
        for g0 in range(0, tw.n_exp, group):
            wins = [tw.window(x, r) for x in range(g0, g0 + group)]
            hot = jnp.concatenate([_as_bf16(h) for _, h in wins], axis=0)
            rows_y = jnp.concatenate([y_ref[0, pl.ds(off, SLOT_WINDOW), :] for off, _ in wins], axis=0)
            acc_scr[...] += _dot_tn(hot, rows_y)
        return carry

    lax.fori_loop(0, tw.rounds, one_round, 0)
    out_ref[0] = h_ref[0] + mod_ref[0, 0, GT_F:GT_F + 1, :] * _rms(acc_scr[...], g_ref[...])


def _combine(pos, first, count, y, h, mods, g_post, n_lat, n_ctx):
    bsz, e, rows, d = y.shape
    nt = n_lat + n_ctx
    lat_tiles = n_lat // ROW_TILE
    return pl.pallas_call(
        functools.partial(_combine_body, n_lat=n_lat, rows=rows, lat_cap=CAP_FACTOR * n_lat // N_EXPERTS),
        grid_spec=pltpu.PrefetchScalarGridSpec(
            num_scalar_prefetch=2,
            grid=(bsz, nt // ROW_TILE),
            in_specs=[pl.BlockSpec((1, e, ROW_TILE), lambda b, i, *_: (b, 0, i)),
                      pl.BlockSpec((1, e * rows, d), lambda b, i, *_: (b, 0, 0)),
                      pl.BlockSpec((1, ROW_TILE, d), lambda b, i, *_: (b, i, 0)),
                      pl.BlockSpec((1, 1, N_MOD, d), lambda b, i, *_: (b, i // lat_tiles, 0, 0)),
                      pl.BlockSpec((1, d), lambda b, i, *_: (0, 0))],
            out_specs=pl.BlockSpec((1, ROW_TILE, d), lambda b, i, *_: (b, i, 0)),
            scratch_shapes=[pltpu.VMEM((ROW_TILE, d), F32)]),
        out_shape=jax.ShapeDtypeStruct((bsz, nt, d), F32),
        compiler_params=_cparams("arbitrary", "arbitrary"),
        name="moe_combine",
    )(first, count, pos, y.reshape(bsz, e * rows, d), h, mods, g_post[None, :])


def _moe(h, u, aff, mods, g_post, wg, wu, wd, n_lat, n_ctx):
    pos, first, count = _select(aff, n_lat, n_ctx)
    xs, gate = _gather(pos, first, count, aff, u, n_lat, n_ctx)
    y = _experts(xs, gate, wg, wu, wd)
    return _combine(pos, first, count, y, h, mods, g_post, n_lat, n_ctx)


def _conv_body(x_ref, w_ref, o_ref, pad_scr, *, n_lat):
    j = pl.program_id(1)
    nt = x_ref.shape[1]
    halo = CONV_HALO
    zeros = jnp.zeros((halo, LANES), F32)
    for lo, hi in ((0, n_lat), (n_lat, nt)):
        n = hi - lo
        if n == 0:
            continue
        pad_scr[0:halo, :] = zeros
        pad_scr[halo:halo + n, :] = x_ref[0, lo:hi, :].astype(F32)
        pad_scr[halo + n:2 * halo + n, :] = zeros
        acc = None
        for tap in range(CONV_K):
            start = halo + tap - CONV_K // 2
            term = pad_scr[start:start + n, :] * w_ref[tap:tap + 1, :]
            acc = term if acc is None else acc + term
        y = _silu(acc)
        inv_norm = lax.rsqrt(jnp.sum(y * y, axis=-1, keepdims=True) + EPS)
        factor = jnp.where(j < GDN_HEADS, inv_norm * GDN_HEAD_DIM ** -0.5,
                           jnp.where(j < 2 * GDN_HEADS, inv_norm, 1.0))
        o_ref[0, lo:hi, :] = y * factor


def _short_conv(p, w_conv, n_lat):
    bsz, nt, _ = p.shape
    nblk = 3 * GDN_HEADS
    return pl.pallas_call(
        functools.partial(_conv_body, n_lat=n_lat),
        grid=(bsz, nblk),
        in_specs=[pl.BlockSpec((1, nt, LANES), lambda b, j: (b, 0, j)),
                  pl.BlockSpec((CONV_K, LANES), lambda b, j: (0, j))],
        out_specs=pl.BlockSpec((1, nt, LANES), lambda b, j: (b, 0, j)),
        out_shape=jax.ShapeDtypeStruct((bsz, nt, nblk * LANES), F32),
        scratch_shapes=[pltpu.VMEM((n_lat + 2 * CONV_HALO, LANES), F32)],
        compiler_params=_cparams("arbitrary", "arbitrary"),
        name="gdn_conv",
    )(p, w_conv)


def _gdn_body(q_ref, k_ref, v_ref, gate_ref, gate_t_ref, alog_ref, dt_ref, alog_t_ref, dt_t_ref,
              o_ref, s_scr, gct_scr, *, rev, n_lat_groups):
    hb = pl.program_id(1)
    step = pl.program_id(2)
    r_t = ROW_TILE
    c_sz = GDN_CHUNK
    goff = 2 * GDN_HEADS if rev else 0

    @pl.when(step == 0)
    def _():
        s_scr[...] = jnp.zeros_like(s_scr)

    ii = lax.broadcasted_iota(jnp.int32, (r_t, r_t), 0)
    jj = lax.broadcasted_iota(jnp.int32, (r_t, r_t), 1)
    same = (ii // c_sz) == (jj // c_sz)
    if rev:
        incl, strict = same & (jj >= ii), same & (jj > ii)
    else:
        incl, strict = same & (jj <= ii), same & (jj < ii)
    m_incl = incl.astype(BF16)
    m_same = same.astype(BF16)
    eye = (ii == jj).astype(F32)

    def pieces(x):
        hi = x.astype(BF16)
        rest = x - hi.astype(F32)
        mid = rest.astype(BF16)
        return hi, mid, (rest - mid.astype(F32)).astype(BF16)

    lane = lax.broadcasted_iota(jnp.int32, (1, LANES), 1)
    is_g = (lane >= goff) & (lane < goff + GDN_HEADS)
    raw = gate_ref[0]
    g_all = jnp.where(is_g, -jnp.exp(alog_ref[...]) * _softplus(raw + dt_ref[...]), 0.0)
    beta_all = _sigmoid(raw)
    g_parts = pieces(g_all)
    gc_all = sum(_dot(m_incl, x) for x in g_parts)
    gl_all = sum(_dot(m_same, x) for x in g_parts)
    g_t = -jnp.exp(alog_t_ref[...]) * _softplus(gate_t_ref[0] + dt_t_ref[...])
    gct_scr[...] = sum(_dot_nt(x, m_incl) for x in pieces(g_t))
    lane_f = lax.broadcasted_iota(jnp.int32, (r_t, LANES), 1)

    def packed(m):
        return sum(m[c * c_sz:(c + 1) * c_sz] for c in range(r_t // c_sz))

    def blockdiag(p):
        return jnp.where(same, jnp.tile(p, (r_t // c_sz, 1)), 0.0)

    def column(a, idx):
        return jnp.sum(jnp.where(lane_f == idx, a, 0.0), axis=1, keepdims=True)

    heads = range(GDN_HEADS_PER_STEP)
    n_chunks = r_t // c_sz
    n_levels = int(math.log2(c_sz)) - 1
    st = []
    for hh in heads:
        head = hb * GDN_HEADS_PER_STEP + hh
        cols = slice(hh * GDN_HEAD_DIM, (hh + 1) * GDN_HEAD_DIM)
        q, k, v = q_ref[0, :, cols], k_ref[0, :, cols], v_ref[0, :, cols]
        gc = column(gc_all, goff + head)
        gl = column(gl_all, goff + head)
        beta = column(beta_all, goff + GDN_HEADS + head)
        gc_row = gct_scr[pl.ds(goff + head, 1), :]
        decay = jnp.where(incl, jnp.exp(jnp.where(incl, gc - gc_row, 0.0)), 0.0)
        kb = k * beta
        k_bf = k.astype(BF16)
        a_mat = jnp.where(strict, _dot_nt(kb.astype(BF16), k_bf) * decay, 0.0)
        a_pk = packed(a_mat)
        st.append(dict(
            cols=cols, gl=gl, inv=packed(eye) - a_pk, pw=a_pk.astype(BF16), pw_bd=a_mat.astype(BF16),
            rhs=jnp.concatenate([v * beta, kb * jnp.exp(gc)], axis=1).astype(BF16),
            intra=(_dot_nt(q.astype(BF16), k_bf) * decay).astype(BF16),
            q_dec=(q * jnp.exp(gc)).astype(BF16),
            k_dec_t=(k * jnp.exp(gl - gc)).T.astype(BF16),
            s=s_scr[hh], outs=[None] * n_chunks))
    for t in st:
        sq = _dot(t["pw"], t["pw_bd"])
        t["pw"], t["pw_bd"] = sq.astype(BF16), blockdiag(sq).astype(BF16)
    for level in range(n_levels):
        last = level == n_levels - 1
        for t in st:
            inv_bf = t["inv"].astype(BF16)
            both = _dot(inv_bf if last else jnp.concatenate([inv_bf, t["pw"]], axis=0), t["pw_bd"])
            t["inv"] = t["inv"] + both[:c_sz]
            if not last:
                t["pw"], t["pw_bd"] = both[c_sz:].astype(BF16), blockdiag(both[c_sz:]).astype(BF16)
    for t in st:
        uw = _dot(blockdiag(t["inv"]).astype(BF16), t["rhs"])
        t["u"], t["w"] = uw[:, :GDN_HEAD_DIM], uw[:, GDN_HEAD_DIM:].astype(BF16)
    for c in (range(n_chunks - 1, -1, -1) if rev else range(n_chunks)):
        rows = slice(c * c_sz, (c + 1) * c_sz)
        for t in st:
            t["s_bf"] = t["s"].astype(BF16)
            t["v_new"] = (t["u"][rows] - _dot(t["w"][rows], t["s_bf"])).astype(BF16)
        for t in st:
            t["outs"][c] = _dot(t["q_dec"][rows], t["s_bf"]) + _dot(t["intra"][rows, rows], t["v_new"])
            t["s"] = (t["s"] * jnp.exp(t["gl"][c * c_sz:c * c_sz + 1, :])
                      + _dot(t["k_dec_t"][:, rows], t["v_new"]))
    for hh, t in enumerate(st):
        s_scr[hh] = t["s"]
        o_ref[0, :, t["cols"]] = jnp.concatenate(t["outs"], axis=0)


def _gdn_scan(qkv, gates, gates_t, a_log_row, dt_row, a_log_col, dt_col, n_lat, rev):
    bsz, nt, _ = qkv.shape
    hps = GDN_HEADS_PER_STEP
    hblocks = GDN_HEADS // hps
    n_groups = nt // ROW_TILE
    lat_groups = n_lat // ROW_TILE
    width = hps * GDN_HEAD_DIM

    def group(i):
        lat = (lat_groups - i) if rev else (i - 1)
        return jnp.where(i == 0, n_groups - 1, lat)

    return pl.pallas_call(
        functools.partial(_gdn_body, rev=rev, n_lat_groups=lat_groups),
        grid=(bsz, hblocks, n_groups),
        in_specs=[pl.BlockSpec((1, ROW_TILE, width), lambda b, h, i: (b, group(i), h)),
                  pl.BlockSpec((1, ROW_TILE, width), lambda b, h, i: (b, group(i), hblocks + h)),
                  pl.BlockSpec((1, ROW_TILE, width), lambda b, h, i: (b, group(i), 2 * hblocks + h)),
                  pl.BlockSpec((1, ROW_TILE, LANES), lambda b, h, i: (b, group(i), 0)),
                  pl.BlockSpec((1, 4 * GDN_HEADS, ROW_TILE), lambda b, h, i: (b, 0, group(i))),
                  pl.BlockSpec((1, LANES), lambda b, h, i: (0, 0)),
                  pl.BlockSpec((1, LANES), lambda b, h, i: (0, 0)),
                  pl.BlockSpec((4 * GDN_HEADS, 1), lambda b, h, i: (0, 0)),
                  pl.BlockSpec((4 * GDN_HEADS, 1), lambda b, h, i: (0, 0))],
        out_specs=pl.BlockSpec((1, ROW_TILE, width), lambda b, h, i: (b, group(i), h)),
        out_shape=jax.ShapeDtypeStruct((bsz, nt, GDN_HEADS * GDN_HEAD_DIM), F32),
        scratch_shapes=[pltpu.VMEM((hps, GDN_HEAD_DIM, GDN_HEAD_DIM), F32),
                        pltpu.VMEM((4 * GDN_HEADS, ROW_TILE), F32)],
        compiler_params=_cparams("arbitrary", "arbitrary", "arbitrary"),
        name="gdn_scan_bwd" if rev else "gdn_scan_fwd",
    )(qkv, qkv, qkv, gates, gates_t, a_log_row, dt_row, a_log_col, dt_col)


def _gdn_out_body(of_ref, ob_ref, z_ref, gn_ref, w_ref, h_ref, mod_ref, g_ref, g_ffn_ref, wr_ref,
                  out_ref, u_ref, aff_ref, a_scr):
    for hh in range(GDN_HEADS):
        cols = slice(hh * GDN_HEAD_DIM, (hh + 1) * GDN_HEAD_DIM)
        o = of_ref[0, :, cols] + ob_ref[0, :, cols]
        a_scr[:, cols] = (_rms(o, gn_ref[...]) * _silu(z_ref[0, :, cols].astype(F32))).astype(BF16)
    y = _dot(a_scr[...], w_ref[...])
    _residual_and_route(h_ref[0], y, mod_ref[0, 0], g_ref, g_ffn_ref, wr_ref, out_ref, u_ref, aff_ref)


def _gdn_out(o_f, o_b, p, g_onorm, w_bf, h, mods, g_post, g_ffn, w_router, n_lat):
    bsz, _, d = h.shape
    width = GDN_HEADS * GDN_HEAD_DIM
    r_in, r_out, r_shape = _route_specs(bsz, n_lat, d)
    return pl.pallas_call(
        _gdn_out_body,
        grid=(bsz, n_lat // ROW_TILE),
        in_specs=[pl.BlockSpec((1, ROW_TILE, width), lambda b, i: (b, i, 0)),
                  pl.BlockSpec((1, ROW_TILE, width), lambda b, i: (b, i, 0)),
                  pl.BlockSpec((1, ROW_TILE, width), lambda b, i: (b, i, 3)),
                  pl.BlockSpec((1, GDN_HEAD_DIM), lambda b, i: (0, 0)),
                  pl.BlockSpec((width, d), lambda b, i: (0, 0)),
                  pl.BlockSpec((1, ROW_TILE, d), lambda b, i: (b, i, 0)),
                  pl.BlockSpec((1, 1, N_MOD, d), lambda b, i: (b, 0, 0, 0)),
                  pl.BlockSpec((1, d), lambda b, i: (0, 0))] + r_in,
        out_specs=r_out,
        out_shape=r_shape,
        scratch_shapes=[pltpu.VMEM((ROW_TILE, width), BF16)],
        compiler_params=_cparams("arbitrary", "arbitrary"),
        name="gdn_out",
    )(o_f, o_b, p, g_onorm[None, :], w_bf, h, mods, g_post[None, :], g_ffn[None, :], _pad_lanes(w_router))


def _gate_vectors(a_log_f, dt_bias_f, a_log_b, dt_bias_b):
    z = jnp.zeros((GDN_HEADS,), F32)
    a = jnp.concatenate([a_log_f, z, a_log_b, z])
    d = jnp.concatenate([dt_bias_f, z, dt_bias_b, z])
    pad = jnp.zeros((LANES - 4 * GDN_HEADS,), F32)
    return (jnp.concatenate([a, pad])[None, :], jnp.concatenate([d, pad])[None, :], a[:, None], d[:, None])


def kernel(x, c, ctx, c_ctx, l0_w_ada, l0_b_ada, l0_g_pre_mix, l0_g_post_mix, l0_g_pre_ffn, l0_g_post_ffn, l0_w_qkv, l0_lambda_q1, l0_lambda_k1, l0_lambda_q2, l0_lambda_k2, l0_g_subln, l0_w_o, l0_w_router, l0_w_gate, l0_w_up, l0_w_down, l1_w_ada, l1_b_ada, l1_g_pre_mix, l1_g_post_mix, l1_g_pre_ffn, l1_g_post_ffn, l1_w_in, l1_w_conv, l1_a_log_f, l1_dt_bias_f, l1_a_log_b, l1_dt_bias_b, l1_g_onorm, l1_w_o, l1_w_router, l1_w_gate, l1_w_up, l1_w_down):
    n_lat, n_ctx = x.shape[1], ctx.shape[1]
    h = jnp.concatenate([x, ctx], axis=1)

    mods = _ada_mods(c, c_ctx, l0_w_ada, l0_b_ada)
    qkv = _project(h, mods, l0_g_pre_mix, l0_w_qkv.astype(BF16), n_lat, rope_tables=_rope_tables(n_lat, n_ctx))
    lam_vecs = jnp.stack([l0_lambda_q1, l0_lambda_k1, l0_lambda_q2, l0_lambda_k2])
    o = _diff_attention(qkv, lam_vecs, l0_g_subln, n_lat, depth=0)
    h, u, aff = _out_project(o, l0_w_o.astype(BF16), h, mods, l0_g_post_mix, l0_g_pre_ffn, l0_w_router, n_lat)
    h = _moe(h, u, aff, mods, l0_g_post_ffn, l0_w_gate, l0_w_up, l0_w_down, n_lat, n_ctx)

    mods = _ada_mods(c, c_ctx, l1_w_ada, l1_b_ada)
    width = GDN_HEADS * GDN_HEAD_DIM
    w_gates = jnp.pad(l1_w_in[:, 4 * width:], ((0, 0), (0, LANES - 4 * GDN_HEADS))).astype(BF16)
    p, gates = _project(h, mods, l1_g_pre_mix, l1_w_in[:, :4 * width].astype(BF16), n_lat, w_gates=w_gates)
    gates_t = jnp.swapaxes(gates[:, :, :4 * GDN_HEADS], 1, 2)
    qkv = _short_conv(p, l1_w_conv, n_lat)
    gv = _gate_vectors(l1_a_log_f, l1_dt_bias_f, l1_a_log_b, l1_dt_bias_b)
    o_f = _gdn_scan(qkv, gates, gates_t, *gv, n_lat, rev=False)
    o_b = _gdn_scan(qkv, gates, gates_t, *gv, n_lat, rev=True)
    hl, u, aff = _gdn_out(o_f, o_b, p, l1_g_onorm, l1_w_o.astype(BF16), h, mods, l1_g_post_mix, l1_g_pre_ffn,
                          l1_w_router, n_lat)
    return _moe(hl, u, aff, mods, l1_g_post_ffn, l1_w_gate, l1_w_up, l1_w_down, n_lat, 0)
```

```python
import functools
import math

import jax
import jax.numpy as jnp
from jax import lax
from jax.experimental import pallas as pl
from jax.experimental.pallas import tpu as pltpu

F32 = jnp.float32
BF16 = jnp.bfloat16
HIGHEST = lax.Precision.HIGHEST

EPS = 1e-6
N_MOD = 6
GRID_W = 64
ROPE_BASE = 10000.0
DA_HEADS = 8
DA_HEAD_DIM = 64
GDN_HEADS = 8
GDN_HEAD_DIM = 128
CONV_K = 5
CONV_HALO = 8
N_EXPERTS = 16
CAP_FACTOR = 2

LANES = 128
ROW_TILE = 256
PROJ_ROWS = 768
ATTN_Q_ROWS = 1024
ATTN_CHAIN_ROWS = 128
Q_PRESCALE = DA_HEAD_DIM ** -0.5 * math.log2(math.e)
GDN_CHUNK = 64
GDN_HEADS_PER_STEP = 8
FF_CHUNK = 1024
SLOT_WINDOW = 64
SLOT_ALIGN = 16
TABLE_COUNT = 64
VMEM_LIMIT = 56 * 1024 * 1024

SH_M, SC_M, GT_M, SH_F, SC_F, GT_F = range(6)


def _cparams(*sem):
    return pltpu.CompilerParams(dimension_semantics=sem, vmem_limit_bytes=VMEM_LIMIT)


def _dot(a, b):
    return jnp.dot(a, b, preferred_element_type=F32)


def _dot_nt(a, b, precision=None):
    return lax.dot_general(a, b, (((1,), (1,)), ((), ())), preferred_element_type=F32, precision=precision)


def _dot_tn(a, b):
    return lax.dot_general(a, b, (((0,), (0,)), ((), ())), preferred_element_type=F32)


def _rms(x, g):
    return x * lax.rsqrt(jnp.mean(x * x, axis=-1, keepdims=True) + EPS) * g


def _sigmoid(x):
    return 1.0 / (1.0 + jnp.exp(-x))


def _silu(x):
    return x * _sigmoid(x)


def _softplus(x):
    return jnp.maximum(x, 0.0) + jnp.log(1.0 + jnp.exp(-jnp.abs(x)))


def _ada_body(c_ref, w_ref, b_ref, o_ref):
    o_ref[...] = jnp.dot(_silu(c_ref[...]), w_ref[...], precision=HIGHEST, preferred_element_type=F32) + b_ref[...]


def _ada_mods(c, c_ctx, w_ada, b_ada):
    bsz, d = c.shape
    rows = 16
    cc = jnp.concatenate([c, c_ctx[None, :], jnp.zeros((rows - bsz - 1, d), F32)], axis=0)
    tn = 1024
    m = pl.pallas_call(
        _ada_body,
        grid=(N_MOD * d // tn,),
        in_specs=[pl.BlockSpec((rows, d), lambda j: (0, 0)),
                  pl.BlockSpec((d, tn), lambda j: (0, j)),
                  pl.BlockSpec((1, tn), lambda j: (0, j))],
        out_specs=pl.BlockSpec((rows, tn), lambda j: (0, j)),
        out_shape=jax.ShapeDtypeStruct((rows, N_MOD * d), F32),
        compiler_params=_cparams("arbitrary"),
        name="ada_mods",
    )(cc, w_ada, b_ada[None, :])
    lat = m[:bsz].reshape(bsz, 1, N_MOD, d)
    ctx = jnp.broadcast_to(m[bsz].reshape(1, 1, N_MOD, d), (bsz, 1, N_MOD, d))
    return jnp.concatenate([lat, ctx], axis=1)


def _proj_body(h_ref, mod_ref, g_ref, w_ref, *rest, n_lat, rope, gates):
    rest = list(rest)
    cos_ref, sin_ref = (rest.pop(0), rest.pop(0)) if rope else (None, None)
    wg_ref = rest.pop(0) if gates else None
    o_ref = rest.pop(0)
    og_ref = rest.pop(0) if gates else None
    x = h_ref[0]
    tm, d = x.shape
    row = pl.program_id(1) * tm + lax.broadcasted_iota(jnp.int32, (tm, 1), 0)
    is_ctx = row >= n_lat
    scale = jnp.where(is_ctx, mod_ref[0, 1, SC_M:SC_M + 1, :], mod_ref[0, 0, SC_M:SC_M + 1, :])
    shift = jnp.where(is_ctx, mod_ref[0, 1, SH_M:SH_M + 1, :], mod_ref[0, 0, SH_M:SH_M + 1, :])
    u = (_rms(x, g_ref[...]) * (1.0 + scale) + shift).astype(BF16)
    if gates:
        og_ref[0] = _dot(u, wg_ref[...])
    for j in range(w_ref.shape[1] // d):
        cols = slice(j * d, (j + 1) * d)
        acc = _dot(u, w_ref[:, cols])
        if rope and j < 2:
            reps = d // LANES
            cos = jnp.tile(cos_ref[...], (1, reps))
            sin = jnp.tile(sin_ref[...], (1, reps))
            lane = lax.broadcasted_iota(jnp.int32, acc.shape, 1)
            half = DA_HEAD_DIM // 4
            first = (lane % (2 * half)) < half
            partner = jnp.where(first, pltpu.roll(acc, d - half, 1), pltpu.roll(acc, half, 1))
            acc = acc * cos + partner * sin
            if j == 0:
                acc = acc * Q_PRESCALE
        o_ref[0, :, cols] = acc.astype(o_ref.dtype)


def _project(h, mods, g, w_bf, n_lat, rope_tables=None, w_gates=None):
    bsz, nt, d = h.shape
    n_out = w_bf.shape[1]
    tm = PROJ_ROWS
    assert nt % tm == 0 and n_lat % ROW_TILE == 0 and (nt - n_lat) == ROW_TILE
    rope = rope_tables is not None
    gates = w_gates is not None
    in_specs = [pl.BlockSpec((1, tm, d), lambda b, i: (b, i, 0)),
                pl.BlockSpec((1, 2, N_MOD, d), lambda b, i: (b, 0, 0, 0)),
                pl.BlockSpec((1, d), lambda b, i: (0, 0)),
                pl.BlockSpec((d, n_out), lambda b, i: (0, 0))]
    args = [h, mods, g[None, :], w_bf]
    out_specs = [pl.BlockSpec((1, tm, n_out), lambda b, i: (b, i, 0))]
    out_shape = [jax.ShapeDtypeStruct((bsz, nt, n_out), BF16)]
    if rope:
        in_specs += [pl.BlockSpec((tm, LANES), lambda b, i: (i, 0))] * 2
        args += list(rope_tables)
    if gates:
        in_specs.append(pl.BlockSpec((d, LANES), lambda b, i: (0, 0)))
        args.append(w_gates)
        out_specs.append(pl.BlockSpec((1, tm, LANES), lambda b, i: (b, i, 0)))
        out_shape.append(jax.ShapeDtypeStruct((bsz, nt, LANES), F32))
    out = pl.pallas_call(
        functools.partial(_proj_body, n_lat=n_lat, rope=rope, gates=gates),
        grid=(bsz, nt // tm),
        in_specs=in_specs,
        out_specs=out_specs,
        out_shape=out_shape,
        compiler_params=_cparams("arbitrary", "arbitrary"),
        name="mod_project",
    )(*args)
    return out if gates else out[0]


def _rope_tables(n_lat, n_ctx):
    rows = n_lat // GRID_W
    r = jnp.repeat(jnp.arange(rows), GRID_W).astype(F32)
    col = jnp.tile(jnp.arange(GRID_W), rows).astype(F32)
    half = DA_HEAD_DIM // 2
    inv = ROPE_BASE ** (-jnp.arange(0, half, 2, dtype=F32) / half)
    ang_r, ang_c = r[:, None] * inv, col[:, None] * inv
    cos = jnp.concatenate([jnp.cos(ang_r)] * 2 + [jnp.cos(ang_c)] * 2, axis=-1)
    sin = jnp.concatenate([-jnp.sin(ang_r), jnp.sin(ang_r), -jnp.sin(ang_c), jnp.sin(ang_c)], axis=-1)
    cos = jnp.concatenate([jnp.tile(cos, (1, 2)), jnp.ones((n_ctx, LANES), F32)], axis=0)
    sin = jnp.concatenate([jnp.tile(sin, (1, 2)), jnp.zeros((n_ctx, LANES), F32)], axis=0)
    return cos, sin


def _attn_body(lam_ref, q_ref, k_ref, v_ref, gs_ref, o_ref, *, lam_init):
    lv = lam_ref[...]
    lam = (jnp.exp(jnp.sum(lv[0:1] * lv[1:2], axis=-1, keepdims=True))
           - jnp.exp(jnp.sum(lv[2:3] * lv[3:4], axis=-1, keepdims=True)) + lam_init)
    hw = 2 * DA_HEAD_DIM
    n_q = q_ref.shape[1]
    lane = lax.broadcasted_iota(jnp.int32, (n_q, hw), 1)
    rows = min(n_q, ATTN_CHAIN_ROWS)
    chains = [(hh, r) for hh in range(q_ref.shape[2] // hw) for r in range(0, n_q, rows)]
    lhs, v_one = {}, {}
    for hh in range(q_ref.shape[2] // hw):
        cols = slice(hh * hw, (hh + 1) * hw)
        q, v = q_ref[0, :, cols], v_ref[0, :, cols]
        v_one[hh] = jnp.concatenate([v, jnp.ones_like(v)], axis=1)
        qm = [jnp.where((lane >= c * DA_HEAD_DIM) & (lane < (c + 1) * DA_HEAD_DIM), q, jnp.zeros_like(q))
              for c in range(2)]
        for r in range(0, n_q, rows):
            lhs[hh, r] = jnp.concatenate([qm[0][r:r + rows], qm[1][r:r + rows]], axis=0)
    s = [_dot_nt(lhs[hh, r], k_ref[0, :, hh * hw:(hh + 1) * hw]) for hh, r in chains]
    e = [jnp.exp2((x - jnp.max(x, axis=-1, keepdims=True)).astype(BF16)) for x in s]
    ov = [_dot(x, v_one[hh]) for x, (hh, r) in zip(e, chains)]
    att = [x[:, :hw] / x[:, hw:hw + 1] for x in ov]
    for (hh, r), a in zip(chains, att):
        o = a[:rows] - lam * a[rows:]
        o_ref[0, r:r + rows, hh * hw:(hh + 1) * hw] = (_rms(o, gs_ref[...]) * (1.0 - lam_init)).astype(o_ref.dtype)


def _diff_attention(qkv, lam_vecs, g_subln, n_lat, depth):
    bsz, nt, d3 = qkv.shape
    d = d3 // 3
    hw = 2 * DA_HEAD_DIM
    lam_init = 0.8 - 0.6 * math.exp(-0.3 * depth)
    n_ctx = nt - n_lat

    def call(n_q, tq, q_blk0, n_k, k_blk0, heads):
        hblocks = DA_HEADS // heads
        width = heads * hw
        return pl.pallas_call(
            functools.partial(_attn_body, lam_init=lam_init),
            grid=(bsz, hblocks, n_q // tq),
            in_specs=[pl.BlockSpec((4, DA_HEAD_DIM), lambda b, h, i: (0, 0)),
                      pl.BlockSpec((1, tq, width), lambda b, h, i: (b, q_blk0 + i, h)),
                      pl.BlockSpec((1, n_k, width), lambda b, h, i: (b, k_blk0, hblocks + h)),
                      pl.BlockSpec((1, n_k, width), lambda b, h, i: (b, k_blk0, 2 * hblocks + h)),
                      pl.BlockSpec((1, hw), lambda b, h, i: (0, 0))],
            out_specs=pl.BlockSpec((1, tq, width), lambda b, h, i: (b, i, h)),
            out_shape=jax.ShapeDtypeStruct((bsz, n_q, d), BF16),
            compiler_params=_cparams("arbitrary", "arbitrary", "arbitrary"),
            name="diff_attention",
        )(lam_vecs, qkv, qkv, qkv, g_subln[None, :])

    o_lat = call(n_lat, min(ATTN_Q_ROWS, n_lat), 0, nt, 0, heads=1)
    o_ctx = call(n_ctx, n_ctx, n_lat // n_ctx, n_ctx, n_lat // n_ctx, heads=DA_HEADS)
    return jnp.concatenate([o_lat, o_ctx], axis=1)


def _residual_and_route(h, y, mod, g_post_ref, g_ffn_ref, wr_ref, out_ref, u_ref, aff_ref):
    h_new = h + mod[GT_M:GT_M + 1, :] * _rms(y, g_post_ref[...])
    out_ref[0] = h_new
    u = _rms(h_new, g_ffn_ref[...]) * (1.0 + mod[SC_F:SC_F + 1, :]) + mod[SH_F:SH_F + 1, :]
    u_hi = u.astype(BF16)
    u_ref[0] = u_hi
    u_lo = (u - u_hi.astype(F32)).astype(BF16)
    w = wr_ref[...]
    w_hi = w.astype(BF16)
    w_lo = (w - w_hi.astype(F32)).astype(BF16)
    logits = (_dot(u_hi, w_hi) + _dot(u_hi, w_lo) + _dot(u_lo, w_hi)).T[:N_EXPERTS]
    e = jnp.exp(logits - jnp.max(logits, axis=0, keepdims=True))
    aff_ref[0] = e / jnp.sum(e, axis=0, keepdims=True)


def _pad_lanes(w):
    return jnp.pad(w, ((0, 0), (0, LANES - w.shape[1])))


def _route_specs(bsz, nt, d):
    in_specs = [pl.BlockSpec((1, d), lambda b, i: (0, 0)),
                pl.BlockSpec((d, LANES), lambda b, i: (0, 0))]
    out_specs = [pl.BlockSpec((1, ROW_TILE, d), lambda b, i: (b, i, 0)),
                 pl.BlockSpec((1, ROW_TILE, d), lambda b, i: (b, i, 0)),
                 pl.BlockSpec((1, N_EXPERTS, ROW_TILE), lambda b, i: (b, 0, i))]
    out_shape = [jax.ShapeDtypeStruct((bsz, nt, d), F32),
                 jax.ShapeDtypeStruct((bsz, nt, d), BF16),
                 jax.ShapeDtypeStruct((bsz, N_EXPERTS, nt), F32)]
    return in_specs, out_specs, out_shape


def _oproj_body(o_ref, w_ref, h_ref, mod_ref, g_ref, g_ffn_ref, wr_ref, out_ref, u_ref, aff_ref):
    y = _dot(o_ref[0], w_ref[...])
    _residual_and_route(h_ref[0], y, mod_ref[0, 0], g_ref, g_ffn_ref, wr_ref, out_ref, u_ref, aff_ref)


def _out_project(o, w_bf, h, mods, g_post, g_ffn, w_router, n_lat):
    bsz, nt, d = h.shape
    lat_tiles = n_lat // ROW_TILE
    r_in, r_out, r_shape = _route_specs(bsz, nt, d)
    return pl.pallas_call(
        _oproj_body,
        grid=(bsz, nt // ROW_TILE),
        in_specs=[pl.BlockSpec((1, ROW_TILE, d), lambda b, i: (b, i, 0)),
                  pl.BlockSpec((d, d), lambda b, i: (0, 0)),
                  pl.BlockSpec((1, ROW_TILE, d), lambda b, i: (b, i, 0)),
                  pl.BlockSpec((1, 1, N_MOD, d), lambda b, i: (b, i // lat_tiles, 0, 0)),
                  pl.BlockSpec((1, d), lambda b, i: (0, 0))] + r_in,
        out_specs=r_out,
        out_shape=r_shape,
        compiler_params=_cparams("arbitrary", "arbitrary"),
        name="out_project",
    )(o, w_bf, h, mods, g_post[None, :], g_ffn[None, :], _pad_lanes(w_router))


def _lane_cumsum(x):
    n = x.shape[1]
    jj = lax.broadcasted_iota(jnp.int32, (ROW_TILE, ROW_TILE), 0)
    nn = lax.broadcasted_iota(jnp.int32, (ROW_TILE, ROW_TILE), 1)
    tri = (jj <= nn).astype(BF16)
    run = jnp.zeros((x.shape[0], 1), F32)
    parts = []
    for t in range(n // ROW_TILE):
        local = _dot(x[:, t * ROW_TILE:(t + 1) * ROW_TILE].astype(BF16), tri) + run
        parts.append(local)
        run = local[:, ROW_TILE - 1:ROW_TILE]
    return jnp.concatenate(parts, axis=1) if len(parts) > 1 else parts[0]


def _select_slots(aff, cap):
    bits = pltpu.bitcast(aff, jnp.int32)

    def enough(cand):
        return jnp.sum((bits >= cand).astype(F32), axis=1, keepdims=True) >= cap

    def step(t, lo):
        hi_bit = jnp.left_shift(jnp.int32(1), 29 - 2 * t)
        lo_bit = jnp.left_shift(jnp.int32(1), 28 - 2 * t)
        both, first, second = lo | hi_bit | lo_bit, lo | hi_bit, lo | lo_bit
        return jnp.where(enough(both), both, jnp.where(enough(first), first, jnp.where(enough(second), second, lo)))

    top = jnp.full((aff.shape[0], 1), 1 << 30, jnp.int32)
    thr = jnp.where(enough(top), top, 0)
    thr = lax.fori_loop(0, 15, step, thr)
    gt = bits > thr
    eq = bits == thr
    need = cap - jnp.sum(gt.astype(F32), axis=1, keepdims=True)
    eq_rank = _lane_cumsum(eq.astype(F32))
    sel = gt | (eq & (eq_rank <= need))
    sel_f = sel.astype(F32)
    slot = _lane_cumsum(sel_f) - 1.0
    return jnp.where(sel, slot, -1.0).astype(jnp.int32), sel_f


def _tile_table(sel_f, tile0):
    lane = lax.broadcasted_iota(jnp.int32, (sel_f.shape[0], LANES), 1)
    run = jnp.zeros((sel_f.shape[0], 1), F32)
    tab = jnp.zeros((sel_f.shape[0], LANES), F32)
    for t in range(sel_f.shape[1] // ROW_TILE):
        inside = jnp.sum(sel_f[:, t * ROW_TILE:(t + 1) * ROW_TILE], axis=1, keepdims=True)
        tab = jnp.where(lane == tile0 + t, run, tab)
        tab = jnp.where(lane == TABLE_COUNT + tile0 + t, inside, tab)
        run = run + inside
    return tab


def _select_body(aff_ref, pos_ref, tab_ref, *, n_lat, n_ctx):
    aff = aff_ref[0]
    pos, sel_f = _select_slots(aff[:, :n_lat], CAP_FACTOR * n_lat // N_EXPERTS)
    pos_ref[0, :, :n_lat] = pos
    tab = _tile_table(sel_f, 0)
    if n_ctx:
        pos, sel_f = _select_slots(aff[:, n_lat:], CAP_FACTOR * n_ctx // N_EXPERTS)
        pos_ref[0, :, n_lat:] = pos
        tab = tab + _tile_table(sel_f, n_lat // ROW_TILE)
    tab_ref[0] = tab.astype(jnp.int32)


def _select(aff, n_lat, n_ctx):
    bsz, e, nt = aff.shape
    n_tiles = nt // ROW_TILE
    pos, tab = pl.pallas_call(
        functools.partial(_select_body, n_lat=n_lat, n_ctx=n_ctx),
        grid=(bsz,),
        in_specs=[pl.BlockSpec((1, e, nt), lambda b: (b, 0, 0))],
        out_specs=[pl.BlockSpec((1, e, nt), lambda b: (b, 0, 0)),
                   pl.BlockSpec((1, e, LANES), lambda b: (b, 0, 0))],
        out_shape=[jax.ShapeDtypeStruct((bsz, e, nt), jnp.int32),
                   jax.ShapeDtypeStruct((bsz, e, LANES), jnp.int32)],
        compiler_params=_cparams("arbitrary"),
        name="moe_select",
    )(aff)
    first = tab[:, :, :n_tiles].reshape(-1)
    count = tab[:, :, TABLE_COUNT:TABLE_COUNT + n_tiles].reshape(-1)
    return pos, first, count


class _TileWindows:
    def __init__(self, first_ref, count_ref, pos_ref, n_lat, rows, lat_cap):
        b, tile = pl.program_id(0), pl.program_id(1)
        n_exp = pos_ref.shape[1]
        self.rows, self.n_exp = rows, n_exp
        base = jnp.where(tile * ROW_TILE >= n_lat, lat_cap, 0)
        pos = pos_ref[0]
        self.slot = jnp.where(pos >= 0, pos + base, -1)
        self.starts, self.rounds = [], 0
        for x in range(n_exp):
            at = (b * n_exp + x) * pl.num_programs(1) + tile
            first = first_ref[at]
            start = base + (first // SLOT_ALIGN) * SLOT_ALIGN
            self.starts.append(start)
            self.rounds = jnp.maximum(self.rounds, (base + first + count_ref[at] - start + SLOT_WINDOW - 1) // SLOT_WINDOW)
        self.ids0 = lax.broadcasted_iota(jnp.int32, (SLOT_WINDOW, ROW_TILE), 0)

    def window(self, x, r):
        want = self.starts[x] + r * SLOT_WINDOW
        s0 = jnp.minimum(want, self.rows - SLOT_WINDOW)
        ids = self.ids0 + s0
        hot = (self.slot[x:x + 1, :] == ids) & (ids >= want)
        return pl.multiple_of(x * self.rows + s0, SLOT_ALIGN), hot


def _as_bf16(mask):
    return jnp.where(mask, 1.0, 0.0).astype(BF16)


def _gather_body(first_ref, count_ref, pos_ref, aff_ref, u_ref, xs_ref, gate_ref, *, n_lat, rows, lat_cap):
    @pl.when(pl.program_id(1) == 0)
    def _():
        xs_ref[...] = jnp.zeros_like(xs_ref)
        gate_ref[...] = jnp.zeros_like(gate_ref)

    tw = _TileWindows(first_ref, count_ref, pos_ref, n_lat, rows, lat_cap)
    aff = aff_ref[0]
    u = u_ref[0]

    def one_round(r, carry):
        wins = [tw.window(x, r) for x in range(tw.n_exp)]
        part = _dot(jnp.concatenate([_as_bf16(hot) for _, hot in wins], axis=0), u)
        for x, (off, hot) in enumerate(wins):
            dst = pl.ds(off, SLOT_WINDOW)
            xs_ref[0, dst, :] = xs_ref[0, dst, :] + part[x * SLOT_WINDOW:(x + 1) * SLOT_WINDOW].astype(BF16)
            gate = jnp.sum(jnp.where(hot, aff[x:x + 1, :], 0.0), axis=1, keepdims=True)
            gate_ref[0, dst, :] = gate_ref[0, dst, :] + jnp.broadcast_to(gate, (SLOT_WINDOW, LANES))
        return carry

    lax.fori_loop(0, tw.rounds, one_round, 0)


def _slot_rows(n_lat, n_ctx):
    return CAP_FACTOR * n_lat // N_EXPERTS + CAP_FACTOR * n_ctx // N_EXPERTS


def _gather(pos, first, count, aff, u, n_lat, n_ctx):
    bsz, nt, d = u.shape
    e = pos.shape[1]
    rows = _slot_rows(n_lat, n_ctx)
    xs, gate = pl.pallas_call(
        functools.partial(_gather_body, n_lat=n_lat, rows=rows, lat_cap=CAP_FACTOR * n_lat // N_EXPERTS),
        grid_spec=pltpu.PrefetchScalarGridSpec(
            num_scalar_prefetch=2,
            grid=(bsz, nt // ROW_TILE),
            in_specs=[pl.BlockSpec((1, e, ROW_TILE), lambda b, i, *_: (b, 0, i)),
                      pl.BlockSpec((1, e, ROW_TILE), lambda b, i, *_: (b, 0, i)),
                      pl.BlockSpec((1, ROW_TILE, d), lambda b, i, *_: (b, i, 0))],
            out_specs=[pl.BlockSpec((1, e * rows, d), lambda b, i, *_: (b, 0, 0)),
                       pl.BlockSpec((1, e * rows, LANES), lambda b, i, *_: (b, 0, 0))]),
        out_shape=[jax.ShapeDtypeStruct((bsz, e * rows, d), BF16),
                   jax.ShapeDtypeStruct((bsz, e * rows, LANES), F32)],
        compiler_params=_cparams("arbitrary", "arbitrary"),
        name="moe_gather",
    )(first, count, pos, aff, u)
    return xs.reshape(bsz, e, rows, d), gate.reshape(bsz, e, rows, LANES)


def _expert_body(xs_ref, gate_ref, wg_hbm, wu_hbm, wd_hbm, y_ref, wg_scr, wu_scr, wd_scr,
                 wg_stage, wu_stage, wd_stage, sems):
    x, b = pl.program_id(0), pl.program_id(1)
    n_exp, n_chunks = pl.num_programs(0), pl.num_programs(1)
    rows_in, rows_mid = wg_stage.shape[0], wd_stage.shape[0]

    def chunk_copies(expert, chunk):
        return (pltpu.make_async_copy(wg_hbm.at[expert, pl.ds(chunk * rows_in, rows_in), :], wg_stage, sems.at[0]),
                pltpu.make_async_copy(wu_hbm.at[expert, pl.ds(chunk * rows_in, rows_in), :], wu_stage, sems.at[1]),
                pltpu.make_async_copy(wd_hbm.at[expert, pl.ds(chunk * rows_mid, rows_mid), :], wd_stage, sems.at[2]))

    def land(copy, chunk):
        at_in = pl.ds(pl.multiple_of(chunk * rows_in, SLOT_ALIGN), rows_in)
        at_mid = pl.ds(pl.multiple_of(chunk * rows_mid, SLOT_ALIGN), rows_mid)
        wg_scr[copy, at_in, :] = wg_stage[...].astype(BF16)
        wu_scr[copy, at_in, :] = wu_stage[...].astype(BF16)
        wd_scr[copy, at_mid, :] = wd_stage[...].astype(BF16)

    @pl.when((x == 0) & (b == 0))
    def _():
        def fetch(chunk, carry):
            copies = chunk_copies(0, chunk)
            for cp in copies:
                cp.start()
            for cp in copies:
                cp.wait()
            land(0, chunk)
            return carry

        lax.fori_loop(0, n_chunks, fetch, 0)

    def start_next(_, carry):
        for cp in chunk_copies(x + 1, b):
            cp.start()
        return carry

    lax.fori_loop(0, jnp.where(x + 1 < n_exp, 1, 0), start_next, 0)

    cur = x % 2
    xs = xs_ref[0, 0]
    acc = jnp.zeros((xs.shape[0], wd_scr.shape[2]), F32)
    for s in range(wg_scr.shape[2] // FF_CHUNK):
        cols = slice(s * FF_CHUNK, (s + 1) * FF_CHUNK)
        hid = _silu(_dot(xs, wg_scr[cur, :, cols])) * _dot(xs, wu_scr[cur, :, cols])
        acc = acc + _dot(hid.astype(BF16), wd_scr[cur, cols, :])
    y_ref[0, 0] = (acc * gate_ref[0, 0, :, 0:1]).astype(y_ref.dtype)

    @pl.when(x + 1 < n_exp)
    def _():
        for cp in chunk_copies(x + 1, b):
            cp.wait()
        land(1 - cur, b)


def _experts(xs, gate, wg, wu, wd):
    bsz, e, rows, d = xs.shape
    ff = wg.shape[2]
    return pl.pallas_call(
        _expert_body,
        grid=(e, bsz),
        in_specs=[pl.BlockSpec((1, 1, rows, d), lambda x, b: (b, x, 0, 0)),
                  pl.BlockSpec((1, 1, rows, LANES), lambda x, b: (b, x, 0, 0)),
                  pl.BlockSpec(memory_space=pl.ANY),
                  pl.BlockSpec(memory_space=pl.ANY),
                  pl.BlockSpec(memory_space=pl.ANY)],
        out_specs=pl.BlockSpec((1, 1, rows, d), lambda x, b: (b, x, 0, 0)),
        out_shape=jax.ShapeDtypeStruct((bsz, e, rows, d), BF16),
        scratch_shapes=[pltpu.VMEM((2, d, ff), BF16), pltpu.VMEM((2, d, ff), BF16), pltpu.VMEM((2, ff, d), BF16),
                        pltpu.VMEM((d // bsz, ff), F32), pltpu.VMEM((d // bsz, ff), F32),
                        pltpu.VMEM((ff // bsz, d), F32), pltpu.SemaphoreType.DMA((3,))],
        compiler_params=_cparams("arbitrary", "arbitrary"),
        name="moe_experts",
    )(xs, gate, wg, wu, wd)


def _combine_body(first_ref, count_ref, pos_ref, y_ref, h_ref, mod_ref, g_ref, out_ref, acc_scr,
                  *, n_lat, rows, lat_cap):
    tw = _TileWindows(first_ref, count_ref, pos_ref, n_lat, rows, lat_cap)
    acc_scr[...] = jnp.zeros_like(acc_scr)

    def one_round(r, carry):
        group = ROW_TILE // SLOT_WINDOW
        for g0 in range(0, tw.n_exp, group):
            wins = [tw.window(x, r) for x in range(g0, g0 + group)]
            hot = jnp.concatenate([_as_bf16(h) for _, h in wins], axis=0)
            rows_y = jnp.concatenate([y_ref[0, pl.ds(off, SLOT_WINDOW), :] for off, _ in wins], axis=0)
            acc_scr[...] += _dot_tn(hot, rows_y)
        return carry

    lax.fori_loop(0, tw.rounds, one_round, 0)
    out_ref[0] = h_ref[0] + mod_ref[0, 0, GT_F:GT_F + 1, :] * _rms(acc_scr[...], g_ref[...])


def _combine(pos, first, count, y, h, mods, g_post, n_lat, n_ctx):
    bsz, e, rows, d = y.shape
    nt = n_lat + n_ctx
    lat_tiles = n_lat // ROW_TILE
    return pl.pallas_call(
        functools.partial(_combine_body, n_lat=n_lat, rows=rows, lat_cap=CAP_FACTOR * n_lat // N_EXPERTS),
        grid_spec=pltpu.PrefetchScalarGridSpec(
            num_scalar_prefetch=2,
            grid=(bsz, nt // ROW_TILE),
            in_specs=[pl.BlockSpec((1, e, ROW_TILE), lambda b, i, *_: (b, 0, i)),
                      pl.BlockSpec((1, e * rows, d), lambda b, i, *_: (b, 0, 0)),
                      pl.BlockSpec((1, ROW_TILE, d), lambda b, i, *_: (b, i, 0)),
                      pl.BlockSpec((1, 1, N_MOD, d), lambda b, i, *_: (b, i // lat_tiles, 0, 0)),
                      pl.BlockSpec((1, d), lambda b, i, *_: (0, 0))],
            out_specs=pl.BlockSpec((1, ROW_TILE, d), lambda b, i, *_: (b, i, 0)),
            scratch_shapes=[pltpu.VMEM((ROW_TILE, d), F32)]),
        out_shape=jax.ShapeDtypeStruct((bsz, nt, d), F32),
        compiler_params=_cparams("arbitrary", "arbitrary"),
        name="moe_combine",
    )(first, count, pos, y.reshape(bsz, e * rows, d), h, mods, g_post[None, :])


def _moe(h, u, aff, mods, g_post, wg, wu, wd, n_lat, n_ctx):
    pos, first, count = _select(aff, n_lat, n_ctx)
    xs, gate = _gather(pos, first, count, aff, u, n_lat, n_ctx)
    y = _experts(xs, gate, wg, wu, wd)
    return _combine(pos, first, count, y, h, mods, g_post, n_lat, n_ctx)


def _conv_body(x_ref, w_ref, o_ref, pad_scr, *, n_lat):
    j = pl.program_id(1)
    nt = x_ref.shape[1]
    halo = CONV_HALO
    zeros = jnp.zeros((halo, LANES), F32)
    for lo, hi in ((0, n_lat), (n_lat, nt)):
        n = hi - lo
        if n == 0:
            continue
        pad_scr[0:halo, :] = zeros
        pad_scr[halo:halo + n, :] = x_ref[0, lo:hi, :].astype(F32)
        pad_scr[halo + n:2 * halo + n, :] = zeros
        acc = None
        for tap in range(CONV_K):
            start = halo + tap - CONV_K // 2
            term = pad_scr[start:start + n, :] * w_ref[tap:tap + 1, :]
            acc = term if acc is None else acc + term
        y = _silu(acc)
        inv_norm = lax.rsqrt(jnp.sum(y * y, axis=-1, keepdims=True) + EPS)
        factor = jnp.where(j < GDN_HEADS, inv_norm * GDN_HEAD_DIM ** -0.5,
                           jnp.where(j < 2 * GDN_HEADS, inv_norm, 1.0))
        o_ref[0, lo:hi, :] = y * factor


def _short_conv(p, w_conv, n_lat):
    bsz, nt, _ = p.shape
    nblk = 3 * GDN_HEADS
    return pl.pallas_call(
        functools.partial(_conv_body, n_lat=n_lat),
        grid=(bsz, nblk),
        in_specs=[pl.BlockSpec((1, nt, LANES), lambda b, j: (b, 0, j)),
                  pl.BlockSpec((CONV_K, LANES), lambda b, j: (0, j))],
        out_specs=pl.BlockSpec((1, nt, LANES), lambda b, j: (b, 0, j)),
        out_shape=jax.ShapeDtypeStruct((bsz, nt, nblk * LANES), F32),
        scratch_shapes=[pltpu.VMEM((n_lat + 2 * CONV_HALO, LANES), F32)],
        compiler_params=_cparams("arbitrary", "arbitrary"),
        name="gdn_conv",
    )(p, w_conv)


def _gdn_body(q_ref, k_ref, v_ref, gate_ref, gate_t_ref, alog_ref, dt_ref, alog_t_ref, dt_t_ref,
              o_ref, s_scr, gct_scr, *, rev, n_lat_groups):
    hb = pl.program_id(1)
    step = pl.program_id(2)
    r_t = ROW_TILE
    c_sz = GDN_CHUNK
    goff = 2 * GDN_HEADS if rev else 0

    @pl.when(step == 0)
    def _():
        s_scr[...] = jnp.zeros_like(s_scr)

    ii = lax.broadcasted_iota(jnp.int32, (r_t, r_t), 0)
    jj = lax.broadcasted_iota(jnp.int32, (r_t, r_t), 1)
    same = (ii // c_sz) == (jj // c_sz)
    if rev:
        incl, strict = same & (jj >= ii), same & (jj > ii)
    else:
        incl, strict = same & (jj <= ii), same & (jj < ii)
    m_incl = incl.astype(BF16)
    m_same = same.astype(BF16)
    eye = (ii == jj).astype(F32)

    def pieces(x):
        hi = x.astype(BF16)
        rest = x - hi.astype(F32)
        mid = rest.astype(BF16)
        return hi, mid, (rest - mid.astype(F32)).astype(BF16)

    lane = lax.broadcasted_iota(jnp.int32, (1, LANES), 1)
    is_g = (lane >= goff) & (lane < goff + GDN_HEADS)
    raw = gate_ref[0]
    g_all = jnp.where(is_g, -jnp.exp(alog_ref[...]) * _softplus(raw + dt_ref[...]), 0.0)
    beta_all = _sigmoid(raw)
    g_parts = pieces(g_all)
    gc_all = sum(_dot(m_incl, x) for x in g_parts)
    gl_all = sum(_dot(m_same, x) for x in g_parts)
    g_t = -jnp.exp(alog_t_ref[...]) * _softplus(gate_t_ref[0] + dt_t_ref[...])
    gct_scr[...] = sum(_dot_nt(x, m_incl) for x in pieces(g_t))
    lane_f = lax.broadcasted_iota(jnp.int32, (r_t, LANES), 1)

    def packed(m):
        return sum(m[c * c_sz:(c + 1) * c_sz] for c in range(r_t // c_sz))

    def blockdiag(p):
        return jnp.where(same, jnp.tile(p, (r_t // c_sz, 1)), 0.0)

    def column(a, idx):
        return jnp.sum(jnp.where(lane_f == idx, a, 0.0), axis=1, keepdims=True)

    heads = range(GDN_HEADS_PER_STEP)
    n_chunks = r_t // c_sz
    n_levels = int(math.log2(c_sz)) - 1
    st = []
    for hh in heads:
        head = hb * GDN_HEADS_PER_STEP + hh
        cols = slice(hh * GDN_HEAD_DIM, (hh + 1) * GDN_HEAD_DIM)
        q, k, v = q_ref[0, :, cols], k_ref[0, :, cols], v_ref[0, :, cols]
        gc = column(gc_all, goff + head)
        gl = column(gl_all, goff + head)
        beta = column(beta_all, goff + GDN_HEADS + head)
        gc_row = gct_scr[pl.ds(goff + head, 1), :]
        decay = jnp.where(incl, jnp.exp(jnp.where(incl, gc - gc_row, 0.0)), 0.0)
        kb = k * beta
        k_bf = k.astype(BF16)
        a_mat = jnp.where(strict, _dot_nt(kb.astype(BF16), k_bf) * decay, 0.0)
        a_pk = packed(a_mat)
        st.append(dict(
            cols=cols, gl=gl, inv=packed(eye) - a_pk, pw=a_pk.astype(BF16), pw_bd=a_mat.astype(BF16),
            rhs=jnp.concatenate([v * beta, kb * jnp.exp(gc)], axis=1).astype(BF16),
            intra=(_dot_nt(q.astype(BF16), k_bf) * decay).astype(BF16),
            q_dec=(q * jnp.exp(gc)).astype(BF16),
            k_dec_t=(k * jnp.exp(gl - gc)).T.astype(BF16),
            s=s_scr[hh], outs=[None] * n_chunks))
    for t in st:
        sq = _dot(t["pw"], t["pw_bd"])
        t["pw"], t["pw_bd"] = sq.astype(BF16), blockdiag(sq).astype(BF16)
    for level in range(n_levels):
        last = level == n_levels - 1
        for t in st:
            inv_bf = t["inv"].astype(BF16)
            both = _dot(inv_bf if last else jnp.concatenate([inv_bf, t["pw"]], axis=0), t["pw_bd"])
            t["inv"] = t["inv"] + both[:c_sz]
            if not last:
                t["pw"], t["pw_bd"] = both[c_sz:].astype(BF16), blockdiag(both[c_sz:]).astype(BF16)
    for t in st:
        uw = _dot(blockdiag(t["inv"]).astype(BF16), t["rhs"])
        t["u"], t["w"] = uw[:, :GDN_HEAD_DIM], uw[:, GDN_HEAD_DIM:].astype(BF16)
    for c in (range(n_chunks - 1, -1, -1) if rev else range(n_chunks)):
        rows = slice(c * c_sz, (c + 1) * c_sz)
        for t in st:
            t["s_bf"] = t["s"].astype(BF16)
            t["v_new"] = (t["u"][rows] - _dot(t["w"][rows], t["s_bf"])).astype(BF16)
        for t in st:
            t["outs"][c] = _dot(t["q_dec"][rows], t["s_bf"]) + _dot(t["intra"][rows, rows], t["v_new"])
            t["s"] = (t["s"] * jnp.exp(t["gl"][c * c_sz:c * c_sz + 1, :])
                      + _dot(t["k_dec_t"][:, rows], t["v_new"]))
    for hh, t in enumerate(st):
        s_scr[hh] = t["s"]
        o_ref[0, :, t["cols"]] = jnp.concatenate(t["outs"], axis=0)


def _gdn_scan(qkv, gates, gates_t, a_log_row, dt_row, a_log_col, dt_col, n_lat, rev):
    bsz, nt, _ = qkv.shape
    hps = GDN_HEADS_PER_STEP
    hblocks = GDN_HEADS // hps
    n_groups = nt // ROW_TILE
    lat_groups = n_lat // ROW_TILE
    width = hps * GDN_HEAD_DIM

    def group(i):
        lat = (lat_groups - i) if rev else (i - 1)
        return jnp.where(i == 0, n_groups - 1, lat)

    return pl.pallas_call(
        functools.partial(_gdn_body, rev=rev, n_lat_groups=lat_groups),
        grid=(bsz, hblocks, n_groups),
        in_specs=[pl.BlockSpec((1, ROW_TILE, width), lambda b, h, i: (b, group(i), h)),
                  pl.BlockSpec((1, ROW_TILE, width), lambda b, h, i: (b, group(i), hblocks + h)),
                  pl.BlockSpec((1, ROW_TILE, width), lambda b, h, i: (b, group(i), 2 * hblocks + h)),
                  pl.BlockSpec((1, ROW_TILE, LANES), lambda b, h, i: (b, group(i), 0)),
                  pl.BlockSpec((1, 4 * GDN_HEADS, ROW_TILE), lambda b, h, i: (b, 0, group(i))),
                  pl.BlockSpec((1, LANES), lambda b, h, i: (0, 0)),
                  pl.BlockSpec((1, LANES), lambda b, h, i: (0, 0)),
                  pl.BlockSpec((4 * GDN_HEADS, 1), lambda b, h, i: (0, 0)),
                  pl.BlockSpec((4 * GDN_HEADS, 1), lambda b, h, i: (0, 0))],
        out_specs=pl.BlockSpec((1, ROW_TILE, width), lambda b, h, i: (b, group(i), h)),
        out_shape=jax.ShapeDtypeStruct((bsz, nt, GDN_HEADS * GDN_HEAD_DIM), F32),
        scratch_shapes=[pltpu.VMEM((hps, GDN_HEAD_DIM, GDN_HEAD_DIM), F32),
                        pltpu.VMEM((4 * GDN_HEADS, ROW_TILE), F32)],
        compiler_params=_cparams("arbitrary", "arbitrary", "arbitrary"),
        name="gdn_scan_bwd" if rev else "gdn_scan_fwd",
    )(qkv, qkv, qkv, gates, gates_t, a_log_row, dt_row, a_log_col, dt_col)


def _gdn_out_body(of_ref, ob_ref, z_ref, gn_ref, w_ref, h_ref, mod_ref, g_ref, g_ffn_ref, wr_ref,
                  out_ref, u_ref, aff_ref, a_scr):
    for hh in range(GDN_HEADS):
        cols = slice(hh * GDN_HEAD_DIM, (hh + 1) * GDN_HEAD_DIM)
        o = of_ref[0, :, cols] + ob_ref[0, :, cols]
        a_scr[:, cols] = (_rms(o, gn_ref[...]) * _silu(z_ref[0, :, cols].astype(F32))).astype(BF16)
    y = _dot(a_scr[...], w_ref[...])
    _residual_and_route(h_ref[0], y, mod_ref[0, 0], g_ref, g_ffn_ref, wr_ref, out_ref, u_ref, aff_ref)


def _gdn_out(o_f, o_b, p, g_onorm, w_bf, h, mods, g_post, g_ffn, w_router, n_lat):
    bsz, _, d = h.shape
    width = GDN_HEADS * GDN_HEAD_DIM
    r_in, r_out, r_shape = _route_specs(bsz, n_lat, d)
    return pl.pallas_call(
        _gdn_out_body,
        grid=(bsz, n_lat // ROW_TILE),
        in_specs=[pl.BlockSpec((1, ROW_TILE, width), lambda b, i: (b, i, 0)),
                  pl.BlockSpec((1, ROW_TILE, width), lambda b, i: (b, i, 0)),
                  pl.BlockSpec((1, ROW_TILE, width), lambda b, i: (b, i, 3)),
                  pl.BlockSpec((1, GDN_HEAD_DIM), lambda b, i: (0, 0)),
                  pl.BlockSpec((width, d), lambda b, i: (0, 0)),
                  pl.BlockSpec((1, ROW_TILE, d), lambda b, i: (b, i, 0)),
                  pl.BlockSpec((1, 1, N_MOD, d), lambda b, i: (b, 0, 0, 0)),
                  pl.BlockSpec((1, d), lambda b, i: (0, 0))] + r_in,
        out_specs=r_out,
        out_shape=r_shape,
        scratch_shapes=[pltpu.VMEM((ROW_TILE, width), BF16)],
        compiler_params=_cparams("arbitrary", "arbitrary"),
        name="gdn_out",
    )(o_f, o_b, p, g_onorm[None, :], w_bf, h, mods, g_post[None, :], g_ffn[None, :], _pad_lanes(w_router))


def _gate_vectors(a_log_f, dt_bias_f, a_log_b, dt_bias_b):
    z = jnp.zeros((GDN_HEADS,), F32)
    a = jnp.concatenate([a_log_f, z, a_log_b, z])
    d = jnp.concatenate([dt_bias_f, z, dt_bias_b, z])
    pad = jnp.zeros((LANES - 4 * GDN_HEADS,), F32)
    return (jnp.concatenate([a, pad])[None, :], jnp.concatenate([d, pad])[None, :], a[:, None], d[:, None])


def kernel(x, c, ctx, c_ctx, l0_w_ada, l0_b_ada, l0_g_pre_mix, l0_g_post_mix, l0_g_pre_ffn, l0_g_post_ffn, l0_w_qkv, l0_lambda_q1, l0_lambda_k1, l0_lambda_q2, l0_lambda_k2, l0_g_subln, l0_w_o, l0_w_router, l0_w_gate, l0_w_up, l0_w_down, l1_w_ada, l1_b_ada, l1_g_pre_mix, l1_g_post_mix, l1_g_pre_ffn, l1_g_post_ffn, l1_w_in, l1_w_conv, l1_a_log_f, l1_dt_bias_f, l1_a_log_b, l1_dt_bias_b, l1_g_onorm, l1_w_o, l1_w_router, l1_w_gate, l1_w_up, l1_w_down):
    n_lat, n_ctx = x.shape[1], ctx.shape[1]
    h = jnp.concatenate([x, ctx], axis=1)

    mods = _ada_mods(c, c_ctx, l0_w_ada, l0_b_ada)
    qkv = _project(h, mods, l0_g_pre_mix, l0_w_qkv.astype(BF16), n_lat, rope_tables=_rope_tables(n_lat, n_ctx))
    lam_vecs = jnp.stack([l0_lambda_q1, l0_lambda_k1, l0_lambda_q2, l0_lambda_k2])
    o = _diff_attention(qkv, lam_vecs, l0_g_subln, n_lat, depth=0)
    h, u, aff = _out_project(o, l0_w_o.astype(BF16), h, mods, l0_g_post_mix, l0_g_pre_ffn, l0_w_router, n_lat)
    h = _moe(h, u, aff, mods, l0_g_post_ffn, l0_w_gate, l0_w_up, l0_w_down, n_lat, n_ctx)

    mods = _ada_mods(c, c_ctx, l1_w_ada, l1_b_ada)
    width = GDN_HEADS * GDN_HEAD_DIM
    w_gates = jnp.pad(l1_w_in[:, 4 * width:], ((0, 0), (0, LANES - 4 * GDN_HEADS))).astype(BF16)
    p, gates = _project(h, mods, l1_g_pre_mix, l1_w_in[:, :4 * width].astype(BF16), n_lat, w_gates=w_gates)
    gates_t = jnp.swapaxes(gates[:, :, :4 * GDN_HEADS], 1, 2)
    qkv = _short_conv(p, l1_w_conv, n_lat)
    gv = _gate_vectors(l1_a_log_f, l1_dt_bias_f, l1_a_log_b, l1_dt_bias_b)
    o_f = _gdn_scan(qkv, gates, gates_t, *gv, n_lat, rev=False)
    o_b = _gdn_scan(qkv, gates, gates_t, *gv, n_lat, rev=True)
    hl, u, aff = _gdn_out(o_f, o_b, p, l1_g_onorm, l1_w_o.astype(BF16), h, mods, l1_g_post_mix, l1_g_pre_ffn,
                          l1_w_router, n_lat)
    return _moe(hl, u, aff, mods, l1_g_post_ffn, l1_w_gate, l1_w_up, l1_w_down, n_lat, 0)
```

```python
import functools
import math

import jax
import jax.numpy as jnp
from jax import lax
from jax.experimental import pallas as pl
from jax.experimental.pallas import tpu as pltpu

F32 = jnp.float32
BF16 = jnp.bfloat16
HIGHEST = lax.Precision.HIGHEST

EPS = 1e-6
N_MOD = 6
GRID_W = 64
ROPE_BASE = 10000.0
DA_HEADS = 8
DA_HEAD_DIM = 64
GDN_HEADS = 8
GDN_HEAD_DIM = 128
CONV_K = 5
CONV_HALO = 8
N_EXPERTS = 16
CAP_FACTOR = 2

LANES = 128
ROW_TILE = 256
PROJ_ROWS = 768
ATTN_Q_ROWS = 1024
ATTN_CHAIN_ROWS = 128
Q_PRESCALE = DA_HEAD_DIM ** -0.5 * math.log2(math.e)
GDN_CHUNK = 64
GDN_HEADS_PER_STEP = 8
FF_CHUNK = 1024
SLOT_WINDOW = 64
SLOT_ALIGN = 16
TABLE_COUNT = 64
VMEM_LIMIT = 56 * 1024 * 1024

SH_M, SC_M, GT_M, SH_F, SC_F, GT_F = range(6)


def _cparams(*sem):
    return pltpu.CompilerParams(dimension_semantics=sem, vmem_limit_bytes=VMEM_LIMIT)


def _dot(a, b):
    return jnp.dot(a, b, preferred_element_type=F32)


def _dot_nt(a, b, precision=None):
    return lax.dot_general(a, b, (((1,), (1,)), ((), ())), preferred_element_type=F32, precision=precision)


def _dot_tn(a, b):
    return lax.dot_general(a, b, (((0,), (0,)), ((), ())), preferred_element_type=F32)


def _rms(x, g):
    return x * lax.rsqrt(jnp.mean(x * x, axis=-1, keepdims=True) + EPS) * g


def _sigmoid(x):
    return 1.0 / (1.0 + jnp.exp(-x))


def _silu(x):
    return x * _sigmoid(x)


def _softplus(x):
    return jnp.maximum(x, 0.0) + jnp.log(1.0 + jnp.exp(-jnp.abs(x)))


def _ada_body(c_ref, w_ref, b_ref, o_ref):
    o_ref[...] = jnp.dot(_silu(c_ref[...]), w_ref[...], precision=HIGHEST, preferred_element_type=F32) + b_ref[...]


def _ada_mods(c, c_ctx, w_ada, b_ada):
    bsz, d = c.shape
    rows = 16
    cc = jnp.concatenate([c, c_ctx[None, :], jnp.zeros((rows - bsz - 1, d), F32)], axis=0)
    tn = 1024
    m = pl.pallas_call(
        _ada_body,
        grid=(N_MOD * d // tn,),
        in_specs=[pl.BlockSpec((rows, d), lambda j: (0, 0)),
                  pl.BlockSpec((d, tn), lambda j: (0, j)),
                  pl.BlockSpec((1, tn), lambda j: (0, j))],
        out_specs=pl.BlockSpec((rows, tn), lambda j: (0, j)),
        out_shape=jax.ShapeDtypeStruct((rows, N_MOD * d), F32),
        compiler_params=_cparams("arbitrary"),
        name="ada_mods",
    )(cc, w_ada, b_ada[None, :])
    lat = m[:bsz].reshape(bsz, 1, N_MOD, d)
    ctx = jnp.broadcast_to(m[bsz].reshape(1, 1, N_MOD, d), (bsz, 1, N_MOD, d))
    return jnp.concatenate([lat, ctx], axis=1)


def _proj_body(h_ref, mod_ref, g_ref, w_ref, *rest, n_lat, rope, gates):
    rest = list(rest)
    cos_ref, sin_ref = (rest.pop(0), rest.pop(0)) if rope else (None, None)
    wg_ref = rest.pop(0) if gates else None
    o_ref = rest.pop(0)
    og_ref = rest.pop(0) if gates else None
    x = h_ref[0]
    tm, d = x.shape
    row = pl.program_id(1) * tm + lax.broadcasted_iota(jnp.int32, (tm, 1), 0)
    is_ctx = row >= n_lat
    scale = jnp.where(is_ctx, mod_ref[0, 1, SC_M:SC_M + 1, :], mod_ref[0, 0, SC_M:SC_M + 1, :])
    shift = jnp.where(is_ctx, mod_ref[0, 1, SH_M:SH_M + 1, :], mod_ref[0, 0, SH_M:SH_M + 1, :])
    u = (_rms(x, g_ref[...]) * (1.0 + scale) + shift).astype(BF16)
    if gates:
        og_ref[0] = _dot(u, wg_ref[...])
    for j in range(w_ref.shape[1] // d):
        cols = slice(j * d, (j + 1) * d)
        acc = _dot(u, w_ref[:, cols])
        if rope and j < 2:
            reps = d // LANES
            cos = jnp.tile(cos_ref[...], (1, reps))
            sin = jnp.tile(sin_ref[...], (1, reps))
            lane = lax.broadcasted_iota(jnp.int32, acc.shape, 1)
            half = DA_HEAD_DIM // 4
            first = (lane % (2 * half)) < half
            partner = jnp.where(first, pltpu.roll(acc, d - half, 1), pltpu.roll(acc, half, 1))
            acc = acc * cos + partner * sin
            if j == 0:
                acc = acc * Q_PRESCALE
        o_ref[0, :, cols] = acc.astype(o_ref.dtype)


def _project(h, mods, g, w_bf, n_lat, rope_tables=None, w_gates=None):
    bsz, nt, d = h.shape
    n_out = w_bf.shape[1]
    tm = PROJ_ROWS
    assert nt % tm == 0 and n_lat % ROW_TILE == 0 and (nt - n_lat) == ROW_TILE
    rope = rope_tables is not None
    gates = w_gates is not None
    in_specs = [pl.BlockSpec((1, tm, d), lambda b, i: (b, i, 0)),
                pl.BlockSpec((1, 2, N_MOD, d), lambda b, i: (b, 0, 0, 0)),
                pl.BlockSpec((1, d), lambda b, i: (0, 0)),
                pl.BlockSpec((d, n_out), lambda b, i: (0, 0))]
    args = [h, mods, g[None, :], w_bf]
    out_specs = [pl.BlockSpec((1, tm, n_out), lambda b, i: (b, i, 0))]
    out_shape = [jax.ShapeDtypeStruct((bsz, nt, n_out), BF16)]
    if rope:
        in_specs += [pl.BlockSpec((tm, LANES), lambda b, i: (i, 0))] * 2
        args += list(rope_tables)
    if gates:
        in_specs.append(pl.BlockSpec((d, LANES), lambda b, i: (0, 0)))
        args.append(w_gates)
        out_specs.append(pl.BlockSpec((1, tm, LANES), lambda b, i: (b, i, 0)))
        out_shape.append(jax.ShapeDtypeStruct((bsz, nt, LANES), F32))
    out = pl.pallas_call(
        functools.partial(_proj_body, n_lat=n_lat, rope=rope, gates=gates),
        grid=(bsz, nt // tm),
        in_specs=in_specs,
        out_specs=out_specs,
        out_shape=out_shape,
        compiler_params=_cparams("arbitrary", "arbitrary"),
        name="mod_project",
    )(*args)
    return out if gates else out[0]


def _rope_tables(n_lat, n_ctx):
    rows = n_lat // GRID_W
    r = jnp.repeat(jnp.arange(rows), GRID_W).astype(F32)
    col = jnp.tile(jnp.arange(GRID_W), rows).astype(F32)
    half = DA_HEAD_DIM // 2
    inv = ROPE_BASE ** (-jnp.arange(0, half, 2, dtype=F32) / half)
    ang_r, ang_c = r[:, None] * inv, col[:, None] * inv
    cos = jnp.concatenate([jnp.cos(ang_r)] * 2 + [jnp.cos(ang_c)] * 2, axis=-1)
    sin = jnp.concatenate([-jnp.sin(ang_r), jnp.sin(ang_r), -jnp.sin(ang_c), jnp.sin(ang_c)], axis=-1)
    cos = jnp.concatenate([jnp.tile(cos, (1, 2)), jnp.ones((n_ctx, LANES), F32)], axis=0)
    sin = jnp.concatenate([jnp.tile(sin, (1, 2)), jnp.zeros((n_ctx, LANES), F32)], axis=0)
    return cos, sin


def _attn_body(lam_ref, q_ref, k_ref, v_ref, gs_ref, o_ref, *, lam_init):
    lv = lam_ref[...]
    lam = (jnp.exp(jnp.sum(lv[0:1] * lv[1:2], axis=-1, keepdims=True))
           - jnp.exp(jnp.sum(lv[2:3] * lv[3:4], axis=-1, keepdims=True)) + lam_init)
    hw = 2 * DA_HEAD_DIM
    n_q = q_ref.shape[1]
    lane = lax.broadcasted_iota(jnp.int32, (n_q, hw), 1)
    rows = min(n_q, ATTN_CHAIN_ROWS)
    chains = [(hh, r) for hh in range(q_ref.shape[2] // hw) for r in range(0, n_q, rows)]
    lhs, v_one = {}, {}
    for hh in range(q_ref.shape[2] // hw):
        cols = slice(hh * hw, (hh + 1) * hw)
        q, v = q_ref[0, :, cols], v_ref[0, :, cols]
        v_one[hh] = jnp.concatenate([v, jnp.ones_like(v)], axis=1)
        qm = [jnp.where((lane >= c * DA_HEAD_DIM) & (lane < (c + 1) * DA_HEAD_DIM), q, jnp.zeros_like(q))
              for c in range(2)]
        for r in range(0, n_q, rows):
            lhs[hh, r] = jnp.concatenate([qm[0][r:r + rows], qm[1][r:r + rows]], axis=0)
    s = [_dot_nt(lhs[hh, r], k_ref[0, :, hh * hw:(hh + 1) * hw]) for hh, r in chains]
    e = [jnp.exp2((x - jnp.max(x, axis=-1, keepdims=True)).astype(BF16)) for x in s]
    ov = [_dot(x, v_one[hh]) for x, (hh, r) in zip(e, chains)]
    att = [x[:, :hw] / x[:, hw:hw + 1] for x in ov]
    for (hh, r), a in zip(chains, att):
        o = a[:rows] - lam * a[rows:]
        o_ref[0, r:r + rows, hh * hw:(hh + 1) * hw] = (_rms(o, gs_ref[...]) * (1.0 - lam_init)).astype(o_ref.dtype)


def _diff_attention(qkv, lam_vecs, g_subln, n_lat, depth):
    bsz, nt, d3 = qkv.shape
    d = d3 // 3
    hw = 2 * DA_HEAD_DIM
    lam_init = 0.8 - 0.6 * math.exp(-0.3 * depth)
    n_ctx = nt - n_lat

    def call(n_q, tq, q_blk0, n_k, k_blk0, heads):
        hblocks = DA_HEADS // heads
        width = heads * hw
        return pl.pallas_call(
            functools.partial(_attn_body, lam_init=lam_init),
            grid=(bsz, hblocks, n_q // tq),
            in_specs=[pl.BlockSpec((4, DA_HEAD_DIM), lambda b, h, i: (0, 0)),
                      pl.BlockSpec((1, tq, width), lambda b, h, i: (b, q_blk0 + i, h)),
                      pl.BlockSpec((1, n_k, width), lambda b, h, i: (b, k_blk0, hblocks + h)),
                      pl.BlockSpec((1, n_k, width), lambda b, h, i: (b, k_blk0, 2 * hblocks + h)),
                      pl.BlockSpec((1, hw), lambda b, h, i: (0, 0))],
            out_specs=pl.BlockSpec((1, tq, width), lambda b, h, i: (b, i, h)),
            out_shape=jax.ShapeDtypeStruct((bsz, n_q, d), BF16),
            compiler_params=_cparams("arbitrary", "arbitrary", "arbitrary"),
            name="diff_attention",
        )(lam_vecs, qkv, qkv, qkv, g_subln[None, :])

    o_lat = call(n_lat, min(ATTN_Q_ROWS, n_lat), 0, nt, 0, heads=1)
    o_ctx = call(n_ctx, n_ctx, n_lat // n_ctx, n_ctx, n_lat // n_ctx, heads=DA_HEADS)
    return o_lat, o_ctx


def _residual_and_route(h, y, mod, g_post_ref, g_ffn_ref, wr_ref, out_ref, u_ref, aff_ref):
    h_new = h + mod[GT_M:GT_M + 1, :] * _rms(y, g_post_ref[...])
    out_ref[0] = h_new
    u = _rms(h_new, g_ffn_ref[...]) * (1.0 + mod[SC_F:SC_F + 1, :]) + mod[SH_F:SH_F + 1, :]
    u_hi = u.astype(BF16)
    u_ref[0] = u_hi
    u_lo = (u - u_hi.astype(F32)).astype(BF16)
    w = wr_ref[...]
    w_hi = w.astype(BF16)
    w_lo = (w - w_hi.astype(F32)).astype(BF16)
    logits = (_dot(u_hi, w_hi) + _dot(u_hi, w_lo) + _dot(u_lo, w_hi)).T[:N_EXPERTS]
    e = jnp.exp(logits - jnp.max(logits, axis=0, keepdims=True))
    aff_ref[0] = e / jnp.sum(e, axis=0, keepdims=True)


def _pad_lanes(w):
    return jnp.pad(w, ((0, 0), (0, LANES - w.shape[1])))


def _route_specs(bsz, nt, d):
    in_specs = [pl.BlockSpec((1, d), lambda b, i: (0, 0)),
                pl.BlockSpec((d, LANES), lambda b, i: (0, 0))]
    out_specs = [pl.BlockSpec((1, ROW_TILE, d), lambda b, i: (b, i, 0)),
                 pl.BlockSpec((1, ROW_TILE, d), lambda b, i: (b, i, 0)),
                 pl.BlockSpec((1, N_EXPERTS, ROW_TILE), lambda b, i: (b, 0, i))]
    out_shape = [jax.ShapeDtypeStruct((bsz, nt, d), F32),
                 jax.ShapeDtypeStruct((bsz, nt, d), BF16),
                 jax.ShapeDtypeStruct((bsz, N_EXPERTS, nt), F32)]
    return in_specs, out_specs, out_shape


def _oproj_body(o_lat_ref, o_ctx_ref, w_ref, h_ref, mod_ref, g_ref, g_ffn_ref, wr_ref, out_ref, u_ref, aff_ref,
                *, lat_tiles):
    o = jnp.where(pl.program_id(1) < lat_tiles, o_lat_ref[0], o_ctx_ref[0])
    y = _dot(o, w_ref[...])
    _residual_and_route(h_ref[0], y, mod_ref[0, 0], g_ref, g_ffn_ref, wr_ref, out_ref, u_ref, aff_ref)


def _out_project(o_lat, o_ctx, w_bf, h, mods, g_post, g_ffn, w_router, n_lat):
    bsz, nt, d = h.shape
    lat_tiles = n_lat // ROW_TILE
    r_in, r_out, r_shape = _route_specs(bsz, nt, d)
    return pl.pallas_call(
        functools.partial(_oproj_body, lat_tiles=lat_tiles),
        grid=(bsz, nt // ROW_TILE),
        in_specs=[pl.BlockSpec((1, ROW_TILE, d), lambda b, i: (b, jnp.minimum(i, lat_tiles - 1), 0)),
                  pl.BlockSpec((1, ROW_TILE, d), lambda b, i: (b, 0, 0)),
                  pl.BlockSpec((d, d), lambda b, i: (0, 0)),
                  pl.BlockSpec((1, ROW_TILE, d), lambda b, i: (b, i, 0)),
                  pl.BlockSpec((1, 1, N_MOD, d), lambda b, i: (b, i // lat_tiles, 0, 0)),
                  pl.BlockSpec((1, d), lambda b, i: (0, 0))] + r_in,
        out_specs=r_out,
        out_shape=r_shape,
        compiler_params=_cparams("arbitrary", "arbitrary"),
        name="out_project",
    )(o_lat, o_ctx, w_bf, h, mods, g_post[None, :], g_ffn[None, :], _pad_lanes(w_router))


def _lane_cumsum(x):
    n = x.shape[1]
    jj = lax.broadcasted_iota(jnp.int32, (ROW_TILE, ROW_TILE), 0)
    nn = lax.broadcasted_iota(jnp.int32, (ROW_TILE, ROW_TILE), 1)
    tri = (jj <= nn).astype(BF16)
    run = jnp.zeros((x.shape[0], 1), F32)
    parts = []
    for t in range(n // ROW_TILE):
        local = _dot(x[:, t * ROW_TILE:(t + 1) * ROW_TILE].astype(BF16), tri) + run
        parts.append(local)
        run = local[:, ROW_TILE - 1:ROW_TILE]
    return jnp.concatenate(parts, axis=1) if len(parts) > 1 else parts[0]


def _select_slots(aff, cap):
    bits = pltpu.bitcast(aff, jnp.int32)

    def enough(cand):
        return jnp.sum((bits >= cand).astype(F32), axis=1, keepdims=True) >= cap

    def step(t, lo):
        hi_bit = jnp.left_shift(jnp.int32(1), 29 - 2 * t)
        lo_bit = jnp.left_shift(jnp.int32(1), 28 - 2 * t)
        both, first, second = lo | hi_bit | lo_bit, lo | hi_bit, lo | lo_bit
        return jnp.where(enough(both), both, jnp.where(enough(first), first, jnp.where(enough(second), second, lo)))

    top = jnp.full((aff.shape[0], 1), 1 << 30, jnp.int32)
    thr = jnp.where(enough(top), top, 0)
    thr = lax.fori_loop(0, 15, step, thr)
    gt = bits > thr
    eq = bits == thr
    need = cap - jnp.sum(gt.astype(F32), axis=1, keepdims=True)
    eq_rank = _lane_cumsum(eq.astype(F32))
    sel = gt | (eq & (eq_rank <= need))
    sel_f = sel.astype(F32)
    slot = _lane_cumsum(sel_f) - 1.0
    return jnp.where(sel, slot, -1.0).astype(jnp.int32), sel_f


def _tile_table(sel_f, tile0):
    lane = lax.broadcasted_iota(jnp.int32, (sel_f.shape[0], LANES), 1)
    run = jnp.zeros((sel_f.shape[0], 1), F32)
    tab = jnp.zeros((sel_f.shape[0], LANES), F32)
    for t in range(sel_f.shape[1] // ROW_TILE):
        inside = jnp.sum(sel_f[:, t * ROW_TILE:(t + 1) * ROW_TILE], axis=1, keepdims=True)
        tab = jnp.where(lane == tile0 + t, run, tab)
        tab = jnp.where(lane == TABLE_COUNT + tile0 + t, inside, tab)
        run = run + inside
    return tab


def _select_body(aff_ref, pos_ref, tab_ref, *, n_lat, n_ctx):
    aff = aff_ref[0]
    pos, sel_f = _select_slots(aff[:, :n_lat], CAP_FACTOR * n_lat // N_EXPERTS)
    pos_ref[0, :, :n_lat] = pos
    tab = _tile_table(sel_f, 0)
    if n_ctx:
        pos, sel_f = _select_slots(aff[:, n_lat:], CAP_FACTOR * n_ctx // N_EXPERTS)
        pos_ref[0, :, n_lat:] = pos
        tab = tab + _tile_table(sel_f, n_lat // ROW_TILE)
    tab_ref[0] = tab.astype(jnp.int32)


def _select(aff, n_lat, n_ctx):
    bsz, e, nt = aff.shape
    n_tiles = nt // ROW_TILE
    pos, tab = pl.pallas_call(
        functools.partial(_select_body, n_lat=n_lat, n_ctx=n_ctx),
        grid=(bsz,),
        in_specs=[pl.BlockSpec((1, e, nt), lambda b: (b, 0, 0))],
        out_specs=[pl.BlockSpec((1, e, nt), lambda b: (b, 0, 0)),
                   pl.BlockSpec((1, e, LANES), lambda b: (b, 0, 0))],
        out_shape=[jax.ShapeDtypeStruct((bsz, e, nt), jnp.int32),
                   jax.ShapeDtypeStruct((bsz, e, LANES), jnp.int32)],
        compiler_params=_cparams("arbitrary"),
        name="moe_select",
    )(aff)
    first = tab[:, :, :n_tiles].reshape(-1)
    count = tab[:, :, TABLE_COUNT:TABLE_COUNT + n_tiles].reshape(-1)
    return pos, first, count


class _TileWindows:
    def __init__(self, first_ref, count_ref, pos_ref, n_lat, rows, lat_cap):
        b, tile = pl.program_id(0), pl.program_id(1)
        n_exp = pos_ref.shape[1]
        self.rows, self.n_exp = rows, n_exp
        base = jnp.where(tile * ROW_TILE >= n_lat, lat_cap, 0)
        pos = pos_ref[0]
        self.slot = jnp.where(pos >= 0, pos + base, -1)
        self.starts, self.rounds = [], 0
        for x in range(n_exp):
            at = (b * n_exp + x) * pl.num_programs(1) + tile
            first = first_ref[at]
            start = base + (first // SLOT_ALIGN) * SLOT_ALIGN
            self.starts.append(start)
            self.rounds = jnp.maximum(self.rounds, (base + first + count_ref[at] - start + SLOT_WINDOW - 1) // SLOT_WINDOW)
        self.ids0 = lax.broadcasted_iota(jnp.int32, (SLOT_WINDOW, ROW_TILE), 0)

    def window(self, x, r):
        want = self.starts[x] + r * SLOT_WINDOW
        s0 = jnp.minimum(want, self.rows - SLOT_WINDOW)
        ids = self.ids0 + s0
        hot = (self.slot[x:x + 1, :] == ids) & (ids >= want)
        return pl.multiple_of(x * self.rows + s0, SLOT_ALIGN), hot


def _as_bf16(mask):
    return jnp.where(mask, 1.0, 0.0).astype(BF16)


def _gather_body(first_ref, count_ref, pos_ref, aff_ref, u_ref, xs_ref, gate_ref, *, n_lat, rows, lat_cap):
    @pl.when(pl.program_id(1) == 0)
    def _():
        xs_ref[...] = jnp.zeros_like(xs_ref)
        gate_ref[...] = jnp.zeros_like(gate_ref)

    tw = _TileWindows(first_ref, count_ref, pos_ref, n_lat, rows, lat_cap)
    aff = aff_ref[0]
    u = u_ref[0]

    def one_round(r, carry):
        wins = [tw.window(x, r) for x in range(tw.n_exp)]
        part = _dot(jnp.concatenate([_as_bf16(hot) for _, hot in wins], axis=0), u)
        for x, (off, hot) in enumerate(wins):
            dst = pl.ds(off, SLOT_WINDOW)
            xs_ref[0, dst, :] = xs_ref[0, dst, :] + part[x * SLOT_WINDOW:(x + 1) * SLOT_WINDOW].astype(BF16)
            gate = jnp.sum(jnp.where(hot, aff[x:x + 1, :], 0.0), axis=1, keepdims=True)
            gate_ref[0, dst, :] = gate_ref[0, dst, :] + jnp.broadcast_to(gate, (SLOT_WINDOW, LANES))
        return carry

    lax.fori_loop(0, tw.rounds, one_round, 0)


def _slot_rows(n_lat, n_ctx):
    return CAP_FACTOR * n_lat // N_EXPERTS + CAP_FACTOR * n_ctx // N_EXPERTS


def _gather(pos, first, count, aff, u, n_lat, n_ctx):
    bsz, nt, d = u.shape
    e = pos.shape[1]
    rows = _slot_rows(n_lat, n_ctx)
    xs, gate = pl.pallas_call(
        functools.partial(_gather_body, n_lat=n_lat, rows=rows, lat_cap=CAP_FACTOR * n_lat // N_EXPERTS),
        grid_spec=pltpu.PrefetchScalarGridSpec(
            num_scalar_prefetch=2,
            grid=(bsz, nt // ROW_TILE),
            in_specs=[pl.BlockSpec((1, e, ROW_TILE), lambda b, i, *_: (b, 0, i)),
                      pl.BlockSpec((1, e, ROW_TILE), lambda b, i, *_: (b, 0, i)),
                      pl.BlockSpec((1, ROW_TILE, d), lambda b, i, *_: (b, i, 0))],
            out_specs=[pl.BlockSpec((1, e * rows, d), lambda b, i, *_: (b, 0, 0)),
                       pl.BlockSpec((1, e * rows, LANES), lambda b, i, *_: (b, 0, 0))]),
        out_shape=[jax.ShapeDtypeStruct((bsz, e * rows, d), BF16),
                   jax.ShapeDtypeStruct((bsz, e * rows, LANES), F32)],
        compiler_params=_cparams("arbitrary", "arbitrary"),
        name="moe_gather",
    )(first, count, pos, aff, u)
    return xs.reshape(bsz, e, rows, d), gate.reshape(bsz, e, rows, LANES)


def _expert_body(xs_ref, gate_ref, wg_hbm, wu_hbm, wd_hbm, y_ref, wg_scr, wu_scr, wd_scr,
                 wg_stage, wu_stage, wd_stage, sems):
    x, b = pl.program_id(0), pl.program_id(1)
    n_exp, n_chunks = pl.num_programs(0), pl.num_programs(1)
    rows_in, rows_mid = wg_stage.shape[0], wd_stage.shape[0]

    def chunk_copies(expert, chunk):
        return (pltpu.make_async_copy(wg_hbm.at[expert, pl.ds(chunk * rows_in, rows_in), :], wg_stage, sems.at[0]),
                pltpu.make_async_copy(wu_hbm.at[expert, pl.ds(chunk * rows_in, rows_in), :], wu_stage, sems.at[1]),
                pltpu.make_async_copy(wd_hbm.at[expert, pl.ds(chunk * rows_mid, rows_mid), :], wd_stage, sems.at[2]))

    def land(copy, chunk):
        at_in = pl.ds(pl.multiple_of(chunk * rows_in, SLOT_ALIGN), rows_in)
        at_mid = pl.ds(pl.multiple_of(chunk * rows_mid, SLOT_ALIGN), rows_mid)
        wg_scr[copy, at_in, :] = wg_stage[...].astype(BF16)
        wu_scr[copy, at_in, :] = wu_stage[...].astype(BF16)
        wd_scr[copy, at_mid, :] = wd_stage[...].astype(BF16)

    @pl.when((x == 0) & (b == 0))
    def _():
        def fetch(chunk, carry):
            copies = chunk_copies(0, chunk)
            for cp in copies:
                cp.start()
            for cp in copies:
                cp.wait()
            land(0, chunk)
            return carry

        lax.fori_loop(0, n_chunks, fetch, 0)

    def start_next(_, carry):
        for cp in chunk_copies(x + 1, b):
            cp.start()
        return carry

    lax.fori_loop(0, jnp.where(x + 1 < n_exp, 1, 0), start_next, 0)

    cur = x % 2
    xs = xs_ref[0, 0]
    acc = jnp.zeros((xs.shape[0], wd_scr.shape[2]), F32)
    for s in range(wg_scr.shape[2] // FF_CHUNK):
        cols = slice(s * FF_CHUNK, (s + 1) * FF_CHUNK)
        hid = _silu(_dot(xs, wg_scr[cur, :, cols])) * _dot(xs, wu_scr[cur, :, cols])
        acc = acc + _dot(hid.astype(BF16), wd_scr[cur, cols, :])
    y_ref[0, 0] = (acc * gate_ref[0, 0, :, 0:1]).astype(y_ref.dtype)

    @pl.when(x + 1 < n_exp)
    def _():
        for cp in chunk_copies(x + 1, b):
            cp.wait()
        land(1 - cur, b)


def _experts(xs, gate, wg, wu, wd):
    bsz, e, rows, d = xs.shape
    ff = wg.shape[2]
    return pl.pallas_call(
        _expert_body,
        grid=(e, bsz),
        in_specs=[pl.BlockSpec((1, 1, rows, d), lambda x, b: (b, x, 0, 0)),
                  pl.BlockSpec((1, 1, rows, LANES), lambda x, b: (b, x, 0, 0)),
                  pl.BlockSpec(memory_space=pl.ANY),
                  pl.BlockSpec(memory_space=pl.ANY),
                  pl.BlockSpec(memory_space=pl.ANY)],
        out_specs=pl.BlockSpec((1, 1, rows, d), lambda x, b: (b, x, 0, 0)),
        out_shape=jax.ShapeDtypeStruct((bsz, e, rows, d), BF16),
        scratch_shapes=[pltpu.VMEM((2, d, ff), BF16), pltpu.VMEM((2, d, ff), BF16), pltpu.VMEM((2, ff, d), BF16),
                        pltpu.VMEM((d // bsz, ff), F32), pltpu.VMEM((d // bsz, ff), F32),
                        pltpu.VMEM((ff // bsz, d), F32), pltpu.SemaphoreType.DMA((3,))],
        compiler_params=_cparams("arbitrary", "arbitrary"),
        name="moe_experts",
    )(xs, gate, wg, wu, wd)


def _combine_body(first_ref, count_ref, pos_ref, y_ref, h_ref, mod_ref, g_ref, out_ref, acc_scr,
                  *, n_lat, rows, lat_cap):
    tw = _TileWindows(first_ref, count_ref, pos_ref, n_lat, rows, lat_cap)
    acc_scr[...] = jnp.zeros_like(acc_scr)

    def one_round(r, carry):
        group = ROW_TILE // SLOT_WINDOW
        for g0 in range(0, tw.n_exp, group):
            wins = [tw.window(x, r) for x in range(g0, g0 + group)]
            hot = jnp.concatenate([_as_bf16(h) for _, h in wins], axis=0)
            rows_y = jnp.concatenate([y_ref[0, pl.ds(off, SLOT_WINDOW), :] for off, _ in wins], axis=0)
            acc_scr[...] += _dot_tn(hot, rows_y)
        return carry

    lax.fori_loop(0, tw.rounds, one_round, 0)
    out_ref[0] = h_ref[0] + mod_ref[0, 0, GT_F:GT_F + 1, :] * _rms(acc_scr[...], g_ref[...])


def _combine(pos, first, count, y, h, mods, g_post, n_lat, n_ctx):
    bsz, e, rows, d = y.shape
    nt = n_lat + n_ctx
    lat_tiles = n_lat // ROW_TILE
    return pl.pallas_call(
        functools.partial(_combine_body, n_lat=n_lat, rows=rows, lat_cap=CAP_FACTOR * n_lat // N_EXPERTS),
        grid_spec=pltpu.PrefetchScalarGridSpec(
            num_scalar_prefetch=2,
            grid=(bsz, nt // ROW_TILE),
            in_specs=[pl.BlockSpec((1, e, ROW_TILE), lambda b, i, *_: (b, 0, i)),
                      pl.BlockSpec((1, e * rows, d), lambda b, i, *_: (b, 0, 0)),
                      pl.BlockSpec((1, ROW_TILE, d), lambda b, i, *_: (b, i, 0)),
                      pl.BlockSpec((1, 1, N_MOD, d), lambda b, i, *_: (b, i // lat_tiles, 0, 0)),
                      pl.BlockSpec((1, d), lambda b, i, *_: (0, 0))],
            out_specs=pl.BlockSpec((1, ROW_TILE, d), lambda b, i, *_: (b, i, 0)),
            scratch_shapes=[pltpu.VMEM((ROW_TILE, d), F32)]),
        out_shape=jax.ShapeDtypeStruct((bsz, nt, d), F32),
        compiler_params=_cparams("arbitrary", "arbitrary"),
        name="moe_combine",
    )(first, count, pos, y.reshape(bsz, e * rows, d), h, mods, g_post[None, :])


def _moe(h, u, aff, mods, g_post, wg, wu, wd, n_lat, n_ctx):
    pos, first, count = _select(aff, n_lat, n_ctx)
    xs, gate = _gather(pos, first, count, aff, u, n_lat, n_ctx)
    y = _experts(xs, gate, wg, wu, wd)
    return _combine(pos, first, count, y, h, mods, g_post, n_lat, n_ctx)


def _conv_body(x_ref, w_ref, o_ref, pad_scr, *, n_lat):
    j = pl.program_id(1)
    nt = x_ref.shape[1]
    halo = CONV_HALO
    zeros = jnp.zeros((halo, LANES), F32)
    for lo, hi in ((0, n_lat), (n_lat, nt)):
        n = hi - lo
        if n == 0:
            continue
        pad_scr[0:halo, :] = zeros
        pad_scr[halo:halo + n, :] = x_ref[0, lo:hi, :].astype(F32)
        pad_scr[halo + n:2 * halo + n, :] = zeros
        acc = None
        for tap in range(CONV_K):
            start = halo + tap - CONV_K // 2
            term = pad_scr[start:start + n, :] * w_ref[tap:tap + 1, :]
            acc = term if acc is None else acc + term
        y = _silu(acc)
        inv_norm = lax.rsqrt(jnp.sum(y * y, axis=-1, keepdims=True) + EPS)
        factor = jnp.where(j < GDN_HEADS, inv_norm * GDN_HEAD_DIM ** -0.5,
                           jnp.where(j < 2 * GDN_HEADS, inv_norm, 1.0))
        o_ref[0, lo:hi, :] = y * factor


def _short_conv(p, w_conv, n_lat):
    bsz, nt, _ = p.shape
    nblk = 3 * GDN_HEADS
    return pl.pallas_call(
        functools.partial(_conv_body, n_lat=n_lat),
        grid=(bsz, nblk),
        in_specs=[pl.BlockSpec((1, nt, LANES), lambda b, j: (b, 0, j)),
                  pl.BlockSpec((CONV_K, LANES), lambda b, j: (0, j))],
        out_specs=pl.BlockSpec((1, nt, LANES), lambda b, j: (b, 0, j)),
        out_shape=jax.ShapeDtypeStruct((bsz, nt, nblk * LANES), F32),
        scratch_shapes=[pltpu.VMEM((n_lat + 2 * CONV_HALO, LANES), F32)],
        compiler_params=_cparams("arbitrary", "arbitrary"),
        name="gdn_conv",
    )(p, w_conv)


def _gdn_body(q_ref, k_ref, v_ref, gate_ref, gate_t_ref, alog_ref, dt_ref, alog_t_ref, dt_t_ref,
              o_ref, s_scr, gct_scr, *, rev, n_lat_groups):
    hb = pl.program_id(1)
    step = pl.program_id(2)
    r_t = ROW_TILE
    c_sz = GDN_CHUNK
    goff = 2 * GDN_HEADS if rev else 0

    @pl.when(step == 0)
    def _():
        s_scr[...] = jnp.zeros_like(s_scr)

    ii = lax.broadcasted_iota(jnp.int32, (r_t, r_t), 0)
    jj = lax.broadcasted_iota(jnp.int32, (r_t, r_t), 1)
    same = (ii // c_sz) == (jj // c_sz)
    if rev:
        incl, strict = same & (jj >= ii), same & (jj > ii)
    else:
        incl, strict = same & (jj <= ii), same & (jj < ii)
    m_incl = incl.astype(BF16)
    m_same = same.astype(BF16)
    eye = (ii == jj).astype(F32)

    def pieces(x):
        hi = x.astype(BF16)
        rest = x - hi.astype(F32)
        mid = rest.astype(BF16)
        return hi, mid, (rest - mid.astype(F32)).astype(BF16)

    lane = lax.broadcasted_iota(jnp.int32, (1, LANES), 1)
    is_g = (lane >= goff) & (lane < goff + GDN_HEADS)
    raw = gate_ref[0]
    g_all = jnp.where(is_g, -jnp.exp(alog_ref[...]) * _softplus(raw + dt_ref[...]), 0.0)
    beta_all = _sigmoid(raw)
    g_parts = pieces(g_all)
    gc_all = sum(_dot(m_incl, x) for x in g_parts)
    gl_all = sum(_dot(m_same, x) for x in g_parts)
    g_t = -jnp.exp(alog_t_ref[...]) * _softplus(gate_t_ref[0] + dt_t_ref[...])
    gct_scr[...] = sum(_dot_nt(x, m_incl) for x in pieces(g_t))
    lane_f = lax.broadcasted_iota(jnp.int32, (r_t, LANES), 1)

    def packed(m):
        return sum(m[c * c_sz:(c + 1) * c_sz] for c in range(r_t // c_sz))

    def blockdiag(p):
        return jnp.where(same, jnp.tile(p, (r_t // c_sz, 1)), 0.0)

    def column(a, idx):
        return jnp.sum(jnp.where(lane_f == idx, a, 0.0), axis=1, keepdims=True)

    heads = range(GDN_HEADS_PER_STEP)
    n_chunks = r_t // c_sz
    n_levels = int(math.log2(c_sz)) - 1
    st = []
    for hh in heads:
        head = hb * GDN_HEADS_PER_STEP + hh
        cols = slice(hh * GDN_HEAD_DIM, (hh + 1) * GDN_HEAD_DIM)
        q, k, v = q_ref[0, :, cols], k_ref[0, :, cols], v_ref[0, :, cols]
        gc = column(gc_all, goff + head)
        gl = column(gl_all, goff + head)
        beta = column(beta_all, goff + GDN_HEADS + head)
        gc_row = gct_scr[pl.ds(goff + head, 1), :]
        decay = jnp.where(incl, jnp.exp(jnp.where(incl, gc - gc_row, 0.0)), 0.0)
        kb = k * beta
        k_bf = k.astype(BF16)
        a_mat = jnp.where(strict, _dot_nt(kb.astype(BF16), k_bf) * decay, 0.0)
        a_pk = packed(a_mat)
        st.append(dict(
            cols=cols, gl=gl, inv=packed(eye) - a_pk, pw=a_pk.astype(BF16), pw_bd=a_mat.astype(BF16),
            rhs=jnp.concatenate([v * beta, kb * jnp.exp(gc)], axis=1).astype(BF16),
            intra=(_dot_nt(q.astype(BF16), k_bf) * decay).astype(BF16),
            q_dec=(q * jnp.exp(gc)).astype(BF16),
            k_dec_t=(k * jnp.exp(gl - gc)).T.astype(BF16),
            s=s_scr[hh], outs=[None] * n_chunks))
    for t in st:
        sq = _dot(t["pw"], t["pw_bd"])
        t["pw"], t["pw_bd"] = sq.astype(BF16), blockdiag(sq).astype(BF16)
    for level in range(n_levels):
        last = level == n_levels - 1
        for t in st:
            inv_bf = t["inv"].astype(BF16)
            both = _dot(inv_bf if last else jnp.concatenate([inv_bf, t["pw"]], axis=0), t["pw_bd"])
            t["inv"] = t["inv"] + both[:c_sz]
            if not last:
                t["pw"], t["pw_bd"] = both[c_sz:].astype(BF16), blockdiag(both[c_sz:]).astype(BF16)
    for t in st:
        uw = _dot(blockdiag(t["inv"]).astype(BF16), t["rhs"])
        t["u"], t["w"] = uw[:, :GDN_HEAD_DIM], uw[:, GDN_HEAD_DIM:].astype(BF16)
    for c in (range(n_chunks - 1, -1, -1) if rev else range(n_chunks)):
        rows = slice(c * c_sz, (c + 1) * c_sz)
        for t in st:
            t["s_bf"] = t["s"].astype(BF16)
            t["v_new"] = (t["u"][rows] - _dot(t["w"][rows], t["s_bf"])).astype(BF16)
        for t in st:
            t["outs"][c] = _dot(t["q_dec"][rows], t["s_bf"]) + _dot(t["intra"][rows, rows], t["v_new"])
            t["s"] = (t["s"] * jnp.exp(t["gl"][c * c_sz:c * c_sz + 1, :])
                      + _dot(t["k_dec_t"][:, rows], t["v_new"]))
    for hh, t in enumerate(st):
        s_scr[hh] = t["s"]
        o_ref[0, :, t["cols"]] = jnp.concatenate(t["outs"], axis=0)


def _gdn_scan(qkv, gates, gates_t, a_log_row, dt_row, a_log_col, dt_col, n_lat, rev):
    bsz, nt, _ = qkv.shape
    hps = GDN_HEADS_PER_STEP
    hblocks = GDN_HEADS // hps
    n_groups = nt // ROW_TILE
    lat_groups = n_lat // ROW_TILE
    width = hps * GDN_HEAD_DIM

    def group(i):
        lat = (lat_groups - i) if rev else (i - 1)
        return jnp.where(i == 0, n_groups - 1, lat)

    return pl.pallas_call(
        functools.partial(_gdn_body, rev=rev, n_lat_groups=lat_groups),
        grid=(bsz, hblocks, n_groups),
        in_specs=[pl.BlockSpec((1, ROW_TILE, width), lambda b, h, i: (b, group(i), h)),
                  pl.BlockSpec((1, ROW_TILE, width), lambda b, h, i: (b, group(i), hblocks + h)),
                  pl.BlockSpec((1, ROW_TILE, width), lambda b, h, i: (b, group(i), 2 * hblocks + h)),
                  pl.BlockSpec((1, ROW_TILE, LANES), lambda b, h, i: (b, group(i), 0)),
                  pl.BlockSpec((1, 4 * GDN_HEADS, ROW_TILE), lambda b, h, i: (b, 0, group(i))),
                  pl.BlockSpec((1, LANES), lambda b, h, i: (0, 0)),
                  pl.BlockSpec((1, LANES), lambda b, h, i: (0, 0)),
                  pl.BlockSpec((4 * GDN_HEADS, 1), lambda b, h, i: (0, 0)),
                  pl.BlockSpec((4 * GDN_HEADS, 1), lambda b, h, i: (0, 0))],
        out_specs=pl.BlockSpec((1, ROW_TILE, width), lambda b, h, i: (b, group(i), h)),
        out_shape=jax.ShapeDtypeStruct((bsz, nt, GDN_HEADS * GDN_HEAD_DIM), F32),
        scratch_shapes=[pltpu.VMEM((hps, GDN_HEAD_DIM, GDN_HEAD_DIM), F32),
                        pltpu.VMEM((4 * GDN_HEADS, ROW_TILE), F32)],
        compiler_params=_cparams("arbitrary", "arbitrary", "arbitrary"),
        name="gdn_scan_bwd" if rev else "gdn_scan_fwd",
    )(qkv, qkv, qkv, gates, gates_t, a_log_row, dt_row, a_log_col, dt_col)


def _gdn_out_body(of_ref, ob_ref, z_ref, gn_ref, w_ref, h_ref, mod_ref, g_ref, g_ffn_ref, wr_ref,
                  out_ref, u_ref, aff_ref, a_scr):
    for hh in range(GDN_HEADS):
        cols = slice(hh * GDN_HEAD_DIM, (hh + 1) * GDN_HEAD_DIM)
        o = of_ref[0, :, cols] + ob_ref[0, :, cols]
        a_scr[:, cols] = (_rms(o, gn_ref[...]) * _silu(z_ref[0, :, cols].astype(F32))).astype(BF16)
    y = _dot(a_scr[...], w_ref[...])
    _residual_and_route(h_ref[0], y, mod_ref[0, 0], g_ref, g_ffn_ref, wr_ref, out_ref, u_ref, aff_ref)


def _gdn_out(o_f, o_b, p, g_onorm, w_bf, h, mods, g_post, g_ffn, w_router, n_lat):
    bsz, _, d = h.shape
    width = GDN_HEADS * GDN_HEAD_DIM
    r_in, r_out, r_shape = _route_specs(bsz, n_lat, d)
    return pl.pallas_call(
        _gdn_out_body,
        grid=(bsz, n_lat // ROW_TILE),
        in_specs=[pl.BlockSpec((1, ROW_TILE, width), lambda b, i: (b, i, 0)),
                  pl.BlockSpec((1, ROW_TILE, width), lambda b, i: (b, i, 0)),
                  pl.BlockSpec((1, ROW_TILE, width), lambda b, i: (b, i, 3)),
                  pl.BlockSpec((1, GDN_HEAD_DIM), lambda b, i: (0, 0)),
                  pl.BlockSpec((width, d), lambda b, i: (0, 0)),
                  pl.BlockSpec((1, ROW_TILE, d), lambda b, i: (b, i, 0)),
                  pl.BlockSpec((1, 1, N_MOD, d), lambda b, i: (b, 0, 0, 0)),
                  pl.BlockSpec((1, d), lambda b, i: (0, 0))] + r_in,
        out_specs=r_out,
        out_shape=r_shape,
        scratch_shapes=[pltpu.VMEM((ROW_TILE, width), BF16)],
        compiler_params=_cparams("arbitrary", "arbitrary"),
        name="gdn_out",
    )(o_f, o_b, p, g_onorm[None, :], w_bf, h, mods, g_post[None, :], g_ffn[None, :], _pad_lanes(w_router))


def _gate_vectors(a_log_f, dt_bias_f, a_log_b, dt_bias_b):
    z = jnp.zeros((GDN_HEADS,), F32)
    a = jnp.concatenate([a_log_f, z, a_log_b, z])
    d = jnp.concatenate([dt_bias_f, z, dt_bias_b, z])
    pad = jnp.zeros((LANES - 4 * GDN_HEADS,), F32)
    return (jnp.concatenate([a, pad])[None, :], jnp.concatenate([d, pad])[None, :], a[:, None], d[:, None])


def kernel(x, c, ctx, c_ctx, l0_w_ada, l0_b_ada, l0_g_pre_mix, l0_g_post_mix, l0_g_pre_ffn, l0_g_post_ffn, l0_w_qkv, l0_lambda_q1, l0_lambda_k1, l0_lambda_q2, l0_lambda_k2, l0_g_subln, l0_w_o, l0_w_router, l0_w_gate, l0_w_up, l0_w_down, l1_w_ada, l1_b_ada, l1_g_pre_mix, l1_g_post_mix, l1_g_pre_ffn, l1_g_post_ffn, l1_w_in, l1_w_conv, l1_a_log_f, l1_dt_bias_f, l1_a_log_b, l1_dt_bias_b, l1_g_onorm, l1_w_o, l1_w_router, l1_w_gate, l1_w_up, l1_w_down):
    n_lat, n_ctx = x.shape[1], ctx.shape[1]
    h = jnp.concatenate([x, ctx], axis=1)

    mods = _ada_mods(c, c_ctx, l0_w_ada, l0_b_ada)
    qkv = _project(h, mods, l0_g_pre_mix, l0_w_qkv.astype(BF16), n_lat, rope_tables=_rope_tables(n_lat, n_ctx))
    lam_vecs = jnp.stack([l0_lambda_q1, l0_lambda_k1, l0_lambda_q2, l0_lambda_k2])
    o_lat, o_ctx = _diff_attention(qkv, lam_vecs, l0_g_subln, n_lat, depth=0)
    h, u, aff = _out_project(o_lat, o_ctx, l0_w_o.astype(BF16), h, mods, l0_g_post_mix, l0_g_pre_ffn, l0_w_router,
                             n_lat)
    h = _moe(h, u, aff, mods, l0_g_post_ffn, l0_w_gate, l0_w_up, l0_w_down, n_lat, n_ctx)

    mods = _ada_mods(c, c_ctx, l1_w_ada, l1_b_ada)
    width = GDN_HEADS * GDN_HEAD_DIM
    w_gates = jnp.pad(l1_w_in[:, 4 * width:], ((0, 0), (0, LANES - 4 * GDN_HEADS))).astype(BF16)
    p, gates = _project(h, mods, l1_g_pre_mix, l1_w_in[:, :4 * width].astype(BF16), n_lat, w_gates=w_gates)
    gates_t = jnp.swapaxes(gates[:, :, :4 * GDN_HEADS], 1, 2)
    qkv = _short_conv(p, l1_w_conv, n_lat)
    gv = _gate_vectors(l1_a_log_f, l1_dt_bias_f, l1_a_log_b, l1_dt_bias_b)
    o_f = _gdn_scan(qkv, gates, gates_t, *gv, n_lat, rev=False)
    o_b = _gdn_scan(qkv, gates, gates_t, *gv, n_lat, rev=True)
    hl, u, aff = _gdn_out(o_f, o_b, p, l1_g_onorm, l1_w_o.astype(BF16), h, mods, l1_g_post_mix, l1_g_pre_ffn,
                          l1_w_router, n_lat)
    return _moe(hl, u, aff, mods, l1_g_post_ffn, l1_w_gate, l1_w_up, l1_w_down, n_lat, 0)
```

```python
import functools
import math

import jax
import jax.numpy as jnp
from jax import lax
from jax.experimental import pallas as pl
from jax.experimental.pallas import tpu as pltpu

F32 = jnp.float32
BF16 = jnp.bfloat16
HIGHEST = lax.Precision.HIGHEST

EPS = 1e-6
N_MOD = 6
GRID_W = 64
ROPE_BASE = 10000.0
DA_HEADS = 8
DA_HEAD_DIM = 64
GDN_HEADS = 8
GDN_HEAD_DIM = 128
CONV_K = 5
CONV_HALO = 8
N_EXPERTS = 16
CAP_FACTOR = 2

LANES = 128
ROW_TILE = 256
PROJ_ROWS = 768
ATTN_Q_ROWS = 2048
ATTN_CHAIN_ROWS = 128
Q_PRESCALE = DA_HEAD_DIM ** -0.5 * math.log2(math.e)
GDN_CHUNK = 64
GDN_HEADS_PER_STEP = 8
FF_CHUNK = 1024
SLOT_WINDOW = 64
SLOT_ALIGN = 16
TABLE_COUNT = 64
VMEM_LIMIT = 56 * 1024 * 1024

SH_M, SC_M, GT_M, SH_F, SC_F, GT_F = range(6)


def _cparams(*sem):
    return pltpu.CompilerParams(dimension_semantics=sem, vmem_limit_bytes=VMEM_LIMIT)


def _dot(a, b):
    return jnp.dot(a, b, preferred_element_type=F32)


def _dot_nt(a, b, precision=None):
    return lax.dot_general(a, b, (((1,), (1,)), ((), ())), preferred_element_type=F32, precision=precision)


def _dot_tn(a, b):
    return lax.dot_general(a, b, (((0,), (0,)), ((), ())), preferred_element_type=F32)


def _rms(x, g):
    return x * lax.rsqrt(jnp.mean(x * x, axis=-1, keepdims=True) + EPS) * g


def _sigmoid(x):
    return 1.0 / (1.0 + jnp.exp(-x))


def _silu(x):
    return x * _sigmoid(x)


def _softplus(x):
    return jnp.maximum(x, 0.0) + jnp.log(1.0 + jnp.exp(-jnp.abs(x)))


def _ada_body(c_ref, w_ref, b_ref, o_ref):
    o_ref[...] = jnp.dot(_silu(c_ref[...]), w_ref[...], precision=HIGHEST, preferred_element_type=F32) + b_ref[...]


def _ada_mods(c, c_ctx, w_ada, b_ada):
    bsz, d = c.shape
    rows = 16
    cc = jnp.concatenate([c, c_ctx[None, :], jnp.zeros((rows - bsz - 1, d), F32)], axis=0)
    tn = 1024
    m = pl.pallas_call(
        _ada_body,
        grid=(N_MOD * d // tn,),
        in_specs=[pl.BlockSpec((rows, d), lambda j: (0, 0)),
                  pl.BlockSpec((d, tn), lambda j: (0, j)),
                  pl.BlockSpec((1, tn), lambda j: (0, j))],
        out_specs=pl.BlockSpec((rows, tn), lambda j: (0, j)),
        out_shape=jax.ShapeDtypeStruct((rows, N_MOD * d), F32),
        compiler_params=_cparams("arbitrary"),
        name="ada_mods",
    )(cc, w_ada, b_ada[None, :])
    lat = m[:bsz].reshape(bsz, 1, N_MOD, d)
    ctx = jnp.broadcast_to(m[bsz].reshape(1, 1, N_MOD, d), (bsz, 1, N_MOD, d))
    return jnp.concatenate([lat, ctx], axis=1)


def _proj_body(h_ref, mod_ref, g_ref, w_ref, *rest, n_lat, rope, gates):
    rest = list(rest)
    cos_ref, sin_ref = (rest.pop(0), rest.pop(0)) if rope else (None, None)
    wg_ref = rest.pop(0) if gates else None
    o_ref = rest.pop(0)
    og_ref = rest.pop(0) if gates else None
    x = h_ref[0]
    tm, d = x.shape
    row = pl.program_id(1) * tm + lax.broadcasted_iota(jnp.int32, (tm, 1), 0)
    is_ctx = row >= n_lat
    scale = jnp.where(is_ctx, mod_ref[0, 1, SC_M:SC_M + 1, :], mod_ref[0, 0, SC_M:SC_M + 1, :])
    shift = jnp.where(is_ctx, mod_ref[0, 1, SH_M:SH_M + 1, :], mod_ref[0, 0, SH_M:SH_M + 1, :])
    u = (_rms(x, g_ref[...]) * (1.0 + scale) + shift).astype(BF16)
    if gates:
        og_ref[0] = _dot(u, wg_ref[...])
    for j in range(w_ref.shape[1] // d):
        cols = slice(j * d, (j + 1) * d)
        acc = _dot(u, w_ref[:, cols])
        if rope and j < 2:
            reps = d // LANES
            cos = jnp.tile(cos_ref[...], (1, reps))
            sin = jnp.tile(sin_ref[...], (1, reps))
            lane = lax.broadcasted_iota(jnp.int32, acc.shape, 1)
            half = DA_HEAD_DIM // 4
            first = (lane % (2 * half)) < half
            partner = jnp.where(first, pltpu.roll(acc, d - half, 1), pltpu.roll(acc, half, 1))
            acc = acc * cos + partner * sin
            if j == 0:
                acc = acc * Q_PRESCALE
        o_ref[0, :, cols] = acc.astype(o_ref.dtype)


def _project(h, mods, g, w_bf, n_lat, rope_tables=None, w_gates=None):
    bsz, nt, d = h.shape
    n_out = w_bf.shape[1]
    tm = PROJ_ROWS
    assert nt % tm == 0 and n_lat % ROW_TILE == 0 and (nt - n_lat) == ROW_TILE
    rope = rope_tables is not None
    gates = w_gates is not None
    in_specs = [pl.BlockSpec((1, tm, d), lambda b, i: (b, i, 0)),
                pl.BlockSpec((1, 2, N_MOD, d), lambda b, i: (b, 0, 0, 0)),
                pl.BlockSpec((1, d), lambda b, i: (0, 0)),
                pl.BlockSpec((d, n_out), lambda b, i: (0, 0))]
    args = [h, mods, g[None, :], w_bf]
    out_specs = [pl.BlockSpec((1, tm, n_out), lambda b, i: (b, i, 0))]
    out_shape = [jax.ShapeDtypeStruct((bsz, nt, n_out), BF16)]
    if rope:
        in_specs += [pl.BlockSpec((tm, LANES), lambda b, i: (i, 0))] * 2
        args += list(rope_tables)
    if gates:
        in_specs.append(pl.BlockSpec((d, LANES), lambda b, i: (0, 0)))
        args.append(w_gates)
        out_specs.append(pl.BlockSpec((1, tm, LANES), lambda b, i: (b, i, 0)))
        out_shape.append(jax.ShapeDtypeStruct((bsz, nt, LANES), F32))
    out = pl.pallas_call(
        functools.partial(_proj_body, n_lat=n_lat, rope=rope, gates=gates),
        grid=(bsz, nt // tm),
        in_specs=in_specs,
        out_specs=out_specs,
        out_shape=out_shape,
        compiler_params=_cparams("arbitrary", "arbitrary"),
        name="mod_project",
    )(*args)
    return out if gates else out[0]


def _rope_tables(n_lat, n_ctx):
    rows = n_lat // GRID_W
    r = jnp.repeat(jnp.arange(rows), GRID_W).astype(F32)
    col = jnp.tile(jnp.arange(GRID_W), rows).astype(F32)
    half = DA_HEAD_DIM // 2
    inv = ROPE_BASE ** (-jnp.arange(0, half, 2, dtype=F32) / half)
    ang_r, ang_c = r[:, None] * inv, col[:, None] * inv
    cos = jnp.concatenate([jnp.cos(ang_r)] * 2 + [jnp.cos(ang_c)] * 2, axis=-1)
    sin = jnp.concatenate([-jnp.sin(ang_r), jnp.sin(ang_r), -jnp.sin(ang_c), jnp.sin(ang_c)], axis=-1)
    cos = jnp.concatenate([jnp.tile(cos, (1, 2)), jnp.ones((n_ctx, LANES), F32)], axis=0)
    sin = jnp.concatenate([jnp.tile(sin, (1, 2)), jnp.zeros((n_ctx, LANES), F32)], axis=0)
    return cos, sin


def _attn_body(lam_ref, q_ref, k_ref, v_ref, gs_ref, o_ref, *, lam_init):
    lv = lam_ref[...]
    lam = (jnp.exp(jnp.sum(lv[0:1] * lv[1:2], axis=-1, keepdims=True))
           - jnp.exp(jnp.sum(lv[2:3] * lv[3:4], axis=-1, keepdims=True)) + lam_init)
    hw = 2 * DA_HEAD_DIM
    n_q = q_ref.shape[1]
    lane = lax.broadcasted_iota(jnp.int32, (n_q, hw), 1)
    rows = min(n_q, ATTN_CHAIN_ROWS)
    chains = [(hh, r) for hh in range(q_ref.shape[2] // hw) for r in range(0, n_q, rows)]
    lhs, v_one = {}, {}
    for hh in range(q_ref.shape[2] // hw):
        cols = slice(hh * hw, (hh + 1) * hw)
        q, v = q_ref[0, :, cols], v_ref[0, :, cols]
        v_one[hh] = jnp.concatenate([v, jnp.ones_like(v)], axis=1)
        qm = [jnp.where((lane >= c * DA_HEAD_DIM) & (lane < (c + 1) * DA_HEAD_DIM), q, jnp.zeros_like(q))
              for c in range(2)]
        for r in range(0, n_q, rows):
            lhs[hh, r] = jnp.concatenate([qm[0][r:r + rows], qm[1][r:r + rows]], axis=0)
    s = [_dot_nt(lhs[hh, r], k_ref[0, :, hh * hw:(hh + 1) * hw]) for hh, r in chains]
    e = [jnp.exp2((x - jnp.max(x, axis=-1, keepdims=True)).astype(BF16)) for x in s]
    ov = [_dot(x, v_one[hh]) for x, (hh, r) in zip(e, chains)]
    att = [x[:, :hw] / x[:, hw:hw + 1] for x in ov]
    for (hh, r), a in zip(chains, att):
        o = a[:rows] - lam * a[rows:]
        o_ref[0, r:r + rows, hh * hw:(hh + 1) * hw] = (_rms(o, gs_ref[...]) * (1.0 - lam_init)).astype(o_ref.dtype)


def _diff_attention(qkv, lam_vecs, g_subln, n_lat, depth):
    bsz, nt, d3 = qkv.shape
    d = d3 // 3
    hw = 2 * DA_HEAD_DIM
    lam_init = 0.8 - 0.6 * math.exp(-0.3 * depth)
    n_ctx = nt - n_lat

    def call(n_q, tq, q_blk0, n_k, k_blk0, heads):
        hblocks = DA_HEADS // heads
        width = heads * hw
        return pl.pallas_call(
            functools.partial(_attn_body, lam_init=lam_init),
            grid=(bsz, hblocks, n_q // tq),
            in_specs=[pl.BlockSpec((4, DA_HEAD_DIM), lambda b, h, i: (0, 0)),
                      pl.BlockSpec((1, tq, width), lambda b, h, i: (b, q_blk0 + i, h)),
                      pl.BlockSpec((1, n_k, width), lambda b, h, i: (b, k_blk0, hblocks + h)),
                      pl.BlockSpec((1, n_k, width), lambda b, h, i: (b, k_blk0, 2 * hblocks + h)),
                      pl.BlockSpec((1, hw), lambda b, h, i: (0, 0))],
            out_specs=pl.BlockSpec((1, tq, width), lambda b, h, i: (b, i, h)),
            out_shape=jax.ShapeDtypeStruct((bsz, n_q, d), BF16),
            compiler_params=_cparams("arbitrary", "arbitrary", "arbitrary"),
            name="diff_attention",
        )(lam_vecs, qkv, qkv, qkv, g_subln[None, :])

    o_lat = call(n_lat, min(ATTN_Q_ROWS, n_lat), 0, nt, 0, heads=1)
    o_ctx = call(n_ctx, n_ctx, n_lat // n_ctx, n_ctx, n_lat // n_ctx, heads=DA_HEADS)
    return o_lat, o_ctx


def _residual_and_route(h, y, mod, g_post_ref, g_ffn_ref, wr_ref, out_ref, u_ref, aff_ref):
    h_new = h + mod[GT_M:GT_M + 1, :] * _rms(y, g_post_ref[...])
    out_ref[0] = h_new
    u = _rms(h_new, g_ffn_ref[...]) * (1.0 + mod[SC_F:SC_F + 1, :]) + mod[SH_F:SH_F + 1, :]
    u_hi = u.astype(BF16)
    u_ref[0] = u_hi
    u_lo = (u - u_hi.astype(F32)).astype(BF16)
    w = wr_ref[...]
    w_hi = w.astype(BF16)
    w_lo = (w - w_hi.astype(F32)).astype(BF16)
    logits = (_dot(u_hi, w_hi) + _dot(u_hi, w_lo) + _dot(u_lo, w_hi)).T[:N_EXPERTS]
    e = jnp.exp(logits - jnp.max(logits, axis=0, keepdims=True))
    aff_ref[0] = e / jnp.sum(e, axis=0, keepdims=True)


def _pad_lanes(w):
    return jnp.pad(w, ((0, 0), (0, LANES - w.shape[1])))


def _route_specs(bsz, nt, d):
    in_specs = [pl.BlockSpec((1, d), lambda b, i: (0, 0)),
                pl.BlockSpec((d, LANES), lambda b, i: (0, 0))]
    out_specs = [pl.BlockSpec((1, ROW_TILE, d), lambda b, i: (b, i, 0)),
                 pl.BlockSpec((1, ROW_TILE, d), lambda b, i: (b, i, 0)),
                 pl.BlockSpec((1, N_EXPERTS, ROW_TILE), lambda b, i: (b, 0, i))]
    out_shape = [jax.ShapeDtypeStruct((bsz, nt, d), F32),
                 jax.ShapeDtypeStruct((bsz, nt, d), BF16),
                 jax.ShapeDtypeStruct((bsz, N_EXPERTS, nt), F32)]
    return in_specs, out_specs, out_shape


def _oproj_body(o_lat_ref, o_ctx_ref, w_ref, h_ref, mod_ref, g_ref, g_ffn_ref, wr_ref, out_ref, u_ref, aff_ref,
                *, lat_tiles):
    o = jnp.where(pl.program_id(1) < lat_tiles, o_lat_ref[0], o_ctx_ref[0])
    y = _dot(o, w_ref[...])
    _residual_and_route(h_ref[0], y, mod_ref[0, 0], g_ref, g_ffn_ref, wr_ref, out_ref, u_ref, aff_ref)


def _out_project(o_lat, o_ctx, w_bf, h, mods, g_post, g_ffn, w_router, n_lat):
    bsz, nt, d = h.shape
    lat_tiles = n_lat // ROW_TILE
    r_in, r_out, r_shape = _route_specs(bsz, nt, d)
    return pl.pallas_call(
        functools.partial(_oproj_body, lat_tiles=lat_tiles),
        grid=(bsz, nt // ROW_TILE),
        in_specs=[pl.BlockSpec((1, ROW_TILE, d), lambda b, i: (b, jnp.minimum(i, lat_tiles - 1), 0)),
                  pl.BlockSpec((1, ROW_TILE, d), lambda b, i: (b, 0, 0)),
                  pl.BlockSpec((d, d), lambda b, i: (0, 0)),
                  pl.BlockSpec((1, ROW_TILE, d), lambda b, i: (b, i, 0)),
                  pl.BlockSpec((1, 1, N_MOD, d), lambda b, i: (b, i // lat_tiles, 0, 0)),
                  pl.BlockSpec((1, d), lambda b, i: (0, 0))] + r_in,
        out_specs=r_out,
        out_shape=r_shape,
        compiler_params=_cparams("arbitrary", "arbitrary"),
        name="out_project",
    )(o_lat, o_ctx, w_bf, h, mods, g_post[None, :], g_ffn[None, :], _pad_lanes(w_router))


def _lane_cumsum(x):
    n = x.shape[1]
    jj = lax.broadcasted_iota(jnp.int32, (ROW_TILE, ROW_TILE), 0)
    nn = lax.broadcasted_iota(jnp.int32, (ROW_TILE, ROW_TILE), 1)
    tri = (jj <= nn).astype(BF16)
    run = jnp.zeros((x.shape[0], 1), F32)
    parts = []
    for t in range(n // ROW_TILE):
        local = _dot(x[:, t * ROW_TILE:(t + 1) * ROW_TILE].astype(BF16), tri) + run
        parts.append(local)
        run = local[:, ROW_TILE - 1:ROW_TILE]
    return jnp.concatenate(parts, axis=1) if len(parts) > 1 else parts[0]


def _select_slots(aff, cap):
    bits = pltpu.bitcast(aff, jnp.int32)

    def enough(cand):
        return jnp.sum((bits >= cand).astype(F32), axis=1, keepdims=True) >= cap

    def step(t, lo):
        hi_bit = jnp.left_shift(jnp.int32(1), 29 - 2 * t)
        lo_bit = jnp.left_shift(jnp.int32(1), 28 - 2 * t)
        both, first, second = lo | hi_bit | lo_bit, lo | hi_bit, lo | lo_bit
        return jnp.where(enough(both), both, jnp.where(enough(first), first, jnp.where(enough(second), second, lo)))

    top = jnp.full((aff.shape[0], 1), 1 << 30, jnp.int32)
    thr = jnp.where(enough(top), top, 0)
    thr = lax.fori_loop(0, 15, step, thr)
    gt = bits > thr
    eq = bits == thr
    need = cap - jnp.sum(gt.astype(F32), axis=1, keepdims=True)
    eq_rank = _lane_cumsum(eq.astype(F32))
    sel = gt | (eq & (eq_rank <= need))
    sel_f = sel.astype(F32)
    slot = _lane_cumsum(sel_f) - 1.0
    return jnp.where(sel, slot, -1.0).astype(jnp.int32), sel_f


def _tile_table(sel_f, tile0):
    lane = lax.broadcasted_iota(jnp.int32, (sel_f.shape[0], LANES), 1)
    run = jnp.zeros((sel_f.shape[0], 1), F32)
    tab = jnp.zeros((sel_f.shape[0], LANES), F32)
    for t in range(sel_f.shape[1] // ROW_TILE):
        inside = jnp.sum(sel_f[:, t * ROW_TILE:(t + 1) * ROW_TILE], axis=1, keepdims=True)
        tab = jnp.where(lane == tile0 + t, run, tab)
        tab = jnp.where(lane == TABLE_COUNT + tile0 + t, inside, tab)
        run = run + inside
    return tab


def _select_body(aff_ref, pos_ref, tab_ref, *, n_lat, n_ctx):
    aff = aff_ref[0]
    pos, sel_f = _select_slots(aff[:, :n_lat], CAP_FACTOR * n_lat // N_EXPERTS)
    pos_ref[0, :, :n_lat] = pos
    tab = _tile_table(sel_f, 0)
    if n_ctx:
        pos, sel_f = _select_slots(aff[:, n_lat:], CAP_FACTOR * n_ctx // N_EXPERTS)
        pos_ref[0, :, n_lat:] = pos
        tab = tab + _tile_table(sel_f, n_lat // ROW_TILE)
    tab_ref[0] = tab.astype(jnp.int32)


def _select(aff, n_lat, n_ctx):
    bsz, e, nt = aff.shape
    n_tiles = nt // ROW_TILE
    pos, tab = pl.pallas_call(
        functools.partial(_select_body, n_lat=n_lat, n_ctx=n_ctx),
        grid=(bsz,),
        in_specs=[pl.BlockSpec((1, e, nt), lambda b: (b, 0, 0))],
        out_specs=[pl.BlockSpec((1, e, nt), lambda b: (b, 0, 0)),
                   pl.BlockSpec((1, e, LANES), lambda b: (b, 0, 0))],
        out_shape=[jax.ShapeDtypeStruct((bsz, e, nt), jnp.int32),
                   jax.ShapeDtypeStruct((bsz, e, LANES), jnp.int32)],
        compiler_params=_cparams("arbitrary"),
        name="moe_select",
    )(aff)
    first = tab[:, :, :n_tiles].reshape(-1)
    count = tab[:, :, TABLE_COUNT:TABLE_COUNT + n_tiles].reshape(-1)
    return pos, first, count


class _TileWindows:
    def __init__(self, first_ref, count_ref, pos_ref, n_lat, rows, lat_cap):
        b, tile = pl.program_id(0), pl.program_id(1)
        n_exp = pos_ref.shape[1]
        self.rows, self.n_exp = rows, n_exp
        base = jnp.where(tile * ROW_TILE >= n_lat, lat_cap, 0)
        pos = pos_ref[0]
        self.slot = jnp.where(pos >= 0, pos + base, -1)
        self.starts, self.rounds = [], 0
        for x in range(n_exp):
            at = (b * n_exp + x) * pl.num_programs(1) + tile
            first = first_ref[at]
            start = base + (first // SLOT_ALIGN) * SLOT_ALIGN
            self.starts.append(start)
            self.rounds = jnp.maximum(self.rounds, (base + first + count_ref[at] - start + SLOT_WINDOW - 1) // SLOT_WINDOW)
        self.ids0 = lax.broadcasted_iota(jnp.int32, (SLOT_WINDOW, ROW_TILE), 0)

    def window(self, x, r):
        want = self.starts[x] + r * SLOT_WINDOW
        s0 = jnp.minimum(want, self.rows - SLOT_WINDOW)
        ids = self.ids0 + s0
        hot = (self.slot[x:x + 1, :] == ids) & (ids >= want)
        return pl.multiple_of(x * self.rows + s0, SLOT_ALIGN), hot


def _as_bf16(mask):
    return jnp.where(mask, 1.0, 0.0).astype(BF16)


def _gather_body(first_ref, count_ref, pos_ref, aff_ref, u_ref, xs_ref, gate_ref, *, n_lat, rows, lat_cap):
    @pl.when(pl.program_id(1) == 0)
    def _():
        xs_ref[...] = jnp.zeros_like(xs_ref)
        gate_ref[...] = jnp.zeros_like(gate_ref)

    tw = _TileWindows(first_ref, count_ref, pos_ref, n_lat, rows, lat_cap)
    aff = aff_ref[0]
    u = u_ref[0]

    def one_round(r, carry):
        wins = [tw.window(x, r) for x in range(tw.n_exp)]
        part = _dot(jnp.concatenate([_as_bf16(hot) for _, hot in wins], axis=0), u)
        for x, (off, hot) in enumerate(wins):
            dst = pl.ds(off, SLOT_WINDOW)
            xs_ref[0, dst, :] = xs_ref[0, dst, :] + part[x * SLOT_WINDOW:(x + 1) * SLOT_WINDOW].astype(BF16)
            gate = jnp.sum(jnp.where(hot, aff[x:x + 1, :], 0.0), axis=1, keepdims=True)
            gate_ref[0, dst, :] = gate_ref[0, dst, :] + jnp.broadcast_to(gate, (SLOT_WINDOW, LANES))
        return carry

    lax.fori_loop(0, tw.rounds, one_round, 0)


def _slot_rows(n_lat, n_ctx):
    return CAP_FACTOR * n_lat // N_EXPERTS + CAP_FACTOR * n_ctx // N_EXPERTS


def _gather(pos, first, count, aff, u, n_lat, n_ctx):
    bsz, nt, d = u.shape
    e = pos.shape[1]
    rows = _slot_rows(n_lat, n_ctx)
    xs, gate = pl.pallas_call(
        functools.partial(_gather_body, n_lat=n_lat, rows=rows, lat_cap=CAP_FACTOR * n_lat // N_EXPERTS),
        grid_spec=pltpu.PrefetchScalarGridSpec(
            num_scalar_prefetch=2,
            grid=(bsz, nt // ROW_TILE),
            in_specs=[pl.BlockSpec((1, e, ROW_TILE), lambda b, i, *_: (b, 0, i)),
                      pl.BlockSpec((1, e, ROW_TILE), lambda b, i, *_: (b, 0, i)),
                      pl.BlockSpec((1, ROW_TILE, d), lambda b, i, *_: (b, i, 0))],
            out_specs=[pl.BlockSpec((1, e * rows, d), lambda b, i, *_: (b, 0, 0)),
                       pl.BlockSpec((1, e * rows, LANES), lambda b, i, *_: (b, 0, 0))]),
        out_shape=[jax.ShapeDtypeStruct((bsz, e * rows, d), BF16),
                   jax.ShapeDtypeStruct((bsz, e * rows, LANES), F32)],
        compiler_params=_cparams("arbitrary", "arbitrary"),
        name="moe_gather",
    )(first, count, pos, aff, u)
    return xs.reshape(bsz, e, rows, d), gate.reshape(bsz, e, rows, LANES)


def _expert_body(xs_ref, gate_ref, wg_hbm, wu_hbm, wd_hbm, y_ref, wg_scr, wu_scr, wd_scr,
                 wg_stage, wu_stage, wd_stage, sems):
    x, b = pl.program_id(0), pl.program_id(1)
    n_exp, n_chunks = pl.num_programs(0), pl.num_programs(1)
    rows_in, rows_mid = wg_stage.shape[0], wd_stage.shape[0]

    def chunk_copies(expert, chunk):
        return (pltpu.make_async_copy(wg_hbm.at[expert, pl.ds(chunk * rows_in, rows_in), :], wg_stage, sems.at[0]),
                pltpu.make_async_copy(wu_hbm.at[expert, pl.ds(chunk * rows_in, rows_in), :], wu_stage, sems.at[1]),
                pltpu.make_async_copy(wd_hbm.at[expert, pl.ds(chunk * rows_mid, rows_mid), :], wd_stage, sems.at[2]))

    def land(copy, chunk):
        at_in = pl.ds(pl.multiple_of(chunk * rows_in, SLOT_ALIGN), rows_in)
        at_mid = pl.ds(pl.multiple_of(chunk * rows_mid, SLOT_ALIGN), rows_mid)
        wg_scr[copy, at_in, :] = wg_stage[...].astype(BF16)
        wu_scr[copy, at_in, :] = wu_stage[...].astype(BF16)
        wd_scr[copy, at_mid, :] = wd_stage[...].astype(BF16)

    @pl.when((x == 0) & (b == 0))
    def _():
        def fetch(chunk, carry):
            copies = chunk_copies(0, chunk)
            for cp in copies:
                cp.start()
            for cp in copies:
                cp.wait()
            land(0, chunk)
            return carry

        lax.fori_loop(0, n_chunks, fetch, 0)

    def start_next(_, carry):
        for cp in chunk_copies(x + 1, b):
            cp.start()
        return carry

    lax.fori_loop(0, jnp.where(x + 1 < n_exp, 1, 0), start_next, 0)

    cur = x % 2
    xs = xs_ref[0, 0]
    acc = jnp.zeros((xs.shape[0], wd_scr.shape[2]), F32)
    for s in range(wg_scr.shape[2] // FF_CHUNK):
        cols = slice(s * FF_CHUNK, (s + 1) * FF_CHUNK)
        hid = _silu(_dot(xs, wg_scr[cur, :, cols])) * _dot(xs, wu_scr[cur, :, cols])
        acc = acc + _dot(hid.astype(BF16), wd_scr[cur, cols, :])
    y_ref[0, 0] = (acc * gate_ref[0, 0, :, 0:1]).astype(y_ref.dtype)

    @pl.when(x + 1 < n_exp)
    def _():
        for cp in chunk_copies(x + 1, b):
            cp.wait()
        land(1 - cur, b)


def _experts(xs, gate, wg, wu, wd):
    bsz, e, rows, d = xs.shape
    ff = wg.shape[2]
    return pl.pallas_call(
        _expert_body,
        grid=(e, bsz),
        in_specs=[pl.BlockSpec((1, 1, rows, d), lambda x, b: (b, x, 0, 0)),
                  pl.BlockSpec((1, 1, rows, LANES), lambda x, b: (b, x, 0, 0)),
                  pl.BlockSpec(memory_space=pl.ANY),
                  pl.BlockSpec(memory_space=pl.ANY),
                  pl.BlockSpec(memory_space=pl.ANY)],
        out_specs=pl.BlockSpec((1, 1, rows, d), lambda x, b: (b, x, 0, 0)),
        out_shape=jax.ShapeDtypeStruct((bsz, e, rows, d), BF16),
        scratch_shapes=[pltpu.VMEM((2, d, ff), BF16), pltpu.VMEM((2, d, ff), BF16), pltpu.VMEM((2, ff, d), BF16),
                        pltpu.VMEM((d // bsz, ff), F32), pltpu.VMEM((d // bsz, ff), F32),
                        pltpu.VMEM((ff // bsz, d), F32), pltpu.SemaphoreType.DMA((3,))],
        compiler_params=_cparams("arbitrary", "arbitrary"),
        name="moe_experts",
    )(xs, gate, wg, wu, wd)


def _combine_body(first_ref, count_ref, pos_ref, y_ref, h_ref, mod_ref, g_ref, out_ref, acc_scr,
                  *, n_lat, rows, lat_cap):
    tw = _TileWindows(first_ref, count_ref, pos_ref, n_lat, rows, lat_cap)
    acc_scr[...] = jnp.zeros_like(acc_scr)

    def one_round(r, carry):
        group = ROW_TILE // SLOT_WINDOW
        for g0 in range(0, tw.n_exp, group):
            wins = [tw.window(x, r) for x in range(g0, g0 + group)]
            hot = jnp.concatenate([_as_bf16(h) for _, h in wins], axis=0)
            rows_y = jnp.concatenate([y_ref[0, pl.ds(off, SLOT_WINDOW), :] for off, _ in wins], axis=0)
            acc_scr[...] += _dot_tn(hot, rows_y)
        return carry

    lax.fori_loop(0, tw.rounds, one_round, 0)
    out_ref[0] = h_ref[0] + mod_ref[0, 0, GT_F:GT_F + 1, :] * _rms(acc_scr[...], g_ref[...])


def _combine(pos, first, count, y, h, mods, g_post, n_lat, n_ctx):
    bsz, e, rows, d = y.shape
    nt = n_lat + n_ctx
    lat_tiles = n_lat // ROW_TILE
    return pl.pallas_call(
        functools.partial(_combine_body, n_lat=n_lat, rows=rows, lat_cap=CAP_FACTOR * n_lat // N_EXPERTS),
        grid_spec=pltpu.PrefetchScalarGridSpec(
            num_scalar_prefetch=2,
            grid=(bsz, nt // ROW_TILE),
            in_specs=[pl.BlockSpec((1, e, ROW_TILE), lambda b, i, *_: (b, 0, i)),
                      pl.BlockSpec((1, e * rows, d), lambda b, i, *_: (b, 0, 0)),
                      pl.BlockSpec((1, ROW_TILE, d), lambda b, i, *_: (b, i, 0)),
                      pl.BlockSpec((1, 1, N_MOD, d), lambda b, i, *_: (b, i // lat_tiles, 0, 0)),
                      pl.BlockSpec((1, d), lambda b, i, *_: (0, 0))],
            out_specs=pl.BlockSpec((1, ROW_TILE, d), lambda b, i, *_: (b, i, 0)),
            scratch_shapes=[pltpu.VMEM((ROW_TILE, d), F32)]),
        out_shape=jax.ShapeDtypeStruct((bsz, nt, d), F32),
        compiler_params=_cparams("arbitrary", "arbitrary"),
        name="moe_combine",
    )(first, count, pos, y.reshape(bsz, e * rows, d), h, mods, g_post[None, :])


def _moe(h, u, aff, mods, g_post, wg, wu, wd, n_lat, n_ctx):
    pos, first, count = _select(aff, n_lat, n_ctx)
    xs, gate = _gather(pos, first, count, aff, u, n_lat, n_ctx)
    y = _experts(xs, gate, wg, wu, wd)
    return _combine(pos, first, count, y, h, mods, g_post, n_lat, n_ctx)


def _conv_body(x_ref, w_ref, o_ref, pad_scr, *, n_lat):
    j = pl.program_id(1)
    nt = x_ref.shape[1]
    halo = CONV_HALO
    zeros = jnp.zeros((halo, LANES), F32)
    for lo, hi in ((0, n_lat), (n_lat, nt)):
        n = hi - lo
        if n == 0:
            continue
        pad_scr[0:halo, :] = zeros
        pad_scr[halo:halo + n, :] = x_ref[0, lo:hi, :].astype(F32)
        pad_scr[halo + n:2 * halo + n, :] = zeros
        acc = None
        for tap in range(CONV_K):
            start = halo + tap - CONV_K // 2
            term = pad_scr[start:start + n, :] * w_ref[tap:tap + 1, :]
            acc = term if acc is None else acc + term
        y = _silu(acc)
        inv_norm = lax.rsqrt(jnp.sum(y * y, axis=-1, keepdims=True) + EPS)
        factor = jnp.where(j < GDN_HEADS, inv_norm * GDN_HEAD_DIM ** -0.5,
                           jnp.where(j < 2 * GDN_HEADS, inv_norm, 1.0))
        o_ref[0, lo:hi, :] = y * factor


def _short_conv(p, w_conv, n_lat):
    bsz, nt, _ = p.shape
    nblk = 3 * GDN_HEADS
    return pl.pallas_call(
        functools.partial(_conv_body, n_lat=n_lat),
        grid=(bsz, nblk),
        in_specs=[pl.BlockSpec((1, nt, LANES), lambda b, j: (b, 0, j)),
                  pl.BlockSpec((CONV_K, LANES), lambda b, j: (0, j))],
        out_specs=pl.BlockSpec((1, nt, LANES), lambda b, j: (b, 0, j)),
        out_shape=jax.ShapeDtypeStruct((bsz, nt, nblk * LANES), F32),
        scratch_shapes=[pltpu.VMEM((n_lat + 2 * CONV_HALO, LANES), F32)],
        compiler_params=_cparams("arbitrary", "arbitrary"),
        name="gdn_conv",
    )(p, w_conv)


def _gdn_body(q_ref, k_ref, v_ref, gate_ref, gate_t_ref, alog_ref, dt_ref, alog_t_ref, dt_t_ref,
              o_ref, s_scr, gct_scr, *, rev, n_lat_groups):
    hb = pl.program_id(1)
    step = pl.program_id(2)
    r_t = ROW_TILE
    c_sz = GDN_CHUNK
    goff = 2 * GDN_HEADS if rev else 0

    @pl.when(step == 0)
    def _():
        s_scr[...] = jnp.zeros_like(s_scr)

    ii = lax.broadcasted_iota(jnp.int32, (r_t, r_t), 0)
    jj = lax.broadcasted_iota(jnp.int32, (r_t, r_t), 1)
    same = (ii // c_sz) == (jj // c_sz)
    if rev:
        incl, strict = same & (jj >= ii), same & (jj > ii)
    else:
        incl, strict = same & (jj <= ii), same & (jj < ii)
    m_incl = incl.astype(BF16)
    m_same = same.astype(BF16)
    eye = (ii == jj).astype(F32)

    def pieces(x):
        hi = x.astype(BF16)
        rest = x - hi.astype(F32)
        mid = rest.astype(BF16)
        return hi, mid, (rest - mid.astype(F32)).astype(BF16)

    lane = lax.broadcasted_iota(jnp.int32, (1, LANES), 1)
    is_g = (lane >= goff) & (lane < goff + GDN_HEADS)
    raw = gate_ref[0]
    g_all = jnp.where(is_g, -jnp.exp(alog_ref[...]) * _softplus(raw + dt_ref[...]), 0.0)
    beta_all = _sigmoid(raw)
    g_parts = pieces(g_all)
    gc_all = sum(_dot(m_incl, x) for x in g_parts)
    gl_all = sum(_dot(m_same, x) for x in g_parts)
    g_t = -jnp.exp(alog_t_ref[...]) * _softplus(gate_t_ref[0] + dt_t_ref[...])
    gct_scr[...] = sum(_dot_nt(x, m_incl) for x in pieces(g_t))
    lane_f = lax.broadcasted_iota(jnp.int32, (r_t, LANES), 1)

    def packed(m):
        return sum(m[c * c_sz:(c + 1) * c_sz] for c in range(r_t // c_sz))

    def blockdiag(p):
        return jnp.where(same, jnp.tile(p, (r_t // c_sz, 1)), 0.0)

    def column(a, idx):
        return jnp.sum(jnp.where(lane_f == idx, a, 0.0), axis=1, keepdims=True)

    heads = range(GDN_HEADS_PER_STEP)
    n_chunks = r_t // c_sz
    n_levels = int(math.log2(c_sz)) - 1
    st = []
    for hh in heads:
        head = hb * GDN_HEADS_PER_STEP + hh
        cols = slice(hh * GDN_HEAD_DIM, (hh + 1) * GDN_HEAD_DIM)
        q, k, v = q_ref[0, :, cols], k_ref[0, :, cols], v_ref[0, :, cols]
        gc = column(gc_all, goff + head)
        gl = column(gl_all, goff + head)
        beta = column(beta_all, goff + GDN_HEADS + head)
        gc_row = gct_scr[pl.ds(goff + head, 1), :]
        decay = jnp.where(incl, jnp.exp(jnp.where(incl, gc - gc_row, 0.0)), 0.0)
        kb = k * beta
        k_bf = k.astype(BF16)
        a_mat = jnp.where(strict, _dot_nt(kb.astype(BF16), k_bf) * decay, 0.0)
        a_pk = packed(a_mat)
        st.append(dict(
            cols=cols, gl=gl, inv=packed(eye) - a_pk, pw=a_pk.astype(BF16), pw_bd=a_mat.astype(BF16),
            rhs=jnp.concatenate([v * beta, kb * jnp.exp(gc)], axis=1).astype(BF16),
            intra=(_dot_nt(q.astype(BF16), k_bf) * decay).astype(BF16),
            q_dec=(q * jnp.exp(gc)).astype(BF16),
            k_dec_t=(k * jnp.exp(gl - gc)).T.astype(BF16),
            s=s_scr[hh], outs=[None] * n_chunks))
    for t in st:
        sq = _dot(t["pw"], t["pw_bd"])
        t["pw"], t["pw_bd"] = sq.astype(BF16), blockdiag(sq).astype(BF16)
    for level in range(n_levels):
        last = level == n_levels - 1
        for t in st:
            inv_bf = t["inv"].astype(BF16)
            both = _dot(inv_bf if last else jnp.concatenate([inv_bf, t["pw"]], axis=0), t["pw_bd"])
            t["inv"] = t["inv"] + both[:c_sz]
            if not last:
                t["pw"], t["pw_bd"] = both[c_sz:].astype(BF16), blockdiag(both[c_sz:]).astype(BF16)
    for t in st:
        uw = _dot(blockdiag(t["inv"]).astype(BF16), t["rhs"])
        t["u"], t["w"] = uw[:, :GDN_HEAD_DIM], uw[:, GDN_HEAD_DIM:].astype(BF16)
    for c in (range(n_chunks - 1, -1, -1) if rev else range(n_chunks)):
        rows = slice(c * c_sz, (c + 1) * c_sz)
        for t in st:
            t["s_bf"] = t["s"].astype(BF16)
            t["v_new"] = (t["u"][rows] - _dot(t["w"][rows], t["s_bf"])).astype(BF16)
        for t in st:
            t["outs"][c] = _dot(t["q_dec"][rows], t["s_bf"]) + _dot(t["intra"][rows, rows], t["v_new"])
            t["s"] = (t["s"] * jnp.exp(t["gl"][c * c_sz:c * c_sz + 1, :])
                      + _dot(t["k_dec_t"][:, rows], t["v_new"]))
    for hh, t in enumerate(st):
        s_scr[hh] = t["s"]
        o_ref[0, :, t["cols"]] = jnp.concatenate(t["outs"], axis=0)


def _gdn_scan(qkv, gates, gates_t, a_log_row, dt_row, a_log_col, dt_col, n_lat, rev):
    bsz, nt, _ = qkv.shape
    hps = GDN_HEADS_PER_STEP
    hblocks = GDN_HEADS // hps
    n_groups = nt // ROW_TILE
    lat_groups = n_lat // ROW_TILE
    width = hps * GDN_HEAD_DIM

    def group(i):
        lat = (lat_groups - i) if rev else (i - 1)
        return jnp.where(i == 0, n_groups - 1, lat)

    return pl.pallas_call(
        functools.partial(_gdn_body, rev=rev, n_lat_groups=lat_groups),
        grid=(bsz, hblocks, n_groups),
        in_specs=[pl.BlockSpec((1, ROW_TILE, width), lambda b, h, i: (b, group(i), h)),
                  pl.BlockSpec((1, ROW_TILE, width), lambda b, h, i: (b, group(i), hblocks + h)),
                  pl.BlockSpec((1, ROW_TILE, width), lambda b, h, i: (b, group(i), 2 * hblocks + h)),
                  pl.BlockSpec((1, ROW_TILE, LANES), lambda b, h, i: (b, group(i), 0)),
                  pl.BlockSpec((1, 4 * GDN_HEADS, ROW_TILE), lambda b, h, i: (b, 0, group(i))),
                  pl.BlockSpec((1, LANES), lambda b, h, i: (0, 0)),
                  pl.BlockSpec((1, LANES), lambda b, h, i: (0, 0)),
                  pl.BlockSpec((4 * GDN_HEADS, 1), lambda b, h, i: (0, 0)),
                  pl.BlockSpec((4 * GDN_HEADS, 1), lambda b, h, i: (0, 0))],
        out_specs=pl.BlockSpec((1, ROW_TILE, width), lambda b, h, i: (b, group(i), h)),
        out_shape=jax.ShapeDtypeStruct((bsz, nt, GDN_HEADS * GDN_HEAD_DIM), F32),
        scratch_shapes=[pltpu.VMEM((hps, GDN_HEAD_DIM, GDN_HEAD_DIM), F32),
                        pltpu.VMEM((4 * GDN_HEADS, ROW_TILE), F32)],
        compiler_params=_cparams("arbitrary", "arbitrary", "arbitrary"),
        name="gdn_scan_bwd" if rev else "gdn_scan_fwd",
    )(qkv, qkv, qkv, gates, gates_t, a_log_row, dt_row, a_log_col, dt_col)


def _gdn_out_body(of_ref, ob_ref, z_ref, gn_ref, w_ref, h_ref, mod_ref, g_ref, g_ffn_ref, wr_ref,
                  out_ref, u_ref, aff_ref, a_scr):
    for hh in range(GDN_HEADS):
        cols = slice(hh * GDN_HEAD_DIM, (hh + 1) * GDN_HEAD_DIM)
        o = of_ref[0, :, cols] + ob_ref[0, :, cols]
        a_scr[:, cols] = (_rms(o, gn_ref[...]) * _silu(z_ref[0, :, cols].astype(F32))).astype(BF16)
    y = _dot(a_scr[...], w_ref[...])
    _residual_and_route(h_ref[0], y, mod_ref[0, 0], g_ref, g_ffn_ref, wr_ref, out_ref, u_ref, aff_ref)


def _gdn_out(o_f, o_b, p, g_onorm, w_bf, h, mods, g_post, g_ffn, w_router, n_lat):
    bsz, _, d = h.shape
    width = GDN_HEADS * GDN_HEAD_DIM
    r_in, r_out, r_shape = _route_specs(bsz, n_lat, d)
    return pl.pallas_call(
        _gdn_out_body,
        grid=(bsz, n_lat // ROW_TILE),
        in_specs=[pl.BlockSpec((1, ROW_TILE, width), lambda b, i: (b, i, 0)),
                  pl.BlockSpec((1, ROW_TILE, width), lambda b, i: (b, i, 0)),
                  pl.BlockSpec((1, ROW_TILE, width), lambda b, i: (b, i, 3)),
                  pl.BlockSpec((1, GDN_HEAD_DIM), lambda b, i: (0, 0)),
                  pl.BlockSpec((width, d), lambda b, i: (0, 0)),
                  pl.BlockSpec((1, ROW_TILE, d), lambda b, i: (b, i, 0)),
                  pl.BlockSpec((1, 1, N_MOD, d), lambda b, i: (b, 0, 0, 0)),
                  pl.BlockSpec((1, d), lambda b, i: (0, 0))] + r_in,
        out_specs=r_out,
        out_shape=r_shape,
        scratch_shapes=[pltpu.VMEM((ROW_TILE, width), BF16)],
        compiler_params=_cparams("arbitrary", "arbitrary"),
        name="gdn_out",
    )(o_f, o_b, p, g_onorm[None, :], w_bf, h, mods, g_post[None, :], g_ffn[None, :], _pad_lanes(w_router))


def _gate_vectors(a_log_f, dt_bias_f, a_log_b, dt_bias_b):
    z = jnp.zeros((GDN_HEADS,), F32)
    a = jnp.concatenate([a_log_f, z, a_log_b, z])
    d = jnp.concatenate([dt_bias_f, z, dt_bias_b, z])
    pad = jnp.zeros((LANES - 4 * GDN_HEADS,), F32)
    return (jnp.concatenate([a, pad])[None, :], jnp.concatenate([d, pad])[None, :], a[:, None], d[:, None])


def kernel(x, c, ctx, c_ctx, l0_w_ada, l0_b_ada, l0_g_pre_mix, l0_g_post_mix, l0_g_pre_ffn, l0_g_post_ffn, l0_w_qkv, l0_lambda_q1, l0_lambda_k1, l0_lambda_q2, l0_lambda_k2, l0_g_subln, l0_w_o, l0_w_router, l0_w_gate, l0_w_up, l0_w_down, l1_w_ada, l1_b_ada, l1_g_pre_mix, l1_g_post_mix, l1_g_pre_ffn, l1_g_post_ffn, l1_w_in, l1_w_conv, l1_a_log_f, l1_dt_bias_f, l1_a_log_b, l1_dt_bias_b, l1_g_onorm, l1_w_o, l1_w_router, l1_w_gate, l1_w_up, l1_w_down):
    n_lat, n_ctx = x.shape[1], ctx.shape[1]
    h = jnp.concatenate([x, ctx], axis=1)

    mods = _ada_mods(c, c_ctx, l0_w_ada, l0_b_ada)
    qkv = _project(h, mods, l0_g_pre_mix, l0_w_qkv.astype(BF16), n_lat, rope_tables=_rope_tables(n_lat, n_ctx))
    lam_vecs = jnp.stack([l0_lambda_q1, l0_lambda_k1, l0_lambda_q2, l0_lambda_k2])
    o_lat, o_ctx = _diff_attention(qkv, lam_vecs, l0_g_subln, n_lat, depth=0)
    h, u, aff = _out_project(o_lat, o_ctx, l0_w_o.astype(BF16), h, mods, l0_g_post_mix, l0_g_pre_ffn, l0_w_router,
                             n_lat)
    h = _moe(h, u, aff, mods, l0_g_post_ffn, l0_w_gate, l0_w_up, l0_w_down, n_lat, n_ctx)

    mods = _ada_mods(c, c_ctx, l1_w_ada, l1_b_ada)
    width = GDN_HEADS * GDN_HEAD_DIM
    w_gates = jnp.pad(l1_w_in[:, 4 * width:], ((0, 0), (0, LANES - 4 * GDN_HEADS))).astype(BF16)
    p, gates = _project(h, mods, l1_g_pre_mix, l1_w_in[:, :4 * width].astype(BF16), n_lat, w_gates=w_gates)
    gates_t = jnp.swapaxes(gates[:, :, :4 * GDN_HEADS], 1, 2)
    qkv = _short_conv(p, l1_w_conv, n_lat)
    gv = _gate_vectors(l1_a_log_f, l1_dt_bias_f, l1_a_log_b, l1_dt_bias_b)
    o_f = _gdn_scan(qkv, gates, gates_t, *gv, n_lat, rev=False)
    o_b = _gdn_scan(qkv, gates, gates_t, *gv, n_lat, rev=True)
    hl, u, aff = _gdn_out(o_f, o_b, p, l1_g_onorm, l1_w_o.astype(BF16), h, mods, l1_g_post_mix, l1_g_pre_ffn,
                          l1_w_router, n_lat)
    return _moe(hl, u, aff, mods, l1_g_post_ffn, l1_w_gate, l1_w_up, l1_w_down, n_lat, 0)
```

```python
import functools
import math

import jax
import jax.numpy as jnp
from jax import lax
from jax.experimental import pallas as pl
from jax.experimental.pallas import tpu as pltpu

F32 = jnp.float32
BF16 = jnp.bfloat16
HIGHEST = lax.Precision.HIGHEST

EPS = 1e-6
N_MOD = 6
GRID_W = 64
ROPE_BASE = 10000.0
DA_HEADS = 8
DA_HEAD_DIM = 64
GDN_HEADS = 8
GDN_HEAD_DIM = 128
CONV_K = 5
CONV_HALO = 8
N_EXPERTS = 16
CAP_FACTOR = 2

LANES = 128
ROW_TILE = 256
PROJ_ROWS = 768
ATTN_Q_ROWS = 2048
ATTN_CHAIN_ROWS = 128
Q_PRESCALE = DA_HEAD_DIM ** -0.5 * math.log2(math.e)
GDN_CHUNK = 64
GDN_HEADS_PER_STEP = 8
FF_CHUNK = 1024
SLOT_WINDOW = 64
SLOT_ALIGN = 16
TABLE_COUNT = 64
VMEM_LIMIT = 56 * 1024 * 1024

SH_M, SC_M, GT_M, SH_F, SC_F, GT_F = range(6)


def _cparams(*sem):
    return pltpu.CompilerParams(dimension_semantics=sem, vmem_limit_bytes=VMEM_LIMIT)


def _dot(a, b):
    return jnp.dot(a, b, preferred_element_type=F32)


def _dot_nt(a, b, precision=None):
    return lax.dot_general(a, b, (((1,), (1,)), ((), ())), preferred_element_type=F32, precision=precision)


def _dot_tn(a, b):
    return lax.dot_general(a, b, (((0,), (0,)), ((), ())), preferred_element_type=F32)


def _rms(x, g):
    return x * lax.rsqrt(jnp.mean(x * x, axis=-1, keepdims=True) + EPS) * g


def _sigmoid(x):
    return 1.0 / (1.0 + jnp.exp(-x))


def _silu(x):
    return x * _sigmoid(x)


def _softplus(x):
    return jnp.maximum(x, 0.0) + jnp.log(1.0 + jnp.exp(-jnp.abs(x)))


def _ada_body(c_ref, w_ref, b_ref, o_ref):
    o_ref[...] = jnp.dot(_silu(c_ref[...]), w_ref[...], precision=HIGHEST, preferred_element_type=F32) + b_ref[...]


def _ada_mods(c, c_ctx, w_ada, b_ada):
    bsz, d = c.shape
    rows = 16
    cc = jnp.concatenate([c, c_ctx[None, :], jnp.zeros((rows - bsz - 1, d), F32)], axis=0)
    tn = 1024
    m = pl.pallas_call(
        _ada_body,
        grid=(N_MOD * d // tn,),
        in_specs=[pl.BlockSpec((rows, d), lambda j: (0, 0)),
                  pl.BlockSpec((d, tn), lambda j: (0, j)),
                  pl.BlockSpec((1, tn), lambda j: (0, j))],
        out_specs=pl.BlockSpec((rows, tn), lambda j: (0, j)),
        out_shape=jax.ShapeDtypeStruct((rows, N_MOD * d), F32),
        compiler_params=_cparams("arbitrary"),
        name="ada_mods",
    )(cc, w_ada, b_ada[None, :])
    lat = m[:bsz].reshape(bsz, 1, N_MOD, d)
    ctx = jnp.broadcast_to(m[bsz].reshape(1, 1, N_MOD, d), (bsz, 1, N_MOD, d))
    return jnp.concatenate([lat, ctx], axis=1)


def _proj_body(h_ref, mod_ref, g_ref, w_ref, *rest, n_lat, rope, gates):
    rest = list(rest)
    cos_ref, sin_ref = (rest.pop(0), rest.pop(0)) if rope else (None, None)
    wg_ref = rest.pop(0) if gates else None
    o_ref = rest.pop(0)
    og_ref = rest.pop(0) if gates else None
    x = h_ref[0]
    tm, d = x.shape
    row = pl.program_id(1) * tm + lax.broadcasted_iota(jnp.int32, (tm, 1), 0)
    is_ctx = row >= n_lat
    scale = jnp.where(is_ctx, mod_ref[0, 1, SC_M:SC_M + 1, :], mod_ref[0, 0, SC_M:SC_M + 1, :])
    shift = jnp.where(is_ctx, mod_ref[0, 1, SH_M:SH_M + 1, :], mod_ref[0, 0, SH_M:SH_M + 1, :])
    u = (_rms(x, g_ref[...]) * (1.0 + scale) + shift).astype(BF16)
    if gates:
        og_ref[0] = _dot(u, wg_ref[...])
    for j in range(w_ref.shape[1] // d):
        cols = slice(j * d, (j + 1) * d)
        acc = _dot(u, w_ref[:, cols])
        if rope and j < 2:
            reps = d // LANES
            cos = jnp.tile(cos_ref[...], (1, reps))
            sin = jnp.tile(sin_ref[...], (1, reps))
            lane = lax.broadcasted_iota(jnp.int32, acc.shape, 1)
            half = DA_HEAD_DIM // 4
            first = (lane % (2 * half)) < half
            partner = jnp.where(first, pltpu.roll(acc, d - half, 1), pltpu.roll(acc, half, 1))
            acc = acc * cos + partner * sin
            if j == 0:
                acc = acc * Q_PRESCALE
        o_ref[0, :, cols] = acc.astype(o_ref.dtype)


def _project(h, mods, g, w_bf, n_lat, rope_tables=None, w_gates=None):
    bsz, nt, d = h.shape
    n_out = w_bf.shape[1]
    tm = PROJ_ROWS
    assert nt % tm == 0 and n_lat % ROW_TILE == 0 and (nt - n_lat) == ROW_TILE
    rope = rope_tables is not None
    gates = w_gates is not None
    in_specs = [pl.BlockSpec((1, tm, d), lambda b, i: (b, i, 0)),
                pl.BlockSpec((1, 2, N_MOD, d), lambda b, i: (b, 0, 0, 0)),
                pl.BlockSpec((1, d), lambda b, i: (0, 0)),
                pl.BlockSpec((d, n_out), lambda b, i: (0, 0))]
    args = [h, mods, g[None, :], w_bf]
    out_specs = [pl.BlockSpec((1, tm, n_out), lambda b, i: (b, i, 0))]
    out_shape = [jax.ShapeDtypeStruct((bsz, nt, n_out), BF16)]
    if rope:
        in_specs += [pl.BlockSpec((tm, LANES), lambda b, i: (i, 0))] * 2
        args += list(rope_tables)
    if gates:
        in_specs.append(pl.BlockSpec((d, LANES), lambda b, i: (0, 0)))
        args.append(w_gates)
        out_specs.append(pl.BlockSpec((1, tm, LANES), lambda b, i: (b, i, 0)))
        out_shape.append(jax.ShapeDtypeStruct((bsz, nt, LANES), F32))
    out = pl.pallas_call(
        functools.partial(_proj_body, n_lat=n_lat, rope=rope, gates=gates),
        grid=(bsz, nt // tm),
        in_specs=in_specs,
        out_specs=out_specs,
        out_shape=out_shape,
        compiler_params=_cparams("arbitrary", "arbitrary"),
        name="mod_project",
    )(*args)
    return out if gates else out[0]


def _rope_tables(n_lat, n_ctx):
    rows = n_lat // GRID_W
    r = jnp.repeat(jnp.arange(rows), GRID_W).astype(F32)
    col = jnp.tile(jnp.arange(GRID_W), rows).astype(F32)
    half = DA_HEAD_DIM // 2
    inv = ROPE_BASE ** (-jnp.arange(0, half, 2, dtype=F32) / half)
    ang_r, ang_c = r[:, None] * inv, col[:, None] * inv
    cos = jnp.concatenate([jnp.cos(ang_r)] * 2 + [jnp.cos(ang_c)] * 2, axis=-1)
    sin = jnp.concatenate([-jnp.sin(ang_r), jnp.sin(ang_r), -jnp.sin(ang_c), jnp.sin(ang_c)], axis=-1)
    cos = jnp.concatenate([jnp.tile(cos, (1, 2)), jnp.ones((n_ctx, LANES), F32)], axis=0)
    sin = jnp.concatenate([jnp.tile(sin, (1, 2)), jnp.zeros((n_ctx, LANES), F32)], axis=0)
    return cos, sin


def _attn_body(lam_ref, q_ref, k_ref, v_ref, gs_ref, o_ref, *, lam_init):
    lv = lam_ref[...]
    lam = (jnp.exp(jnp.sum(lv[0:1] * lv[1:2], axis=-1, keepdims=True))
           - jnp.exp(jnp.sum(lv[2:3] * lv[3:4], axis=-1, keepdims=True)) + lam_init)
    hw = 2 * DA_HEAD_DIM
    n_q = q_ref.shape[1]
    lane = lax.broadcasted_iota(jnp.int32, (n_q, hw), 1)
    rows = min(n_q, ATTN_CHAIN_ROWS)
    chains = [(hh, r) for hh in range(q_ref.shape[2] // hw) for r in range(0, n_q, rows)]
    lhs, v_one = {}, {}
    for hh in range(q_ref.shape[2] // hw):
        cols = slice(hh * hw, (hh + 1) * hw)
        q, v = q_ref[0, :, cols], v_ref[0, :, cols]
        v_one[hh] = jnp.concatenate([v, jnp.ones_like(v)], axis=1)
        qm = [jnp.where((lane >= c * DA_HEAD_DIM) & (lane < (c + 1) * DA_HEAD_DIM), q, jnp.zeros_like(q))
              for c in range(2)]
        for r in range(0, n_q, rows):
            lhs[hh, r] = jnp.concatenate([qm[0][r:r + rows], qm[1][r:r + rows]], axis=0)
    s = [_dot_nt(lhs[hh, r], k_ref[0, :, hh * hw:(hh + 1) * hw]) for hh, r in chains]
    e = [jnp.exp2((x - jnp.max(x, axis=-1, keepdims=True)).astype(BF16)) for x in s]
    ov = [_dot(x, v_one[hh]) for x, (hh, r) in zip(e, chains)]
    att = [x[:, :hw] / x[:, hw:hw + 1] for x in ov]
    for (hh, r), a in zip(chains, att):
        o = a[:rows] - lam * a[rows:]
        o_ref[0, r:r + rows, hh * hw:(hh + 1) * hw] = (_rms(o, gs_ref[...]) * (1.0 - lam_init)).astype(o_ref.dtype)


def _diff_attention(qkv, lam_vecs, g_subln, n_lat, depth):
    bsz, nt, d3 = qkv.shape
    d = d3 // 3
    hw = 2 * DA_HEAD_DIM
    lam_init = 0.8 - 0.6 * math.exp(-0.3 * depth)
    n_ctx = nt - n_lat

    def call(n_q, tq, q_blk0, n_k, k_blk0, heads):
        hblocks = DA_HEADS // heads
        width = heads * hw
        return pl.pallas_call(
            functools.partial(_attn_body, lam_init=lam_init),
            grid=(bsz, hblocks, n_q // tq),
            in_specs=[pl.BlockSpec((4, DA_HEAD_DIM), lambda b, h, i: (0, 0)),
                      pl.BlockSpec((1, tq, width), lambda b, h, i: (b, q_blk0 + i, h)),
                      pl.BlockSpec((1, n_k, width), lambda b, h, i: (b, k_blk0, hblocks + h)),
                      pl.BlockSpec((1, n_k, width), lambda b, h, i: (b, k_blk0, 2 * hblocks + h)),
                      pl.BlockSpec((1, hw), lambda b, h, i: (0, 0))],
            out_specs=pl.BlockSpec((1, tq, width), lambda b, h, i: (b, i, h)),
            out_shape=jax.ShapeDtypeStruct((bsz, n_q, d), BF16),
            compiler_params=_cparams("arbitrary", "arbitrary", "arbitrary"),
            name="diff_attention",
        )(lam_vecs, qkv, qkv, qkv, g_subln[None, :])

    o_lat = call(n_lat, min(ATTN_Q_ROWS, n_lat), 0, nt, 0, heads=1)
    o_ctx = call(n_ctx, n_ctx, n_lat // n_ctx, n_ctx, n_lat // n_ctx, heads=DA_HEADS)
    return o_lat, o_ctx


def _residual_and_route(h, y, mod, g_post_ref, g_ffn_ref, wr_ref, out_ref, u_ref, aff_ref):
    h_new = h + mod[GT_M:GT_M + 1, :] * _rms(y, g_post_ref[...])
    out_ref[0] = h_new
    u = _rms(h_new, g_ffn_ref[...]) * (1.0 + mod[SC_F:SC_F + 1, :]) + mod[SH_F:SH_F + 1, :]
    u_hi = u.astype(BF16)
    u_ref[0] = u_hi
    u_lo = (u - u_hi.astype(F32)).astype(BF16)
    w = wr_ref[...]
    w_hi = w.astype(BF16)
    w_lo = (w - w_hi.astype(F32)).astype(BF16)
    logits = (_dot(u_hi, w_hi) + _dot(u_hi, w_lo) + _dot(u_lo, w_hi)).T[:N_EXPERTS]
    e = jnp.exp(logits - jnp.max(logits, axis=0, keepdims=True))
    aff_ref[0] = e / jnp.sum(e, axis=0, keepdims=True)


def _pad_lanes(w):
    return jnp.pad(w, ((0, 0), (0, LANES - w.shape[1])))


def _route_specs(bsz, nt, d):
    in_specs = [pl.BlockSpec((1, d), lambda b, i: (0, 0)),
                pl.BlockSpec((d, LANES), lambda b, i: (0, 0))]
    out_specs = [pl.BlockSpec((1, ROW_TILE, d), lambda b, i: (b, i, 0)),
                 pl.BlockSpec((1, ROW_TILE, d), lambda b, i: (b, i, 0)),
                 pl.BlockSpec((1, N_EXPERTS, ROW_TILE), lambda b, i: (b, 0, i))]
    out_shape = [jax.ShapeDtypeStruct((bsz, nt, d), F32),
                 jax.ShapeDtypeStruct((bsz, nt, d), BF16),
                 jax.ShapeDtypeStruct((bsz, N_EXPERTS, nt), F32)]
    return in_specs, out_specs, out_shape


def _oproj_body(o_lat_ref, o_ctx_ref, w_ref, h_ref, mod_ref, g_ref, g_ffn_ref, wr_ref, out_ref, u_ref, aff_ref,
                *, lat_tiles):
    o = jnp.where(pl.program_id(1) < lat_tiles, o_lat_ref[0], o_ctx_ref[0])
    y = _dot(o, w_ref[...])
    _residual_and_route(h_ref[0], y, mod_ref[0, 0], g_ref, g_ffn_ref, wr_ref, out_ref, u_ref, aff_ref)


def _out_project(o_lat, o_ctx, w_bf, h, mods, g_post, g_ffn, w_router, n_lat):
    bsz, nt, d = h.shape
    lat_tiles = n_lat // ROW_TILE
    r_in, r_out, r_shape = _route_specs(bsz, nt, d)
    return pl.pallas_call(
        functools.partial(_oproj_body, lat_tiles=lat_tiles),
        grid=(bsz, nt // ROW_TILE),
        in_specs=[pl.BlockSpec((1, ROW_TILE, d), lambda b, i: (b, jnp.minimum(i, lat_tiles - 1), 0)),
                  pl.BlockSpec((1, ROW_TILE, d), lambda b, i: (b, 0, 0)),
                  pl.BlockSpec((d, d), lambda b, i: (0, 0)),
                  pl.BlockSpec((1, ROW_TILE, d), lambda b, i: (b, i, 0)),
                  pl.BlockSpec((1, 1, N_MOD, d), lambda b, i: (b, i // lat_tiles, 0, 0)),
                  pl.BlockSpec((1, d), lambda b, i: (0, 0))] + r_in,
        out_specs=r_out,
        out_shape=r_shape,
        compiler_params=_cparams("arbitrary", "arbitrary"),
        name="out_project",
    )(o_lat, o_ctx, w_bf, h, mods, g_post[None, :], g_ffn[None, :], _pad_lanes(w_router))


def _lane_cumsum(x):
    n = x.shape[1]
    jj = lax.broadcasted_iota(jnp.int32, (ROW_TILE, ROW_TILE), 0)
    nn = lax.broadcasted_iota(jnp.int32, (ROW_TILE, ROW_TILE), 1)
    tri = (jj <= nn).astype(BF16)
    run = jnp.zeros((x.shape[0], 1), F32)
    parts = []
    for t in range(n // ROW_TILE):
        local = _dot(x[:, t * ROW_TILE:(t + 1) * ROW_TILE].astype(BF16), tri) + run
        parts.append(local)
        run = local[:, ROW_TILE - 1:ROW_TILE]
    return jnp.concatenate(parts, axis=1) if len(parts) > 1 else parts[0]


def _select_slots(aff, cap):
    bits = pltpu.bitcast(aff, jnp.int32)

    def enough(cand):
        return jnp.sum((bits >= cand).astype(F32), axis=1, keepdims=True) >= cap

    def step(t, lo):
        hi_bit = jnp.left_shift(jnp.int32(1), 29 - 2 * t)
        lo_bit = jnp.left_shift(jnp.int32(1), 28 - 2 * t)
        both, first, second = lo | hi_bit | lo_bit, lo | hi_bit, lo | lo_bit
        return jnp.where(enough(both), both, jnp.where(enough(first), first, jnp.where(enough(second), second, lo)))

    top = jnp.full((aff.shape[0], 1), 1 << 30, jnp.int32)
    thr = jnp.where(enough(top), top, 0)
    thr = lax.fori_loop(0, 15, step, thr)
    gt = bits > thr
    eq = bits == thr
    need = cap - jnp.sum(gt.astype(F32), axis=1, keepdims=True)
    eq_rank = _lane_cumsum(eq.astype(F32))
    sel = gt | (eq & (eq_rank <= need))
    sel_f = sel.astype(F32)
    slot = _lane_cumsum(sel_f) - 1.0
    return jnp.where(sel, slot, -1.0).astype(jnp.int32), sel_f


def _tile_table(sel_f, tile0):
    lane = lax.broadcasted_iota(jnp.int32, (sel_f.shape[0], LANES), 1)
    run = jnp.zeros((sel_f.shape[0], 1), F32)
    tab = jnp.zeros((sel_f.shape[0], LANES), F32)
    for t in range(sel_f.shape[1] // ROW_TILE):
        inside = jnp.sum(sel_f[:, t * ROW_TILE:(t + 1) * ROW_TILE], axis=1, keepdims=True)
        tab = jnp.where(lane == tile0 + t, run, tab)
        tab = jnp.where(lane == TABLE_COUNT + tile0 + t, inside, tab)
        run = run + inside
    return tab


def _select_body(aff_ref, pos_ref, tab_ref, *, n_lat, n_ctx):
    aff = aff_ref[0]
    pos, sel_f = _select_slots(aff[:, :n_lat], CAP_FACTOR * n_lat // N_EXPERTS)
    pos_ref[0, :, :n_lat] = pos
    tab = _tile_table(sel_f, 0)
    if n_ctx:
        pos, sel_f = _select_slots(aff[:, n_lat:], CAP_FACTOR * n_ctx // N_EXPERTS)
        pos_ref[0, :, n_lat:] = pos
        tab = tab + _tile_table(sel_f, n_lat // ROW_TILE)
    tab_ref[0] = tab.astype(jnp.int32)


def _select(aff, n_lat, n_ctx):
    bsz, e, nt = aff.shape
    n_tiles = nt // ROW_TILE
    pos, tab = pl.pallas_call(
        functools.partial(_select_body, n_lat=n_lat, n_ctx=n_ctx),
        grid=(bsz,),
        in_specs=[pl.BlockSpec((1, e, nt), lambda b: (b, 0, 0))],
        out_specs=[pl.BlockSpec((1, e, nt), lambda b: (b, 0, 0)),
                   pl.BlockSpec((1, e, LANES), lambda b: (b, 0, 0))],
        out_shape=[jax.ShapeDtypeStruct((bsz, e, nt), jnp.int32),
                   jax.ShapeDtypeStruct((bsz, e, LANES), jnp.int32)],
        compiler_params=_cparams("arbitrary"),
        name="moe_select",
    )(aff)
    first = tab[:, :, :n_tiles].reshape(-1)
    count = tab[:, :, TABLE_COUNT:TABLE_COUNT + n_tiles].reshape(-1)
    return pos, first, count


class _TileWindows:
    def __init__(self, first_ref, count_ref, pos_ref, n_lat, rows, lat_cap):
        b, tile = pl.program_id(0), pl.program_id(1)
        n_exp = pos_ref.shape[1]
        self.rows, self.n_exp = rows, n_exp
        base = jnp.where(tile * ROW_TILE >= n_lat, lat_cap, 0)
        pos = pos_ref[0]
        self.slot = jnp.where(pos >= 0, pos + base, -1)
        self.starts, self.rounds = [], 0
        for x in range(n_exp):
            at = (b * n_exp + x) * pl.num_programs(1) + tile
            first = first_ref[at]
            start = base + (first // SLOT_ALIGN) * SLOT_ALIGN
            self.starts.append(start)
            self.rounds = jnp.maximum(self.rounds, (base + first + count_ref[at] - start + SLOT_WINDOW - 1) // SLOT_WINDOW)
        self.ids0 = lax.broadcasted_iota(jnp.int32, (SLOT_WINDOW, ROW_TILE), 0)

    def window(self, x, r):
        want = self.starts[x] + r * SLOT_WINDOW
        s0 = jnp.minimum(want, self.rows - SLOT_WINDOW)
        ids = self.ids0 + s0
        hot = (self.slot[x:x + 1, :] == ids) & (ids >= want)
        return pl.multiple_of(x * self.rows + s0, SLOT_ALIGN), hot


def _as_bf16(mask):
    return jnp.where(mask, 1.0, 0.0).astype(BF16)


def _gather_body(first_ref, count_ref, pos_ref, aff_ref, u_ref, xs_ref, gate_ref, *, n_lat, rows, lat_cap):
    @pl.when(pl.program_id(1) == 0)
    def _():
        xs_ref[...] = jnp.zeros_like(xs_ref)
        gate_ref[...] = jnp.zeros_like(gate_ref)

    tw = _TileWindows(first_ref, count_ref, pos_ref, n_lat, rows, lat_cap)
    aff = aff_ref[0]
    u = u_ref[0]

    def one_round(r, carry):
        wins = [tw.window(x, r) for x in range(tw.n_exp)]
        part = _dot(jnp.concatenate([_as_bf16(hot) for _, hot in wins], axis=0), u)
        for x, (off, hot) in enumerate(wins):
            dst = pl.ds(off, SLOT_WINDOW)
            xs_ref[0, dst, :] = xs_ref[0, dst, :] + part[x * SLOT_WINDOW:(x + 1) * SLOT_WINDOW].astype(BF16)
            gate = jnp.sum(jnp.where(hot, aff[x:x + 1, :], 0.0), axis=1, keepdims=True)
            gate_ref[0, dst, :] = gate_ref[0, dst, :] + jnp.broadcast_to(gate, (SLOT_WINDOW, LANES))
        return carry

    lax.fori_loop(0, tw.rounds, one_round, 0)


def _slot_rows(n_lat, n_ctx):
    return CAP_FACTOR * n_lat // N_EXPERTS + CAP_FACTOR * n_ctx // N_EXPERTS


def _gather(pos, first, count, aff, u, n_lat, n_ctx):
    bsz, nt, d = u.shape
    e = pos.shape[1]
    rows = _slot_rows(n_lat, n_ctx)
    xs, gate = pl.pallas_call(
        functools.partial(_gather_body, n_lat=n_lat, rows=rows, lat_cap=CAP_FACTOR * n_lat // N_EXPERTS),
        grid_spec=pltpu.PrefetchScalarGridSpec(
            num_scalar_prefetch=2,
            grid=(bsz, nt // ROW_TILE),
            in_specs=[pl.BlockSpec((1, e, ROW_TILE), lambda b, i, *_: (b, 0, i)),
                      pl.BlockSpec((1, e, ROW_TILE), lambda b, i, *_: (b, 0, i)),
                      pl.BlockSpec((1, ROW_TILE, d), lambda b, i, *_: (b, i, 0))],
            out_specs=[pl.BlockSpec((1, e * rows, d), lambda b, i, *_: (b, 0, 0)),
                       pl.BlockSpec((1, e * rows, LANES), lambda b, i, *_: (b, 0, 0))]),
        out_shape=[jax.ShapeDtypeStruct((bsz, e * rows, d), BF16),
                   jax.ShapeDtypeStruct((bsz, e * rows, LANES), F32)],
        compiler_params=_cparams("arbitrary", "arbitrary"),
        name="moe_gather",
    )(first, count, pos, aff, u)
    return xs.reshape(bsz, e, rows, d), gate.reshape(bsz, e, rows, LANES)


def _expert_body(xs_ref, gate_ref, wg_hbm, wu_hbm, wd_hbm, y_ref, wg_scr, wu_scr, wd_scr,
                 wg_stage, wu_stage, wd_stage, sems):
    x, b = pl.program_id(0), pl.program_id(1)
    n_exp, n_chunks = pl.num_programs(0), pl.num_programs(1)
    rows_in, rows_mid = wg_stage.shape[0], wd_stage.shape[0]

    def chunk_copies(expert, chunk):
        return (pltpu.make_async_copy(wg_hbm.at[expert, pl.ds(chunk * rows_in, rows_in), :], wg_stage, sems.at[0]),
                pltpu.make_async_copy(wu_hbm.at[expert, pl.ds(chunk * rows_in, rows_in), :], wu_stage, sems.at[1]),
                pltpu.make_async_copy(wd_hbm.at[expert, pl.ds(chunk * rows_mid, rows_mid), :], wd_stage, sems.at[2]))

    def land(copy, chunk):
        at_in = pl.ds(pl.multiple_of(chunk * rows_in, SLOT_ALIGN), rows_in)
        at_mid = pl.ds(pl.multiple_of(chunk * rows_mid, SLOT_ALIGN), rows_mid)
        wg_scr[copy, at_in, :] = wg_stage[...].astype(BF16)
        wu_scr[copy, at_in, :] = wu_stage[...].astype(BF16)
        wd_scr[copy, at_mid, :] = wd_stage[...].astype(BF16)

    @pl.when((x == 0) & (b == 0))
    def _():
        def fetch(chunk, carry):
            copies = chunk_copies(0, chunk)
            for cp in copies:
                cp.start()
            for cp in copies:
                cp.wait()
            land(0, chunk)
            return carry

        lax.fori_loop(0, n_chunks, fetch, 0)

    def start_next(_, carry):
        for cp in chunk_copies(x + 1, b):
            cp.start()
        return carry

    lax.fori_loop(0, jnp.where(x + 1 < n_exp, 1, 0), start_next, 0)

    cur = x % 2
    xs = xs_ref[0, 0]
    acc = jnp.zeros((xs.shape[0], wd_scr.shape[2]), F32)
    for s in range(wg_scr.shape[2] // FF_CHUNK):
        cols = slice(s * FF_CHUNK, (s + 1) * FF_CHUNK)
        hid = _silu(_dot(xs, wg_scr[cur, :, cols])) * _dot(xs, wu_scr[cur, :, cols])
        acc = acc + _dot(hid.astype(BF16), wd_scr[cur, cols, :])
    y_ref[0, 0] = (acc * gate_ref[0, 0, :, 0:1]).astype(y_ref.dtype)

    @pl.when(x + 1 < n_exp)
    def _():
        for cp in chunk_copies(x + 1, b):
            cp.wait()
        land(1 - cur, b)


def _experts(xs, gate, wg, wu, wd):
    bsz, e, rows, d = xs.shape
    ff = wg.shape[2]
    return pl.pallas_call(
        _expert_body,
        grid=(e, bsz),
        in_specs=[pl.BlockSpec((1, 1, rows, d), lambda x, b: (b, x, 0, 0)),
                  pl.BlockSpec((1, 1, rows, LANES), lambda x, b: (b, x, 0, 0)),
                  pl.BlockSpec(memory_space=pl.ANY),
                  pl.BlockSpec(memory_space=pl.ANY),
                  pl.BlockSpec(memory_space=pl.ANY)],
        out_specs=pl.BlockSpec((1, 1, rows, d), lambda x, b: (b, x, 0, 0)),
        out_shape=jax.ShapeDtypeStruct((bsz, e, rows, d), BF16),
        scratch_shapes=[pltpu.VMEM((2, d, ff), BF16), pltpu.VMEM((2, d, ff), BF16), pltpu.VMEM((2, ff, d), BF16),
                        pltpu.VMEM((d // bsz, ff), F32), pltpu.VMEM((d // bsz, ff), F32),
                        pltpu.VMEM((ff // bsz, d), F32), pltpu.SemaphoreType.DMA((3,))],
        compiler_params=_cparams("arbitrary", "arbitrary"),
        name="moe_experts",
    )(xs, gate, wg, wu, wd)


def _combine_body(first_ref, count_ref, pos_ref, y_ref, h_ref, mod_ref, g_ref, out_ref, acc_scr,
                  *, n_lat, rows, lat_cap):
    tw = _TileWindows(first_ref, count_ref, pos_ref, n_lat, rows, lat_cap)
    acc_scr[...] = jnp.zeros_like(acc_scr)

    def one_round(r, carry):
        group = ROW_TILE // SLOT_WINDOW
        for g0 in range(0, tw.n_exp, group):
            wins = [tw.window(x, r) for x in range(g0, g0 + group)]
            hot = jnp.concatenate([_as_bf16(h) for _, h in wins], axis=0)
            rows_y = jnp.concatenate([y_ref[0, pl.ds(off, SLOT_WINDOW), :] for off, _ in wins], axis=0)
            acc_scr[...] += _dot_tn(hot, rows_y)
        return carry

    lax.fori_loop(0, tw.rounds, one_round, 0)
    out_ref[0] = h_ref[0] + mod_ref[0, 0, GT_F:GT_F + 1, :] * _rms(acc_scr[...], g_ref[...])


def _combine(pos, first, count, y, h, mods, g_post, n_lat, n_ctx):
    bsz, e, rows, d = y.shape
    nt = n_lat + n_ctx
    lat_tiles = n_lat // ROW_TILE
    return pl.pallas_call(
        functools.partial(_combine_body, n_lat=n_lat, rows=rows, lat_cap=CAP_FACTOR * n_lat // N_EXPERTS),
        grid_spec=pltpu.PrefetchScalarGridSpec(
            num_scalar_prefetch=2,
            grid=(bsz, nt // ROW_TILE),
            in_specs=[pl.BlockSpec((1, e, ROW_TILE), lambda b, i, *_: (b, 0, i)),
                      pl.BlockSpec((1, e * rows, d), lambda b, i, *_: (b, 0, 0)),
                      pl.BlockSpec((1, ROW_TILE, d), lambda b, i, *_: (b, i, 0)),
                      pl.BlockSpec((1, 1, N_MOD, d), lambda b, i, *_: (b, i // lat_tiles, 0, 0)),
                      pl.BlockSpec((1, d), lambda b, i, *_: (0, 0))],
            out_specs=pl.BlockSpec((1, ROW_TILE, d), lambda b, i, *_: (b, i, 0)),
            scratch_shapes=[pltpu.VMEM((ROW_TILE, d), F32)]),
        out_shape=jax.ShapeDtypeStruct((bsz, nt, d), F32),
        compiler_params=_cparams("arbitrary", "arbitrary"),
        name="moe_combine",
    )(first, count, pos, y.reshape(bsz, e * rows, d), h, mods, g_post[None, :])


def _moe(h, u, aff, mods, g_post, wg, wu, wd, n_lat, n_ctx):
    pos, first, count = _select(aff, n_lat, n_ctx)
    xs, gate = _gather(pos, first, count, aff, u, n_lat, n_ctx)
    y = _experts(xs, gate, wg, wu, wd)
    return _combine(pos, first, count, y, h, mods, g_post, n_lat, n_ctx)


def _conv_body(x_ref, w_ref, o_ref, pad_scr, *, n_lat):
    j = pl.program_id(1)
    nt = x_ref.shape[1]
    halo = CONV_HALO
    zeros = jnp.zeros((halo, LANES), F32)
    for lo, hi in ((0, n_lat), (n_lat, nt)):
        n = hi - lo
        if n == 0:
            continue
        pad_scr[0:halo, :] = zeros
        pad_scr[halo:halo + n, :] = x_ref[0, lo:hi, :].astype(F32)
        pad_scr[halo + n:2 * halo + n, :] = zeros
        acc = None
        for tap in range(CONV_K):
            start = halo + tap - CONV_K // 2
            term = pad_scr[start:start + n, :] * w_ref[tap:tap + 1, :]
            acc = term if acc is None else acc + term
        y = _silu(acc)
        inv_norm = lax.rsqrt(jnp.sum(y * y, axis=-1, keepdims=True) + EPS)
        factor = jnp.where(j < GDN_HEADS, inv_norm * GDN_HEAD_DIM ** -0.5,
                           jnp.where(j < 2 * GDN_HEADS, inv_norm, 1.0))
        o_ref[0, lo:hi, :] = (y * factor).astype(o_ref.dtype)


def _short_conv(p, w_conv, n_lat):
    bsz, nt, _ = p.shape
    nblk = 3 * GDN_HEADS
    return pl.pallas_call(
        functools.partial(_conv_body, n_lat=n_lat),
        grid=(bsz, nblk),
        in_specs=[pl.BlockSpec((1, nt, LANES), lambda b, j: (b, 0, j)),
                  pl.BlockSpec((CONV_K, LANES), lambda b, j: (0, j))],
        out_specs=pl.BlockSpec((1, nt, LANES), lambda b, j: (b, 0, j)),
        out_shape=jax.ShapeDtypeStruct((bsz, nt, nblk * LANES), BF16),
        scratch_shapes=[pltpu.VMEM((n_lat + 2 * CONV_HALO, LANES), F32)],
        compiler_params=_cparams("arbitrary", "arbitrary"),
        name="gdn_conv",
    )(p, w_conv)


def _gdn_body(q_ref, k_ref, v_ref, gate_ref, gate_t_ref, alog_ref, dt_ref, alog_t_ref, dt_t_ref,
              o_ref, s_scr, gct_scr, *, rev, n_lat_groups):
    hb = pl.program_id(1)
    step = pl.program_id(2)
    r_t = ROW_TILE
    c_sz = GDN_CHUNK
    goff = 2 * GDN_HEADS if rev else 0

    @pl.when(step == 0)
    def _():
        s_scr[...] = jnp.zeros_like(s_scr)

    ii = lax.broadcasted_iota(jnp.int32, (r_t, r_t), 0)
    jj = lax.broadcasted_iota(jnp.int32, (r_t, r_t), 1)
    same = (ii // c_sz) == (jj // c_sz)
    if rev:
        incl, strict = same & (jj >= ii), same & (jj > ii)
    else:
        incl, strict = same & (jj <= ii), same & (jj < ii)
    m_incl = incl.astype(BF16)
    m_same = same.astype(BF16)
    eye = (ii == jj).astype(F32)

    def pieces(x):
        hi = x.astype(BF16)
        rest = x - hi.astype(F32)
        mid = rest.astype(BF16)
        return hi, mid, (rest - mid.astype(F32)).astype(BF16)

    lane = lax.broadcasted_iota(jnp.int32, (1, LANES), 1)
    is_g = (lane >= goff) & (lane < goff + GDN_HEADS)
    raw = gate_ref[0]
    g_all = jnp.where(is_g, -jnp.exp(alog_ref[...]) * _softplus(raw + dt_ref[...]), 0.0)
    beta_all = _sigmoid(raw)
    g_parts = pieces(g_all)
    gc_all = sum(_dot(m_incl, x) for x in g_parts)
    gl_all = sum(_dot(m_same, x) for x in g_parts)
    g_t = -jnp.exp(alog_t_ref[...]) * _softplus(gate_t_ref[0] + dt_t_ref[...])
    gct_scr[...] = sum(_dot_nt(x, m_incl) for x in pieces(g_t))
    lane_f = lax.broadcasted_iota(jnp.int32, (r_t, LANES), 1)

    def packed(m):
        return sum(m[c * c_sz:(c + 1) * c_sz] for c in range(r_t // c_sz))

    def blockdiag(p):
        return jnp.where(same, jnp.tile(p, (r_t // c_sz, 1)), 0.0)

    def column(a, idx):
        return jnp.sum(jnp.where(lane_f == idx, a, 0.0), axis=1, keepdims=True)

    heads = range(GDN_HEADS_PER_STEP)
    n_chunks = r_t // c_sz
    n_levels = int(math.log2(c_sz)) - 1
    st = []
    for hh in heads:
        head = hb * GDN_HEADS_PER_STEP + hh
        cols = slice(hh * GDN_HEAD_DIM, (hh + 1) * GDN_HEAD_DIM)
        q_bf, k_bf = q_ref[0, :, cols], k_ref[0, :, cols]
        q, k, v = q_bf.astype(F32), k_bf.astype(F32), v_ref[0, :, cols].astype(F32)
        gc = column(gc_all, goff + head)
        gl = column(gl_all, goff + head)
        beta = column(beta_all, goff + GDN_HEADS + head)
        gc_row = gct_scr[pl.ds(goff + head, 1), :]
        decay = jnp.where(incl, jnp.exp(jnp.where(incl, gc - gc_row, 0.0)), 0.0)
        kb = k * beta
        a_mat = jnp.where(strict, _dot_nt(kb.astype(BF16), k_bf) * decay, 0.0)
        a_pk = packed(a_mat)
        st.append(dict(
            cols=cols, gl=gl, inv=packed(eye) - a_pk, pw=a_pk.astype(BF16), pw_bd=a_mat.astype(BF16),
            rhs=jnp.concatenate([v * beta, kb * jnp.exp(gc)], axis=1).astype(BF16),
            intra=(_dot_nt(q_bf, k_bf) * decay).astype(BF16),
            q_dec=(q * jnp.exp(gc)).astype(BF16),
            k_dec_t=(k * jnp.exp(gl - gc)).T.astype(BF16),
            s=s_scr[hh], outs=[None] * n_chunks))
    for t in st:
        sq = _dot(t["pw"], t["pw_bd"])
        t["pw"], t["pw_bd"] = sq.astype(BF16), blockdiag(sq).astype(BF16)
    for level in range(n_levels):
        last = level == n_levels - 1
        for t in st:
            inv_bf = t["inv"].astype(BF16)
            both = _dot(inv_bf if last else jnp.concatenate([inv_bf, t["pw"]], axis=0), t["pw_bd"])
            t["inv"] = t["inv"] + both[:c_sz]
            if not last:
                t["pw"], t["pw_bd"] = both[c_sz:].astype(BF16), blockdiag(both[c_sz:]).astype(BF16)
    for t in st:
        uw = _dot(blockdiag(t["inv"]).astype(BF16), t["rhs"])
        t["u"], t["w"] = uw[:, :GDN_HEAD_DIM], uw[:, GDN_HEAD_DIM:].astype(BF16)
    for c in (range(n_chunks - 1, -1, -1) if rev else range(n_chunks)):
        rows = slice(c * c_sz, (c + 1) * c_sz)
        for t in st:
            t["s_bf"] = t["s"].astype(BF16)
            t["v_new"] = (t["u"][rows] - _dot(t["w"][rows], t["s_bf"])).astype(BF16)
        for t in st:
            t["outs"][c] = _dot(t["q_dec"][rows], t["s_bf"]) + _dot(t["intra"][rows, rows], t["v_new"])
            t["s"] = (t["s"] * jnp.exp(t["gl"][c * c_sz:c * c_sz + 1, :])
                      + _dot(t["k_dec_t"][:, rows], t["v_new"]))
    for hh, t in enumerate(st):
        s_scr[hh] = t["s"]
        o_ref[0, :, t["cols"]] = jnp.concatenate(t["outs"], axis=0)


def _gdn_scan(qkv, gates, gates_t, a_log_row, dt_row, a_log_col, dt_col, n_lat, rev):
    bsz, nt, _ = qkv.shape
    hps = GDN_HEADS_PER_STEP
    hblocks = GDN_HEADS // hps
    n_groups = nt // ROW_TILE
    lat_groups = n_lat // ROW_TILE
    width = hps * GDN_HEAD_DIM

    def group(i):
        lat = (lat_groups - i) if rev else (i - 1)
        return jnp.where(i == 0, n_groups - 1, lat)

    return pl.pallas_call(
        functools.partial(_gdn_body, rev=rev, n_lat_groups=lat_groups),
        grid=(bsz, hblocks, n_groups),
        in_specs=[pl.BlockSpec((1, ROW_TILE, width), lambda b, h, i: (b, group(i), h)),
                  pl.BlockSpec((1, ROW_TILE, width), lambda b, h, i: (b, group(i), hblocks + h)),
                  pl.BlockSpec((1, ROW_TILE, width), lambda b, h, i: (b, group(i), 2 * hblocks + h)),
                  pl.BlockSpec((1, ROW_TILE, LANES), lambda b, h, i: (b, group(i), 0)),
                  pl.BlockSpec((1, 4 * GDN_HEADS, ROW_TILE), lambda b, h, i: (b, 0, group(i))),
                  pl.BlockSpec((1, LANES), lambda b, h, i: (0, 0)),
                  pl.BlockSpec((1, LANES), lambda b, h, i: (0, 0)),
                  pl.BlockSpec((4 * GDN_HEADS, 1), lambda b, h, i: (0, 0)),
                  pl.BlockSpec((4 * GDN_HEADS, 1), lambda b, h, i: (0, 0))],
        out_specs=pl.BlockSpec((1, ROW_TILE, width), lambda b, h, i: (b, group(i), h)),
        out_shape=jax.ShapeDtypeStruct((bsz, nt, GDN_HEADS * GDN_HEAD_DIM), F32),
        scratch_shapes=[pltpu.VMEM((hps, GDN_HEAD_DIM, GDN_HEAD_DIM), F32),
                        pltpu.VMEM((4 * GDN_HEADS, ROW_TILE), F32)],
        compiler_params=_cparams("arbitrary", "arbitrary", "arbitrary"),
        name="gdn_scan_bwd" if rev else "gdn_scan_fwd",
    )(qkv, qkv, qkv, gates, gates_t, a_log_row, dt_row, a_log_col, dt_col)


def _gdn_out_body(of_ref, ob_ref, z_ref, gn_ref, w_ref, h_ref, mod_ref, g_ref, g_ffn_ref, wr_ref,
                  out_ref, u_ref, aff_ref, a_scr):
    for hh in range(GDN_HEADS):
        cols = slice(hh * GDN_HEAD_DIM, (hh + 1) * GDN_HEAD_DIM)
        o = of_ref[0, :, cols] + ob_ref[0, :, cols]
        a_scr[:, cols] = (_rms(o, gn_ref[...]) * _silu(z_ref[0, :, cols].astype(F32))).astype(BF16)
    y = _dot(a_scr[...], w_ref[...])
    _residual_and_route(h_ref[0], y, mod_ref[0, 0], g_ref, g_ffn_ref, wr_ref, out_ref, u_ref, aff_ref)


def _gdn_out(o_f, o_b, p, g_onorm, w_bf, h, mods, g_post, g_ffn, w_router, n_lat):
    bsz, _, d = h.shape
    width = GDN_HEADS * GDN_HEAD_DIM
    r_in, r_out, r_shape = _route_specs(bsz, n_lat, d)
    return pl.pallas_call(
        _gdn_out_body,
        grid=(bsz, n_lat // ROW_TILE),
        in_specs=[pl.BlockSpec((1, ROW_TILE, width), lambda b, i: (b, i, 0)),
                  pl.BlockSpec((1, ROW_TILE, width), lambda b, i: (b, i, 0)),
                  pl.BlockSpec((1, ROW_TILE, width), lambda b, i: (b, i, 3)),
                  pl.BlockSpec((1, GDN_HEAD_DIM), lambda b, i: (0, 0)),
                  pl.BlockSpec((width, d), lambda b, i: (0, 0)),
                  pl.BlockSpec((1, ROW_TILE, d), lambda b, i: (b, i, 0)),
                  pl.BlockSpec((1, 1, N_MOD, d), lambda b, i: (b, 0, 0, 0)),
                  pl.BlockSpec((1, d), lambda b, i: (0, 0))] + r_in,
        out_specs=r_out,
        out_shape=r_shape,
        scratch_shapes=[pltpu.VMEM((ROW_TILE, width), BF16)],
        compiler_params=_cparams("arbitrary", "arbitrary"),
        name="gdn_out",
    )(o_f, o_b, p, g_onorm[None, :], w_bf, h, mods, g_post[None, :], g_ffn[None, :], _pad_lanes(w_router))


def _gate_vectors(a_log_f, dt_bias_f, a_log_b, dt_bias_b):
    z = jnp.zeros((GDN_HEADS,), F32)
    a = jnp.concatenate([a_log_f, z, a_log_b, z])
    d = jnp.concatenate([dt_bias_f, z, dt_bias_b, z])
    pad = jnp.zeros((LANES - 4 * GDN_HEADS,), F32)
    return (jnp.concatenate([a, pad])[None, :], jnp.concatenate([d, pad])[None, :], a[:, None], d[:, None])


def kernel(x, c, ctx, c_ctx, l0_w_ada, l0_b_ada, l0_g_pre_mix, l0_g_post_mix, l0_g_pre_ffn, l0_g_post_ffn, l0_w_qkv, l0_lambda_q1, l0_lambda_k1, l0_lambda_q2, l0_lambda_k2, l0_g_subln, l0_w_o, l0_w_router, l0_w_gate, l0_w_up, l0_w_down, l1_w_ada, l1_b_ada, l1_g_pre_mix, l1_g_post_mix, l1_g_pre_ffn, l1_g_post_ffn, l1_w_in, l1_w_conv, l1_a_log_f, l1_dt_bias_f, l1_a_log_b, l1_dt_bias_b, l1_g_onorm, l1_w_o, l1_w_router, l1_w_gate, l1_w_up, l1_w_down):
    n_lat, n_ctx = x.shape[1], ctx.shape[1]
    h = jnp.concatenate([x, ctx], axis=1)

    mods = _ada_mods(c, c_ctx, l0_w_ada, l0_b_ada)
    qkv = _project(h, mods, l0_g_pre_mix, l0_w_qkv.astype(BF16), n_lat, rope_tables=_rope_tables(n_lat, n_ctx))
    lam_vecs = jnp.stack([l0_lambda_q1, l0_lambda_k1, l0_lambda_q2, l0_lambda_k2])
    o_lat, o_ctx = _diff_attention(qkv, lam_vecs, l0_g_subln, n_lat, depth=0)
    h, u, aff = _out_project(o_lat, o_ctx, l0_w_o.astype(BF16), h, mods, l0_g_post_mix, l0_g_pre_ffn, l0_w_router,
                             n_lat)
    h = _moe(h, u, aff, mods, l0_g_post_ffn, l0_w_gate, l0_w_up, l0_w_down, n_lat, n_ctx)

    mods = _ada_mods(c, c_ctx, l1_w_ada, l1_b_ada)
    width = GDN_HEADS * GDN_HEAD_DIM
    w_gates = jnp.pad(l1_w_in[:, 4 * width:], ((0, 0), (0, LANES - 4 * GDN_HEADS))).astype(BF16)
    p, gates = _project(h, mods, l1_g_pre_mix, l1_w_in[:, :4 * width].astype(BF16), n_lat, w_gates=w_gates)
    gates_t = jnp.swapaxes(gates[:, :, :4 * GDN_HEADS], 1, 2)
    qkv = _short_conv(p, l1_w_conv, n_lat)
    gv = _gate_vectors(l1_a_log_f, l1_dt_bias_f, l1_a_log_b, l1_dt_bias_b)
    o_f = _gdn_scan(qkv, gates, gates_t, *gv, n_lat, rev=False)
    o_b = _gdn_scan(qkv, gates, gates_t, *gv, n_lat, rev=True)
    hl, u, aff = _gdn_out(o_f, o_b, p, l1_g_onorm, l1_w_o.astype(BF16), h, mods, l1_g_post_mix, l1_g_pre_ffn,
                          l1_w_router, n_lat)
    return _moe(hl, u, aff, mods, l1_g_post_ffn, l1_w_gate, l1_w_up, l1_w_down, n_lat, 0)
```
